```python
import jax
import jax.numpy as jnp
from jax import lax
import numpy as np

D_MODEL = 1024
BATCH = 8
SEQ = 2048
DEPTH = 4

N_MIXERS = 3
N_RWKV = (DEPTH + 2) // N_MIXERS
N_RET = (DEPTH + 1) // N_MIXERS
N_MOBA = DEPTH // N_MIXERS
RWKV_HEAD = 64
RWKV_HEADS = D_MODEL // RWKV_HEAD
RWKV_DECAY_LORA = 64
RWKV_AAA_LORA = 64
RWKV_GATE_LORA = 160
RWKV_GN_EPS = 64e-5
RET_HEADS = 4
RET_QK_DIM = D_MODEL // RET_HEADS
RET_V_DIM = 2 * D_MODEL // RET_HEADS
RET_CHUNK = 128
RET_ROPE_BASE = 10000.0
RET_GN_EPS = 1e-5
MOBA_HEADS = 16
MOBA_HEAD_DIM = D_MODEL // MOBA_HEADS
MOBA_BLOCK = 256
MOBA_TOPK = 3
MOBA_Q_CHUNK = 16
FFN_DIM = 2816
CONV_WIDTH = 3
LN_EPS = 1e-5
DEEPNORM_ALPHA = (2 * DEPTH) ** 0.25
DEEPNORM_BETA = (8 * DEPTH) ** -0.25

kernel_name = 'hybrid_rwkv7_retnet_moba_convffn'


def _layer_norm(x, g, b):
    xf = x.astype(jnp.float32)
    mu = jnp.mean(xf, -1, keepdims=True)
    var = jnp.mean(jnp.square(xf - mu), -1, keepdims=True)
    return ((xf - mu) * lax.rsqrt(var + LN_EPS) * g + b).astype(x.dtype)


def _head_norm(y, g, b, eps):
    yf = y.astype(jnp.float32)
    mu = jnp.mean(yf, -1, keepdims=True)
    var = jnp.mean(jnp.square(yf - mu), -1, keepdims=True)
    yn = ((yf - mu) * lax.rsqrt(var + eps)).reshape(*y.shape[:-2], -1)
    return yn * g + b


def _rwkv7_time_mix(x, mix, w_rkv, w0, w1, w2, a0, a1, a2, g1, g2, k_k, k_a, r_k, gn_g, gn_b, w_o):
    bsz, t_len, d = x.shape
    h, n = RWKV_HEADS, RWKV_HEAD
    f32 = jnp.float32
    xx = jnp.pad(x, ((0, 0), (1, 0), (0, 0)))[:, :-1] - x
    xr, xw, xk, xv, xa, xg = (x + xx * mix[i] for i in range(6))
    r = xr @ w_rkv[0]
    k = xk @ w_rkv[1]
    v = xv @ w_rkv[2]
    logw = -jax.nn.softplus(-(w0 + jnp.tanh(xw @ w1) @ w2).astype(f32)) - 0.5
    decay = jnp.exp(-jnp.exp(logw))
    a = jax.nn.sigmoid((a0 + (xa @ a1) @ a2).astype(f32))
    gate = jax.nn.sigmoid(xg @ g1) @ g2
    heads = lambda t: t.reshape(bsz, t_len, h, n).astype(f32)
    kk = heads(k * k_k)
    kk = kk / jnp.maximum(jnp.sqrt(jnp.sum(kk * kk, -1, keepdims=True)), 1e-12)
    k = k * (1.0 + (a - 1.0) * k_a)
    r_h, k_h, v_h, w_h, a_h = heads(r), heads(k), heads(v), heads(decay), heads(a)

    def step(state, inp):
        r_t, w_t, k_t, v_t, kk_t, a_t = inp
        sa = jnp.einsum('bhij,bhj->bhi', state, -kk_t)
        state = (state * w_t[:, :, None, :]
                 + sa[..., None] * (kk_t * a_t)[:, :, None, :]
                 + v_t[..., None] * k_t[:, :, None, :])
        return state, jnp.einsum('bhij,bhj->bhi', state, r_t)

    xs = tuple(jnp.moveaxis(t, 1, 0) for t in (r_h, w_h, k_h, v_h, kk, a_h))
    s0 = jnp.zeros((bsz, h, n, n), f32)
    _, y = lax.scan(step, s0, xs)
    y = jnp.moveaxis(y, 0, 1)
    y = _head_norm(y, gn_g, gn_b, RWKV_GN_EPS)
    bonus = (jnp.sum(r_h * k_h * r_k, -1, keepdims=True) * v_h).reshape(bsz, t_len, d)
    return ((y + bonus) * gate).astype(x.dtype) @ w_o


def _rotate_every_two(x, pos):
    d = x.shape[-1]
    inv = 1.0 / (RET_ROPE_BASE ** jnp.linspace(0.0, 1.0, d // 2, dtype=jnp.float32))
    ang = pos[:, None].astype(jnp.float32) * inv[None, :]
    cos = jnp.cos(ang)[None, :, None, :]
    sin = jnp.sin(ang)[None, :, None, :]
    xf = x.astype(jnp.float32).reshape(*x.shape[:-1], d // 2, 2)
    x1, x2 = xf[..., 0], xf[..., 1]
    return jnp.stack([x1 * cos - x2 * sin, x2 * cos + x1 * sin], -1).reshape(x.shape)


def _retention(x, w_in, gn_g, gn_b, w_o):
    bsz, t_len, d = x.shape
    h, dk, dv, c = RET_HEADS, RET_QK_DIM, RET_V_DIM, RET_CHUNK
    n_chunks = t_len // c
    f32 = jnp.float32
    proj = x @ w_in
    q, k, v, gate = jnp.split(proj, [d, 2 * d, 4 * d], axis=-1)
    pos = jnp.arange(t_len)
    q = _rotate_every_two(q.reshape(bsz, t_len, h, dk), pos)
    k = _rotate_every_two(k.reshape(bsz, t_len, h, dk), pos) * (dk ** -0.5)
    v = v.reshape(bsz, t_len, h, dv).astype(f32)
    log_gamma = jnp.log(1.0 - 2.0 ** (-5.0 - jnp.arange(h, dtype=f32)))
    idx = jnp.arange(c, dtype=f32)
    rel = idx[:, None] - idx[None, :]
    inner_decay = jnp.where(rel >= 0, jnp.exp(log_gamma[:, None, None] * jnp.maximum(rel, 0.0)), 0.0)
    cross_decay = jnp.exp(log_gamma[:, None] * (idx + 1.0))[None, :, :, None]
    state_decay = jnp.exp(log_gamma[:, None] * (c - 1.0 - idx))[None, :, :, None]
    chunk_decay = jnp.exp(log_gamma * c)[None, :, None, None]
    to_chunks = lambda t: t.reshape(bsz, n_chunks, c, h, t.shape[-1]).transpose(1, 0, 3, 2, 4)

    def step(state, inp):
        q_c, k_c, v_c = inp
        s = jnp.einsum('bhnd,bhmd->bhnm', q_c, k_c) * inner_decay
        o = (jnp.einsum('bhnm,bhme->bhne', s, v_c)
             + jnp.einsum('bhnd,bhde->bhne', q_c, state) * cross_decay)
        state = state * chunk_decay + jnp.einsum('bhmd,bhme->bhde', k_c * state_decay, v_c)
        return state, o

    s0 = jnp.zeros((bsz, h, dk, dv), f32)
    _, o = lax.scan(step, s0, (to_chunks(q), to_chunks(k), to_chunks(v)))
    o = o.transpose(1, 0, 3, 2, 4).reshape(bsz, t_len, h, dv)
    o = _head_norm(o, gn_g, gn_b, RET_GN_EPS)
    return (jax.nn.silu(gate.astype(f32)) * o).astype(x.dtype) @ w_o


def _moba_attention(x, w_qkv, w_o):
    bsz, t_len, d = x.shape
    h, hd, blk, qc = MOBA_HEADS, MOBA_HEAD_DIM, MOBA_BLOCK, MOBA_Q_CHUNK
    f32 = jnp.float32
    q, k, v = (t.reshape(bsz, t_len, h, hd).transpose(0, 2, 1, 3) for t in jnp.split(x @ w_qkv, 3, axis=-1))
    n_blk = -(-t_len // blk)
    pad = ((0, 0), (0, 0), (0, n_blk * blk - t_len), (0, 0))
    k_pad, v_pad = jnp.pad(k, pad), jnp.pad(v, pad)
    kb = k_pad.reshape(bsz, h, n_blk, blk, hd)
    vb = v_pad.reshape(bsz, h, n_blk, blk, hd)
    k_mean = jnp.mean(kb.astype(f32), axis=3)
    q_blk = jnp.arange(t_len) // blk
    gate = jnp.einsum('bhtd,bhnd->bhtn', q.astype(f32), k_mean)
    past = jnp.arange(n_blk)[None, :] < q_blk[:, None]
    gate = jnp.where(past, gate, -jnp.inf)
    topk = min(MOBA_TOPK, n_blk)
    _, sel = lax.top_k(gate, topk)
    sel_valid = jnp.arange(topk)[None, :] < q_blk[:, None]
    scale = hd ** -0.5
    bi = jnp.arange(bsz)[:, None, None, None]
    hi = jnp.arange(h)[None, :, None, None]

    def chunk(ci):
        t0 = ci * qc
        q_c = lax.dynamic_slice_in_dim(q, t0, qc, axis=2).astype(f32)
        sel_c = lax.dynamic_slice_in_dim(sel, t0, qc, axis=2)
        valid_c = lax.dynamic_slice_in_dim(sel_valid, t0, qc, axis=0)
        k_sel = kb[bi, hi, sel_c]
        v_sel = vb[bi, hi, sel_c]
        s_sel = jnp.einsum('bhqd,bhqskd->bhqsk', q_c, k_sel) * scale
        s_sel = jnp.where(valid_c[None, None, :, :, None], s_sel, -jnp.inf)
        own0 = (t0 // blk) * blk
        k_own = lax.dynamic_slice_in_dim(k_pad, own0, blk, axis=2)
        v_own = lax.dynamic_slice_in_dim(v_pad, own0, blk, axis=2)
        s_own = jnp.einsum('bhqd,bhkd->bhqk', q_c, k_own) * scale
        causal = (own0 + jnp.arange(blk))[None, :] <= (t0 + jnp.arange(qc))[:, None]
        s_own = jnp.where(causal, s_own, -jnp.inf)
        logits = jnp.concatenate([s_sel.reshape(bsz, h, qc, topk * blk), s_own], axis=-1)
        p = jax.nn.softmax(logits, axis=-1)
        p_sel = p[..., :topk * blk].reshape(bsz, h, qc, topk, blk)
        p_own = p[..., topk * blk:]
        return (jnp.einsum('bhqsk,bhqskd->bhqd', p_sel, v_sel)
                + jnp.einsum('bhqk,bhkd->bhqd', p_own, v_own))

    o = lax.map(chunk, jnp.arange(t_len // qc))
    o = o.transpose(1, 0, 3, 2, 4).reshape(bsz, t_len, d)
    return o.astype(x.dtype) @ w_o


def _conv_ffn(x, w_in, conv_w, conv_b, w_out):
    hdn = x @ w_in
    hdn = lax.conv_general_dilated(
        hdn, conv_w[:, None, :].astype(hdn.dtype), window_strides=(1,),
        padding=[(CONV_WIDTH - 1, 0)], dimension_numbers=('NWC', 'WIO', 'NWC'),
        feature_group_count=hdn.shape[-1]) + conv_b
    u, g = jnp.split(hdn, 2, axis=-1)
    return (jax.nn.silu(g) * u) @ w_out


def setup_inputs(seed: int = 0) -> dict:
    key = jax.random.key(seed)
    keys = iter(jax.random.split(key, 40))
    f32 = jnp.float32
    nrm = lambda shape, s: jax.random.normal(next(keys), shape, f32) * s
    uni = lambda shape, lo, hi: jax.random.uniform(next(keys), shape, f32, lo, hi)
    d, ff = D_MODEL, FFN_DIM
    return {
        'x': nrm((BATCH, SEQ, d), 1.0),
        'rwkv_mix': uni((N_RWKV, 6, d), 0.0, 1.0),
        'rwkv_w_rkv': nrm((N_RWKV, 3, d, d), d ** -0.5),
        'rwkv_w0': uni((N_RWKV, d), -5.0, 0.0),
        'rwkv_w1': nrm((N_RWKV, d, RWKV_DECAY_LORA), d ** -0.5),
        'rwkv_w2': nrm((N_RWKV, RWKV_DECAY_LORA, d), 0.1 * RWKV_DECAY_LORA ** -0.5),
        'rwkv_a0': nrm((N_RWKV, d), 0.5),
        'rwkv_a1': nrm((N_RWKV, d, RWKV_AAA_LORA), d ** -0.5),
        'rwkv_a2': nrm((N_RWKV, RWKV_AAA_LORA, d), 0.1 * RWKV_AAA_LORA ** -0.5),
        'rwkv_g1': nrm((N_RWKV, d, RWKV_GATE_LORA), d ** -0.5),
        'rwkv_g2': nrm((N_RWKV, RWKV_GATE_LORA, d), RWKV_GATE_LORA ** -0.5),
        'rwkv_k_k': 0.85 + nrm((N_RWKV, d), 0.05),
        'rwkv_k_a': 1.0 + nrm((N_RWKV, d), 0.05),
        'rwkv_r_k': nrm((N_RWKV, RWKV_HEADS, RWKV_HEAD), 0.1),
        'rwkv_gn_g': 1.0 + nrm((N_RWKV, d), 0.02),
        'rwkv_gn_b': nrm((N_RWKV, d), 0.02),
        'rwkv_w_o': nrm((N_RWKV, d, d), DEEPNORM_BETA * d ** -0.5),
        'ret_w_in': nrm((N_RET, d, 6 * d), d ** -0.5),
        'ret_gn_g': 1.0 + nrm((N_RET, 2 * d), 0.02),
        'ret_gn_b': nrm((N_RET, 2 * d), 0.02),
        'ret_w_o': nrm((N_RET, 2 * d, d), DEEPNORM_BETA * (2 * d) ** -0.5),
        'moba_w_qkv': nrm((N_MOBA, d, 3 * d), d ** -0.5),
        'moba_w_o': nrm((N_MOBA, d, d), DEEPNORM_BETA * d ** -0.5),
        'ffn_w_in': nrm((DEPTH, d, 2 * ff), d ** -0.5),
        'ffn_conv_w': nrm((DEPTH, CONV_WIDTH, 2 * ff), CONV_WIDTH ** -0.5),
        'ffn_conv_b': nrm((DEPTH, 2 * ff), 0.02),
        'ffn_w_out': nrm((DEPTH, ff, d), DEEPNORM_BETA * ff ** -0.5),
        'ln1_g': 1.0 + nrm((DEPTH, d), 0.02),
        'ln1_b': nrm((DEPTH, d), 0.02),
        'ln2_g': 1.0 + nrm((DEPTH, d), 0.02),
        'ln2_b': nrm((DEPTH, d), 0.02),
    }


def reference(x, rwkv_mix, rwkv_w_rkv, rwkv_w0, rwkv_w1, rwkv_w2, rwkv_a0, rwkv_a1, rwkv_a2,
              rwkv_g1, rwkv_g2, rwkv_k_k, rwkv_k_a, rwkv_r_k, rwkv_gn_g, rwkv_gn_b, rwkv_w_o,
              ret_w_in, ret_gn_g, ret_gn_b, ret_w_o, moba_w_qkv, moba_w_o,
              ffn_w_in, ffn_conv_w, ffn_conv_b, ffn_w_out, ln1_g, ln1_b, ln2_g, ln2_b):
    for i in range(DEPTH):
        kind, j = i % N_MIXERS, i // N_MIXERS
        if kind == 0:
            mixed = _rwkv7_time_mix(x, rwkv_mix[j], rwkv_w_rkv[j], rwkv_w0[j], rwkv_w1[j], rwkv_w2[j],
                                    rwkv_a0[j], rwkv_a1[j], rwkv_a2[j], rwkv_g1[j], rwkv_g2[j],
                                    rwkv_k_k[j], rwkv_k_a[j], rwkv_r_k[j], rwkv_gn_g[j], rwkv_gn_b[j],
                                    rwkv_w_o[j])
        elif kind == 1:
            mixed = _retention(x, ret_w_in[j], ret_gn_g[j], ret_gn_b[j], ret_w_o[j])
        else:
            mixed = _moba_attention(x, moba_w_qkv[j], moba_w_o[j])
        x = _layer_norm(DEEPNORM_ALPHA * x + mixed, ln1_g[i], ln1_b[i])
        x = _layer_norm(DEEPNORM_ALPHA * x + _conv_ffn(x, ffn_w_in[i], ffn_conv_w[i], ffn_conv_b[i], ffn_w_out[i]),
                        ln2_g[i], ln2_b[i])
    return x
```

```python
import functools

import jax
import jax.numpy as jnp
from jax import lax
from jax.experimental import pallas as pl
from jax.experimental.pallas import tpu as pltpu

F32 = jnp.float32
BF16 = jnp.bfloat16
HIGHEST = lax.Precision.HIGHEST

DEPTH = 4
N_MIXERS = 3
RWKV_HEAD = 64
RWKV_GN_EPS = 64e-5
RET_HEADS = 4
RET_ROPE_BASE = 10000.0
RET_GN_EPS = 1e-5
MOBA_HEADS = 16
MOBA_HEAD_DIM = 64
MOBA_BLOCK = 256
MOBA_TOPK = 3
LN_EPS = 1e-5
DEEPNORM_ALPHA = (2 * DEPTH) ** 0.25

LANES = 128
SUBLANES = 8
BF16_ROWS = 16
VMEM_LIMIT_BYTES = 52 * 1024 * 1024

NEG_BIG = -1e30

_NT = (((1,), (1,)), ((), ()))
_TN = (((0,), (0,)), ((), ()))


def _cparams(*sem):
    return pltpu.CompilerParams(dimension_semantics=sem, vmem_limit_bytes=VMEM_LIMIT_BYTES)


def _sigmoid(x):
    return 1.0 / (1.0 + jnp.exp(-x))


def _layer_norm_rows(y, g, b):
    mu = jnp.mean(y, axis=-1, keepdims=True)
    d = y - mu
    var = jnp.mean(d * d, axis=-1, keepdims=True)
    return d * lax.rsqrt(var + LN_EPS) * g + b


def _mm_body(x_ref, w_ref, o_ref, xb_ref):
    @pl.when(pl.program_id(1) == 0)
    def _():
        xb_ref[...] = x_ref[...].astype(BF16)

    o_ref[...] = jnp.dot(xb_ref[...], w_ref[...], preferred_element_type=F32).astype(o_ref.dtype)


def _matmul(x, w, *, tm, tn, out_dtype=F32):
    m, k = x.shape
    n = w.shape[1]
    return pl.pallas_call(
        _mm_body,
        grid=(m // tm, n // tn),
        in_specs=[pl.BlockSpec((tm, k), lambda i, j: (i, 0)),
                  pl.BlockSpec((k, tn), lambda i, j: (0, j))],
        out_specs=pl.BlockSpec((tm, tn), lambda i, j: (i, j)),
        out_shape=jax.ShapeDtypeStruct((m, n), out_dtype),
        scratch_shapes=[pltpu.VMEM((tm, k), BF16)],
        compiler_params=_cparams("parallel", "arbitrary"),
        name="matmul",
    )(x, w)


def _mm_nt_body(x_ref, wt_ref, o_ref, xb_ref):
    @pl.when(pl.program_id(1) == 0)
    def _():
        xb_ref[...] = x_ref[...].astype(BF16)

    o_ref[...] = lax.dot_general(wt_ref[...], xb_ref[...], _NT, preferred_element_type=F32)


def _matmul_nt(x, wt, *, tm, tn):
    m, k = x.shape
    n = wt.shape[0]
    return pl.pallas_call(
        _mm_nt_body,
        grid=(m // tm, n // tn),
        in_specs=[pl.BlockSpec((tm, k), lambda i, j: (i, 0)),
                  pl.BlockSpec((tn, k), lambda i, j: (j, 0))],
        out_specs=pl.BlockSpec((tn, tm), lambda i, j: (j, i)),
        out_shape=jax.ShapeDtypeStruct((n, m), F32),
        scratch_shapes=[pltpu.VMEM((tm, k), BF16)],
        compiler_params=_cparams("parallel", "arbitrary"),
        name="matmul_nt",
    )(x, wt)


def _mm_res_ln_body(o_ref, w_ref, x_ref, g_ref, b_ref, out_ref, *, lhs_transposed):
    lhs = o_ref[...].astype(BF16)
    if lhs_transposed:
        acc = lax.dot_general(lhs, w_ref[...], _TN, preferred_element_type=F32)
    else:
        acc = jnp.dot(lhs, w_ref[...], preferred_element_type=F32)
    y = DEEPNORM_ALPHA * x_ref[...] + acc
    out_ref[...] = _layer_norm_rows(y, g_ref[...], b_ref[...])


def _matmul_residual_ln(o, w, x, g, b, *, tm, lhs_transposed=False):
    m, d = x.shape
    k = w.shape[0]
    if lhs_transposed:
        o_spec = pl.BlockSpec((k, tm), lambda i: (0, i))
    else:
        o_spec = pl.BlockSpec((tm, k), lambda i: (i, 0))
    return pl.pallas_call(
        functools.partial(_mm_res_ln_body, lhs_transposed=lhs_transposed),
        grid=(m // tm,),
        in_specs=[o_spec,
                  pl.BlockSpec((k, d), lambda i: (0, 0)),
                  pl.BlockSpec((tm, d), lambda i: (i, 0)),
                  pl.BlockSpec((1, d), lambda i: (0, 0)),
                  pl.BlockSpec((1, d), lambda i: (0, 0))],
        out_specs=pl.BlockSpec((tm, d), lambda i: (i, 0)),
        out_shape=jax.ShapeDtypeStruct((m, d), F32),
        compiler_params=_cparams("parallel"),
        name="out_proj_residual_ln",
    )(o, w, x, g.reshape(1, d), b.reshape(1, d))


def _ffn_body(x_ref, xh_ref, wu_ref, wg_ref, cwu_ref, cwg_ref, cbu_ref, cbg_ref, wo_ref, g_ref, b_ref,
              out_ref, xb_ref, hu_ref, hg_ref, acc_ref, *, tm, tiles_per_seq, n_f):
    i = pl.program_id(0)
    f = pl.program_id(1)
    halo = BF16_ROWS

    @pl.when(f == 0)
    def _():
        first = (i % tiles_per_seq) == 0
        xb_ref[0:halo, :] = jnp.where(first, 0.0, xh_ref[...]).astype(BF16)
        xb_ref[halo:, :] = x_ref[...].astype(BF16)
        acc_ref[...] = jnp.zeros_like(acc_ref)

    xb = xb_ref[...]
    hu_ref[...] = jnp.dot(xb, wu_ref[...], preferred_element_type=F32)
    hg_ref[...] = jnp.dot(xb, wg_ref[...], preferred_element_type=F32)

    def conv(h_ref, cw_ref, cb_ref):
        return (cw_ref[0:1, :] * h_ref[pl.ds(halo - 2, tm), :]
                + cw_ref[1:2, :] * h_ref[pl.ds(halo - 1, tm), :]
                + cw_ref[2:3, :] * h_ref[pl.ds(halo, tm), :]
                + cb_ref[...])

    u = conv(hu_ref, cwu_ref, cbu_ref)
    gt = conv(hg_ref, cwg_ref, cbg_ref)
    act = (gt * _sigmoid(gt)) * u
    acc_ref[...] += jnp.dot(act.astype(BF16), wo_ref[...], preferred_element_type=F32)

    @pl.when(f == n_f - 1)
    def _():
        y = DEEPNORM_ALPHA * x_ref[...] + acc_ref[...]
        out_ref[...] = _layer_norm_rows(y, g_ref[...], b_ref[...])


def _conv_ffn_ln(x, seq_len, w_in, conv_w, conv_b, w_out, g, b, *, tm, tf):
    m, d = x.shape
    ff = w_out.shape[0]
    n_f = ff // tf
    halo = BF16_ROWS
    body = functools.partial(_ffn_body, tm=tm, tiles_per_seq=seq_len // tm, n_f=n_f)
    conv_b2 = conv_b.reshape(1, 2 * ff)
    return pl.pallas_call(
        body,
        grid=(m // tm, n_f),
        in_specs=[pl.BlockSpec((tm, d), lambda i, f: (i, 0)),
                  pl.BlockSpec((halo, d), lambda i, f: (jnp.maximum(i * (tm // halo) - 1, 0), 0)),
                  pl.BlockSpec((d, tf), lambda i, f: (0, f)),
                  pl.BlockSpec((d, tf), lambda i, f: (0, n_f + f)),
                  pl.BlockSpec((3, tf), lambda i, f: (0, f)),
                  pl.BlockSpec((3, tf), lambda i, f: (0, n_f + f)),
                  pl.BlockSpec((1, tf), lambda i, f: (0, f)),
                  pl.BlockSpec((1, tf), lambda i, f: (0, n_f + f)),
                  pl.BlockSpec((tf, d), lambda i, f: (f, 0)),
                  pl.BlockSpec((1, d), lambda i, f: (0, 0)),
                  pl.BlockSpec((1, d), lambda i, f: (0, 0))],
        out_specs=pl.BlockSpec((tm, d), lambda i, f: (i, 0)),
        out_shape=jax.ShapeDtypeStruct((m, d), F32),
        scratch_shapes=[pltpu.VMEM((tm + halo, d), BF16),
                        pltpu.VMEM((tm + halo, tf), F32),
                        pltpu.VMEM((tm + halo, tf), F32),
                        pltpu.VMEM((tm, d), F32)],
        compiler_params=_cparams("parallel", "arbitrary"),
        name="conv_ffn_ln",
    )(x, x, w_in, w_in, conv_w, conv_w, conv_b2, conv_b2, w_out, g.reshape(1, d), b.reshape(1, d))


def _head_pair_sum(v, ones_bd):
    hi = v.astype(BF16)
    lo = (v - hi.astype(F32)).astype(BF16)
    return (jnp.dot(hi, ones_bd, preferred_element_type=F32)
            + jnp.dot(lo, ones_bd, preferred_element_type=F32))


def _rwkv_proj_body(x_ref, xp_ref, mix_ref, wrkv_ref, w1_ref, w2_ref, a1_ref, a2_ref, g1_ref, g2_ref,
                    w0_ref, a0_ref, kk_ref, ka_ref,
                    r_out, ld_out, k_out, v_out, kk_out, b_out, gate_out, *, tm, tiles_per_seq):
    i = pl.program_id(0)
    d = x_ref.shape[1]
    x = x_ref[...]
    first = (i % tiles_per_seq) == 0
    prev = jnp.where(first, 0.0, xp_ref[SUBLANES - 1:SUBLANES, :])
    row = lax.broadcasted_iota(jnp.int32, (tm, d), 0)
    xs = jnp.where(row == 0, prev, pltpu.roll(x, 1, 0))
    xx = xs - x

    def mixed(j):
        return (x + xx * mix_ref[j:j + 1, :]).astype(BF16)

    def mm(a, w):
        return jnp.dot(a, w, preferred_element_type=F32)

    r = mm(mixed(0), wrkv_ref[0])
    k = mm(mixed(2), wrkv_ref[1])
    v = mm(mixed(3), wrkv_ref[2])
    lw = w0_ref[...] + mm(jnp.tanh(mm(mixed(1), w1_ref[...])).astype(BF16), w2_ref[...])
    softplus_neg = jnp.maximum(-lw, 0.0) + jnp.log(1.0 + jnp.exp(-jnp.abs(lw)))
    log_decay = -jnp.exp(-softplus_neg - 0.5)
    a = _sigmoid(a0_ref[...] + mm(mm(mixed(4), a1_ref[...]).astype(BF16), a2_ref[...]))
    gate = mm(_sigmoid(mm(mixed(5), g1_ref[...])).astype(BF16), g2_ref[...])

    kk = k * kk_ref[...]
    rr = lax.broadcasted_iota(jnp.int32, (LANES, LANES), 0) // RWKV_HEAD
    cc = lax.broadcasted_iota(jnp.int32, (LANES, LANES), 1) // RWKV_HEAD
    ones_bd = jnp.where(rr == cc, 1.0, 0.0).astype(BF16)
    sq = kk * kk
    ss = jnp.concatenate([_head_pair_sum(sq[:, j * LANES:(j + 1) * LANES], ones_bd)
                          for j in range(d // LANES)], axis=1)
    kk = kk / jnp.maximum(jnp.sqrt(ss), 1e-12)

    r_out[...] = r
    ld_out[...] = log_decay
    k_out[...] = k * (1.0 + (a - 1.0) * ka_ref[...])
    v_out[...] = v
    kk_out[...] = kk
    b_out[...] = kk * a
    gate_out[...] = gate


def _rwkv_proj(x, seq_len, mix, w_rkv, w0, w1, w2, a0, a1, a2, g1, g2, k_k, k_a, *, tm):
    m, d = x.shape
    full = lambda arr: pl.BlockSpec(arr.shape, lambda i: (0,) * arr.ndim)
    vec = lambda a: a.reshape(1, d)
    args = (mix, w_rkv.astype(BF16), w1.astype(BF16), w2.astype(BF16), a1.astype(BF16), a2.astype(BF16),
            g1.astype(BF16), g2.astype(BF16), vec(w0), vec(a0), vec(k_k), vec(k_a))
    row_spec = pl.BlockSpec((tm, d), lambda i: (i, 0))
    return pl.pallas_call(
        functools.partial(_rwkv_proj_body, tm=tm, tiles_per_seq=seq_len // tm),
        grid=(m // tm,),
        in_specs=[row_spec,
                  pl.BlockSpec((SUBLANES, d), lambda i: (jnp.maximum(i * (tm // SUBLANES) - 1, 0), 0))]
                 + [full(a) for a in args],
        out_specs=[row_spec] * 7,
        out_shape=[jax.ShapeDtypeStruct((m, d), F32)] * 7,
        compiler_params=_cparams("parallel"),
        name="rwkv_proj",
    )(x, x, *args)


def _rwkv_chunk_body(r_ref, ld_ref, k_ref, v_ref, kk_ref, b_ref, gate_ref, rk_ref, gg_ref, gb_ref,
                     o_ref, st_ref, *, chunk, pairs):
    c = pl.program_id(2)

    @pl.when(c == 0)
    def _():
        st_ref[...] = jnp.zeros_like(st_ref)

    n = RWKV_HEAD
    row = lax.broadcasted_iota(jnp.int32, (chunk, LANES), 0)
    lane = lax.broadcasted_iota(jnp.int32, (chunk, LANES), 1)
    col = lane % n
    strict = row > col
    incl = row >= col
    eye2 = jnp.where(row == col, 1.0, 0.0)
    head1 = lane >= n
    rr = lax.broadcasted_iota(jnp.int32, (LANES, LANES), 0) // n
    cc = lax.broadcasted_iota(jnp.int32, (LANES, LANES), 1) // n
    bd = rr == cc
    ones_bd = jnp.where(bd, 1.0, 0.0)
    mean_bd = ones_bd * (1.0 / n)
    ltri = jnp.where(lax.broadcasted_iota(jnp.int32, (chunk, chunk), 0)
                     >= lax.broadcasted_iota(jnp.int32, (chunk, chunk), 1), 1.0, 0.0)
    ones_cl = jnp.ones((chunk, LANES), F32)

    def dot(a, b_):
        return jnp.dot(a, b_, precision=HIGHEST, preferred_element_type=F32)

    def dot_nt(a, b_):
        return lax.dot_general(a, b_, _NT, precision=HIGHEST, preferred_element_type=F32)

    def dot_tn(a, b_):
        return lax.dot_general(a, b_, _TN, precision=HIGHEST, preferred_element_type=F32)

    def head_stack(t):
        return jnp.concatenate([jnp.where(head1, 0.0, t), jnp.where(head1, t, 0.0)], axis=0)

    def block_diag(t):
        return jnp.where(bd, jnp.concatenate([t, t], axis=0), 0.0)

    for p in range(pairs):
        sl = slice(p * LANES, (p + 1) * LANES)
        r = r_ref[0, :, sl]
        ld = ld_ref[0, :, sl]
        k = k_ref[0, :, sl]
        v = v_ref[0, :, sl]
        kk = kk_ref[0, :, sl]
        bv = b_ref[0, :, sl]

        cum = dot(ltri, ld)
        cum_last = cum[chunk - 1:chunk, :]
        e_inv = jnp.exp(-cum)
        e_last = jnp.exp(cum_last - cum)
        r_hat = r * jnp.exp(cum)
        a_hat = -(kk * jnp.exp(cum - ld))
        lhs = jnp.concatenate([a_hat, r_hat], axis=0)
        sb = dot_nt(lhs, head_stack(bv * e_inv))
        sk = dot_nt(lhs, head_stack(k * e_inv))
        a_ab = jnp.where(strict, sb[:chunk], 0.0)
        a_rb = jnp.where(incl, sb[chunk:], 0.0)
        a_ak = jnp.where(strict, sk[:chunk], 0.0)
        a_rk = jnp.where(incl, sk[chunk:], 0.0)

        xp = a_ab
        tinv = eye2 + xp
        power = 2
        while power < chunk:
            xp = dot(xp, block_diag(xp))
            tinv = tinv + dot(tinv, block_diag(xp))
            power *= 2

        h0 = st_ref[p]
        hv = head_stack(v)
        z = dot(a_hat, h0) + dot(a_ak, hv)
        u = dot(tinv, head_stack(z))
        y = dot(r_hat, h0) + dot(a_rb, head_stack(u)) + dot(a_rk, hv)
        total_decay = jnp.exp(dot_tn(ld, ones_cl))
        h_new = total_decay * h0 + dot_tn(jnp.concatenate([bv * e_last, k * e_last], axis=0),
                                          jnp.concatenate([u, v], axis=0))
        st_ref[p] = jnp.where(bd, h_new, 0.0)

        mu = dot(y, mean_bd)
        dlt = y - mu
        var = dot(dlt * dlt, mean_bd)
        yn = dlt * lax.rsqrt(var + RWKV_GN_EPS) * gg_ref[:, sl] + gb_ref[:, sl]
        bonus = dot(r * k * rk_ref[:, sl], ones_bd) * v
        o_ref[0, :, sl] = ((yn + bonus) * gate_ref[0, :, sl]).astype(o_ref.dtype)


def _rwkv_chunk(r, ld, k, v, kk, bvec, gate, r_k, gn_g, gn_b, *, pairs):
    bsz, t_len, d = r.shape
    chunk = RWKV_HEAD
    width = pairs * LANES
    seq_spec = pl.BlockSpec((1, chunk, width), lambda b, g, c: (b, c, g))
    vec_spec = pl.BlockSpec((1, width), lambda b, g, c: (0, g))
    return pl.pallas_call(
        functools.partial(_rwkv_chunk_body, chunk=chunk, pairs=pairs),
        grid=(bsz, d // width, t_len // chunk),
        in_specs=[seq_spec] * 7 + [vec_spec] * 3,
        out_specs=seq_spec,
        out_shape=jax.ShapeDtypeStruct((bsz, t_len, d), BF16),
        scratch_shapes=[pltpu.VMEM((pairs, LANES, LANES), F32)],
        compiler_params=_cparams("parallel", "parallel", "arbitrary"),
        name="rwkv_chunk",
    )(r, ld, k, v, kk, bvec, gate, r_k.reshape(1, d), gn_g.reshape(1, d), gn_b.reshape(1, d))


def _rwkv_mixer(x, bsz, t_len, mix, w_rkv, w0, w1, w2, a0, a1, a2, g1, g2, k_k, k_a, r_k, gn_g, gn_b, w_o,
                ln_g, ln_b):
    d = x.shape[1]
    outs = _rwkv_proj(x, t_len, mix, w_rkv, w0, w1, w2, a0, a1, a2, g1, g2, k_k, k_a, tm=256)
    seq = [o.reshape(bsz, t_len, d) for o in outs]
    o = _rwkv_chunk(*seq, r_k, gn_g, gn_b, pairs=2)
    return _matmul_residual_ln(o.reshape(bsz * t_len, d), w_o.astype(BF16), x, ln_g, ln_b, tm=512)


def _ret_chunk_body(q_ref, k_ref, v_ref, gate_ref, cos_ref, sin_ref, g_ref, b_ref, o_ref, st_ref, *, chunk):
    h = pl.program_id(1)
    c = pl.program_id(2)

    @pl.when(c == 0)
    def _():
        st_ref[...] = jnp.zeros_like(st_ref)

    dk = q_ref.shape[2]
    half = dk // 2
    hv = jnp.full((1, 1), h, jnp.int32).astype(F32)
    log_gamma = jnp.log(1.0 - jnp.exp2(-5.0 - hv))
    cos = cos_ref[...]
    sin = sin_ref[...]

    def rotate(t):
        te, to = t[:, :half], t[:, half:]
        return jnp.concatenate([te * cos - to * sin, to * cos + te * sin], axis=1)

    q = rotate(q_ref[0])
    k = rotate(k_ref[0]) * (dk ** -0.5)
    v = v_ref[0].astype(BF16)

    ri = lax.broadcasted_iota(jnp.int32, (chunk, chunk), 0)
    ci = lax.broadcasted_iota(jnp.int32, (chunk, chunk), 1)
    rel = (ri - ci).astype(F32)
    inner = jnp.where(rel >= 0, jnp.exp(log_gamma * jnp.maximum(rel, 0.0)), 0.0)
    pos = lax.broadcasted_iota(jnp.int32, (chunk, dk), 0).astype(F32)
    cross = jnp.exp(log_gamma * (pos + 1.0))
    sdecay = jnp.exp(log_gamma * (chunk - 1.0 - pos))
    chunk_decay = jnp.exp(log_gamma * chunk)

    s = lax.dot_general(q.astype(BF16), k.astype(BF16), _NT, preferred_element_type=F32) * inner
    st = st_ref[...]
    o = (jnp.dot(s.astype(BF16), v, preferred_element_type=F32)
         + jnp.dot((q * cross).astype(BF16), st.astype(BF16), preferred_element_type=F32))
    st_ref[...] = st * chunk_decay + lax.dot_general((k * sdecay).astype(BF16), v, _TN,
                                                     preferred_element_type=F32)

    mu = jnp.mean(o, axis=-1, keepdims=True)
    dlt = o - mu
    var = jnp.mean(dlt * dlt, axis=-1, keepdims=True)
    on = dlt * lax.rsqrt(var + RET_GN_EPS) * g_ref[...] + b_ref[...]
    gt = gate_ref[0]
    o_ref[0] = (gt * _sigmoid(gt) * on).astype(o_ref.dtype)


def _ret_chunk(proj, cos, sin, gn_g, gn_b, *, chunk):
    bsz, t_len, six_d = proj.shape
    d = six_d // 6
    h = RET_HEADS
    dk, dv = d // h, 2 * d // h
    k_off, v_off, g_off = d // dk, 2 * d // dv, 4 * d // dv
    return pl.pallas_call(
        functools.partial(_ret_chunk_body, chunk=chunk),
        grid=(bsz, h, t_len // chunk),
        in_specs=[pl.BlockSpec((1, chunk, dk), lambda b, hh, c: (b, c, hh)),
                  pl.BlockSpec((1, chunk, dk), lambda b, hh, c: (b, c, k_off + hh)),
                  pl.BlockSpec((1, chunk, dv), lambda b, hh, c: (b, c, v_off + hh)),
                  pl.BlockSpec((1, chunk, dv), lambda b, hh, c: (b, c, g_off + hh)),
                  pl.BlockSpec((chunk, dk // 2), lambda b, hh, c: (c, 0)),
                  pl.BlockSpec((chunk, dk // 2), lambda b, hh, c: (c, 0)),
                  pl.BlockSpec((1, dv), lambda b, hh, c: (0, hh)),
                  pl.BlockSpec((1, dv), lambda b, hh, c: (0, hh))],
        out_specs=pl.BlockSpec((1, chunk, dv), lambda b, hh, c: (b, c, hh)),
        out_shape=jax.ShapeDtypeStruct((bsz, t_len, 2 * d), BF16),
        scratch_shapes=[pltpu.VMEM((dk, dv), F32)],
        compiler_params=_cparams("parallel", "parallel", "arbitrary"),
        name="retention_chunk",
    )(proj, proj, proj, proj, cos, sin, gn_g.reshape(1, 2 * d), gn_b.reshape(1, 2 * d))


def _retention_mixer(x, bsz, t_len, w_in, gn_g, gn_b, w_o, ln_g, ln_b):
    d = x.shape[1]
    h = RET_HEADS
    dk = d // h
    w_qk = w_in[:, :2 * d].reshape(d, 2 * h, dk // 2, 2).transpose(0, 1, 3, 2).reshape(d, 2 * d)
    w_perm = jnp.concatenate([w_qk, w_in[:, 2 * d:]], axis=1).astype(BF16)
    proj = _matmul(x, w_perm, tm=1024, tn=512)
    inv = 1.0 / (RET_ROPE_BASE ** jnp.linspace(0.0, 1.0, dk // 2, dtype=F32))
    ang = jnp.arange(t_len, dtype=F32)[:, None] * inv[None, :]
    o = _ret_chunk(proj.reshape(bsz, t_len, 6 * d), jnp.cos(ang), jnp.sin(ang), gn_g, gn_b, chunk=128)
    return _matmul_residual_ln(o.reshape(bsz * t_len, 2 * d), w_o.astype(BF16), x, ln_g, ln_b, tm=512)


def _moba_body(qt_ref, vt_ref, k_ref, o_ref, kmean_ref, *, n_blk):
    qb = pl.program_id(2)
    blk = MOBA_BLOCK
    hd = MOBA_HEAD_DIM

    @pl.when(qb == 0)
    def _():
        kmean_ref[...] = jnp.mean(k_ref[...].reshape(n_blk, blk, LANES), axis=1)

    kmean = kmean_ref[...]
    qt = qt_ref[...]
    zeros = jnp.zeros((hd, blk), F32)
    qz = (jnp.concatenate([qt[:hd], zeros], axis=0), jnp.concatenate([zeros, qt[hd:]], axis=0))
    qz_b = tuple(t.astype(BF16) for t in qz)
    blk_id = lax.broadcasted_iota(jnp.int32, (n_blk, blk), 0)

    sels = []
    for hh in range(2):
        gate = jnp.dot(kmean, qz[hh], precision=HIGHEST, preferred_element_type=F32)
        beaten = jnp.zeros((n_blk, blk), F32)
        for m in range(n_blk):
            gm = gate[m:m + 1, :]
            wins = jnp.where(gm > gate, 1.0, jnp.where(gm == gate, jnp.where(blk_id > m, 1.0, 0.0), 0.0))
            beaten = beaten + jnp.where(m < qb, wins, 0.0)
        sels.append(jnp.where(blk_id < qb, jnp.where(beaten < MOBA_TOPK, 1.0, 0.0), 0.0))

    key_i = lax.broadcasted_iota(jnp.int32, (blk, blk), 0)
    qry_i = lax.broadcasted_iota(jnp.int32, (blk, blk), 1)
    causal = jnp.where(key_i <= qry_i, 1.0, 0.0)
    scale = hd ** -0.5

    def step(nb, carry):
        off = pl.multiple_of(nb * blk, blk)
        kb = k_ref[pl.ds(off, blk), :].astype(BF16)
        vb = vt_ref[:, pl.ds(off, blk)].astype(BF16)
        own = nb == qb
        out = []
        for hh in range(2):
            m_i, l_i, acc = carry[hh]
            s = jnp.dot(kb, qz_b[hh], preferred_element_type=F32) * scale
            sel_row = jnp.sum(jnp.where(blk_id == nb, sels[hh], 0.0), axis=0, keepdims=True)
            allowed = jnp.where(own, causal, jnp.broadcast_to(sel_row, (blk, blk))) > 0.5
            s = jnp.where(allowed, s, NEG_BIG)
            m_new = jnp.maximum(m_i, jnp.max(s, axis=0, keepdims=True))
            p = jnp.where(allowed, jnp.exp(s - m_new), 0.0)
            alpha = jnp.exp(m_i - m_new)
            l_new = alpha * l_i + jnp.sum(p, axis=0, keepdims=True)
            pv = jnp.dot(vb[hh * hd:(hh + 1) * hd], p.astype(BF16), preferred_element_type=F32)
            out.append((m_new, l_new, alpha * acc + pv))
        return tuple(out)

    init = tuple((jnp.full((1, blk), NEG_BIG, F32), jnp.zeros((1, blk), F32), jnp.zeros((hd, blk), F32))
                 for _ in range(2))
    res = lax.fori_loop(0, qb + 1, step, init)
    for hh in range(2):
        _, l_i, acc = res[hh]
        o_ref[hh * hd:(hh + 1) * hd, :] = acc / l_i


def _moba_attention(qvt, k, bsz, t_len):
    d = k.shape[1]
    blk = MOBA_BLOCK
    n_blk = t_len // blk
    n_pairs = d // LANES
    return pl.pallas_call(
        functools.partial(_moba_body, n_blk=n_blk),
        grid=(bsz, n_pairs, n_blk),
        in_specs=[pl.BlockSpec((LANES, blk), lambda b, p, q: (p, b * n_blk + q)),
                  pl.BlockSpec((LANES, t_len), lambda b, p, q: (n_pairs + p, b)),
                  pl.BlockSpec((t_len, LANES), lambda b, p, q: (b, p))],
        out_specs=pl.BlockSpec((LANES, blk), lambda b, p, q: (p, b * n_blk + q)),
        out_shape=jax.ShapeDtypeStruct((d, bsz * t_len), F32),
        scratch_shapes=[pltpu.VMEM((n_blk, LANES), F32)],
        compiler_params=_cparams("parallel", "parallel", "arbitrary"),
        name="moba_attention",
    )(qvt, qvt, k)


def _moba_mixer(x, bsz, t_len, w_qkv, w_o, ln_g, ln_b):
    d = x.shape[1]
    assert t_len % MOBA_BLOCK == 0
    w_k = w_qkv[:, d:2 * d].astype(BF16)
    w_qv_t = jnp.concatenate([w_qkv[:, :d], w_qkv[:, 2 * d:]], axis=1).T.astype(BF16)
    k = _matmul(x, w_k, tm=1024, tn=512)
    qvt = _matmul_nt(x, w_qv_t, tm=1024, tn=512)
    ot = _moba_attention(qvt, k, bsz, t_len)
    return _matmul_residual_ln(ot, w_o.astype(BF16), x, ln_g, ln_b, tm=512, lhs_transposed=True)


def kernel(x, rwkv_mix, rwkv_w_rkv, rwkv_w0, rwkv_w1, rwkv_w2, rwkv_a0, rwkv_a1, rwkv_a2, rwkv_g1, rwkv_g2,
           rwkv_k_k, rwkv_k_a, rwkv_r_k, rwkv_gn_g, rwkv_gn_b, rwkv_w_o, ret_w_in, ret_gn_g, ret_gn_b, ret_w_o,
           moba_w_qkv, moba_w_o, ffn_w_in, ffn_conv_w, ffn_conv_b, ffn_w_out, ln1_g, ln1_b, ln2_g, ln2_b):
    bsz, t_len, d = x.shape
    h = x.reshape(bsz * t_len, d)
    for i in range(DEPTH):
        kind, j = i % N_MIXERS, i // N_MIXERS
        if kind == 0:
            h = _rwkv_mixer(h, bsz, t_len, rwkv_mix[j], rwkv_w_rkv[j], rwkv_w0[j], rwkv_w1[j], rwkv_w2[j],
                            rwkv_a0[j], rwkv_a1[j], rwkv_a2[j], rwkv_g1[j], rwkv_g2[j], rwkv_k_k[j],
                            rwkv_k_a[j], rwkv_r_k[j], rwkv_gn_g[j], rwkv_gn_b[j], rwkv_w_o[j],
                            ln1_g[i], ln1_b[i])
        elif kind == 1:
            h = _retention_mixer(h, bsz, t_len, ret_w_in[j], ret_gn_g[j], ret_gn_b[j], ret_w_o[j],
                                 ln1_g[i], ln1_b[i])
        else:
            h = _moba_mixer(h, bsz, t_len, moba_w_qkv[j], moba_w_o[j], ln1_g[i], ln1_b[i])
        h = _conv_ffn_ln(h, t_len, ffn_w_in[i].astype(BF16), ffn_conv_w[i], ffn_conv_b[i],
                         ffn_w_out[i].astype(BF16), ln2_g[i], ln2_b[i], tm=512, tf=1408)
    return h.reshape(bsz, t_len, d)
```

```python
import functools

import jax
import jax.numpy as jnp
from jax import lax
from jax.experimental import pallas as pl
from jax.experimental.pallas import tpu as pltpu

F32 = jnp.float32
BF16 = jnp.bfloat16
HIGHEST = lax.Precision.HIGHEST

DEPTH = 4
N_MIXERS = 3
RWKV_HEAD = 64
RWKV_GN_EPS = 64e-5
RET_HEADS = 4
RET_ROPE_BASE = 10000.0
RET_GN_EPS = 1e-5
MOBA_HEADS = 16
MOBA_HEAD_DIM = 64
MOBA_BLOCK = 256
MOBA_TOPK = 3
LN_EPS = 1e-5
DEEPNORM_ALPHA = (2 * DEPTH) ** 0.25

LANES = 128
SUBLANES = 8
BF16_ROWS = 16
VMEM_LIMIT_BYTES = 52 * 1024 * 1024

NEG_BIG = -1e30

_NT = (((1,), (1,)), ((), ()))
_TN = (((0,), (0,)), ((), ()))


def _cparams(*sem):
    return pltpu.CompilerParams(dimension_semantics=sem, vmem_limit_bytes=VMEM_LIMIT_BYTES)


def _sigmoid(x):
    return 1.0 / (1.0 + jnp.exp(-x))


def _layer_norm_rows(y, g, b):
    mu = jnp.mean(y, axis=-1, keepdims=True)
    d = y - mu
    var = jnp.mean(d * d, axis=-1, keepdims=True)
    return d * lax.rsqrt(var + LN_EPS) * g + b


def _mm_body(x_ref, w_ref, o_ref, xb_ref):
    @pl.when(pl.program_id(1) == 0)
    def _():
        xb_ref[...] = x_ref[...].astype(BF16)

    o_ref[...] = jnp.dot(xb_ref[...], w_ref[...], preferred_element_type=F32).astype(o_ref.dtype)


def _matmul(x, w, *, tm, tn, out_dtype=F32):
    m, k = x.shape
    n = w.shape[1]
    return pl.pallas_call(
        _mm_body,
        grid=(m // tm, n // tn),
        in_specs=[pl.BlockSpec((tm, k), lambda i, j: (i, 0)),
                  pl.BlockSpec((k, tn), lambda i, j: (0, j))],
        out_specs=pl.BlockSpec((tm, tn), lambda i, j: (i, j)),
        out_shape=jax.ShapeDtypeStruct((m, n), out_dtype),
        scratch_shapes=[pltpu.VMEM((tm, k), BF16)],
        compiler_params=_cparams("parallel", "arbitrary"),
        name="matmul",
    )(x, w)


def _mm_nt_body(x_ref, wt_ref, o_ref, xb_ref):
    @pl.when(pl.program_id(1) == 0)
    def _():
        xb_ref[...] = x_ref[...].astype(BF16)

    o_ref[...] = lax.dot_general(wt_ref[...], xb_ref[...], _NT, preferred_element_type=F32)


def _matmul_nt(x, wt, *, tm, tn):
    m, k = x.shape
    n = wt.shape[0]
    return pl.pallas_call(
        _mm_nt_body,
        grid=(m // tm, n // tn),
        in_specs=[pl.BlockSpec((tm, k), lambda i, j: (i, 0)),
                  pl.BlockSpec((tn, k), lambda i, j: (j, 0))],
        out_specs=pl.BlockSpec((tn, tm), lambda i, j: (j, i)),
        out_shape=jax.ShapeDtypeStruct((n, m), F32),
        scratch_shapes=[pltpu.VMEM((tm, k), BF16)],
        compiler_params=_cparams("parallel", "arbitrary"),
        name="matmul_nt",
    )(x, wt)


def _mm_res_ln_body(o_ref, w_ref, x_ref, g_ref, b_ref, out_ref, *, lhs_transposed):
    lhs = o_ref[...].astype(BF16)
    if lhs_transposed:
        acc = lax.dot_general(lhs, w_ref[...], _TN, preferred_element_type=F32)
    else:
        acc = jnp.dot(lhs, w_ref[...], preferred_element_type=F32)
    y = DEEPNORM_ALPHA * x_ref[...] + acc
    out_ref[...] = _layer_norm_rows(y, g_ref[...], b_ref[...])


def _matmul_residual_ln(o, w, x, g, b, *, tm, lhs_transposed=False):
    m, d = x.shape
    k = w.shape[0]
    if lhs_transposed:
        o_spec = pl.BlockSpec((k, tm), lambda i: (0, i))
    else:
        o_spec = pl.BlockSpec((tm, k), lambda i: (i, 0))
    return pl.pallas_call(
        functools.partial(_mm_res_ln_body, lhs_transposed=lhs_transposed),
        grid=(m // tm,),
        in_specs=[o_spec,
                  pl.BlockSpec((k, d), lambda i: (0, 0)),
                  pl.BlockSpec((tm, d), lambda i: (i, 0)),
                  pl.BlockSpec((1, d), lambda i: (0, 0)),
                  pl.BlockSpec((1, d), lambda i: (0, 0))],
        out_specs=pl.BlockSpec((tm, d), lambda i: (i, 0)),
        out_shape=jax.ShapeDtypeStruct((m, d), F32),
        compiler_params=_cparams("parallel"),
        name="out_proj_residual_ln",
    )(o, w, x, g.reshape(1, d), b.reshape(1, d))


def _ffn_body(x_ref, xh_ref, wu_ref, wg_ref, cwu_ref, cwg_ref, cbu_ref, cbg_ref, wo_ref, g_ref, b_ref,
              out_ref, xb_ref, hu_ref, hg_ref, acc_ref, *, tm, tiles_per_seq, n_f):
    i = pl.program_id(0)
    f = pl.program_id(1)
    halo = BF16_ROWS

    @pl.when(f == 0)
    def _():
        first = (i % tiles_per_seq) == 0
        xb_ref[0:halo, :] = jnp.where(first, 0.0, xh_ref[...]).astype(BF16)
        xb_ref[halo:, :] = x_ref[...].astype(BF16)
        acc_ref[...] = jnp.zeros_like(acc_ref)

    xb = xb_ref[...]
    hu_ref[...] = jnp.dot(xb, wu_ref[...], preferred_element_type=F32)
    hg_ref[...] = jnp.dot(xb, wg_ref[...], preferred_element_type=F32)

    def conv(h_ref, cw_ref, cb_ref):
        return (cw_ref[0:1, :] * h_ref[pl.ds(halo - 2, tm), :]
                + cw_ref[1:2, :] * h_ref[pl.ds(halo - 1, tm), :]
                + cw_ref[2:3, :] * h_ref[pl.ds(halo, tm), :]
                + cb_ref[...])

    u = conv(hu_ref, cwu_ref, cbu_ref)
    gt = conv(hg_ref, cwg_ref, cbg_ref)
    act = (gt * _sigmoid(gt)) * u
    acc_ref[...] += jnp.dot(act.astype(BF16), wo_ref[...], preferred_element_type=F32)

    @pl.when(f == n_f - 1)
    def _():
        y = DEEPNORM_ALPHA * x_ref[...] + acc_ref[...]
        out_ref[...] = _layer_norm_rows(y, g_ref[...], b_ref[...])


def _conv_ffn_ln(x, seq_len, w_in, conv_w, conv_b, w_out, g, b, *, tm, tf):
    m, d = x.shape
    ff = w_out.shape[0]
    n_f = ff // tf
    halo = BF16_ROWS
    body = functools.partial(_ffn_body, tm=tm, tiles_per_seq=seq_len // tm, n_f=n_f)
    conv_b2 = conv_b.reshape(1, 2 * ff)
    return pl.pallas_call(
        body,
        grid=(m // tm, n_f),
        in_specs=[pl.BlockSpec((tm, d), lambda i, f: (i, 0)),
                  pl.BlockSpec((halo, d), lambda i, f: (jnp.maximum(i * (tm // halo) - 1, 0), 0)),
                  pl.BlockSpec((d, tf), lambda i, f: (0, f)),
                  pl.BlockSpec((d, tf), lambda i, f: (0, n_f + f)),
                  pl.BlockSpec((3, tf), lambda i, f: (0, f)),
                  pl.BlockSpec((3, tf), lambda i, f: (0, n_f + f)),
                  pl.BlockSpec((1, tf), lambda i, f: (0, f)),
                  pl.BlockSpec((1, tf), lambda i, f: (0, n_f + f)),
                  pl.BlockSpec((tf, d), lambda i, f: (f, 0)),
                  pl.BlockSpec((1, d), lambda i, f: (0, 0)),
                  pl.BlockSpec((1, d), lambda i, f: (0, 0))],
        out_specs=pl.BlockSpec((tm, d), lambda i, f: (i, 0)),
        out_shape=jax.ShapeDtypeStruct((m, d), F32),
        scratch_shapes=[pltpu.VMEM((tm + halo, d), BF16),
                        pltpu.VMEM((tm + halo, tf), F32),
                        pltpu.VMEM((tm + halo, tf), F32),
                        pltpu.VMEM((tm, d), F32)],
        compiler_params=_cparams("parallel", "arbitrary"),
        name="conv_ffn_ln",
    )(x, x, w_in, w_in, conv_w, conv_w, conv_b2, conv_b2, w_out, g.reshape(1, d), b.reshape(1, d))


def _head_pair_sum(v, ones_bd):
    hi = v.astype(BF16)
    lo = (v - hi.astype(F32)).astype(BF16)
    return (jnp.dot(hi, ones_bd, preferred_element_type=F32)
            + jnp.dot(lo, ones_bd, preferred_element_type=F32))


def _rwkv_proj_body(x_ref, xp_ref, mix_ref, wrkv_ref, w1_ref, w2_ref, a1_ref, a2_ref, g1_ref, g2_ref,
                    w0_ref, a0_ref, kk_ref, ka_ref,
                    r_out, ld_out, k_out, v_out, kk_out, b_out, gate_out, *, tm, tiles_per_seq):
    i = pl.program_id(0)
    d = x_ref.shape[1]
    x = x_ref[...]
    first = (i % tiles_per_seq) == 0
    prev = jnp.where(first, 0.0, xp_ref[SUBLANES - 1:SUBLANES, :])
    row = lax.broadcasted_iota(jnp.int32, (tm, d), 0)
    xs = jnp.where(row == 0, prev, pltpu.roll(x, 1, 0))
    xx = xs - x

    def mixed(j):
        return (x + xx * mix_ref[j:j + 1, :]).astype(BF16)

    def mm(a, w):
        return jnp.dot(a, w, preferred_element_type=F32)

    r = mm(mixed(0), wrkv_ref[0])
    k = mm(mixed(2), wrkv_ref[1])
    v = mm(mixed(3), wrkv_ref[2])
    lw = w0_ref[...] + mm(jnp.tanh(mm(mixed(1), w1_ref[...])).astype(BF16), w2_ref[...])
    softplus_neg = jnp.maximum(-lw, 0.0) + jnp.log(1.0 + jnp.exp(-jnp.abs(lw)))
    log_decay = -jnp.exp(-softplus_neg - 0.5)
    a = _sigmoid(a0_ref[...] + mm(mm(mixed(4), a1_ref[...]).astype(BF16), a2_ref[...]))
    gate = mm(_sigmoid(mm(mixed(5), g1_ref[...])).astype(BF16), g2_ref[...])

    kk = k * kk_ref[...]
    rr = lax.broadcasted_iota(jnp.int32, (LANES, LANES), 0) // RWKV_HEAD
    cc = lax.broadcasted_iota(jnp.int32, (LANES, LANES), 1) // RWKV_HEAD
    ones_bd = jnp.where(rr == cc, 1.0, 0.0).astype(BF16)
    sq = kk * kk
    ss = jnp.concatenate([_head_pair_sum(sq[:, j * LANES:(j + 1) * LANES], ones_bd)
                          for j in range(d // LANES)], axis=1)
    kk = kk / jnp.maximum(jnp.sqrt(ss), 1e-12)

    r_out[...] = r
    ld_out[...] = log_decay
    k_out[...] = k * (1.0 + (a - 1.0) * ka_ref[...])
    v_out[...] = v
    kk_out[...] = kk
    b_out[...] = kk * a
    gate_out[...] = gate


def _rwkv_proj(x, seq_len, mix, w_rkv, w0, w1, w2, a0, a1, a2, g1, g2, k_k, k_a, *, tm):
    m, d = x.shape
    full = lambda arr: pl.BlockSpec(arr.shape, lambda i: (0,) * arr.ndim)
    vec = lambda a: a.reshape(1, d)
    args = (mix, w_rkv.astype(BF16), w1.astype(BF16), w2.astype(BF16), a1.astype(BF16), a2.astype(BF16),
            g1.astype(BF16), g2.astype(BF16), vec(w0), vec(a0), vec(k_k), vec(k_a))
    row_spec = pl.BlockSpec((tm, d), lambda i: (i, 0))
    return pl.pallas_call(
        functools.partial(_rwkv_proj_body, tm=tm, tiles_per_seq=seq_len // tm),
        grid=(m // tm,),
        in_specs=[row_spec,
                  pl.BlockSpec((SUBLANES, d), lambda i: (jnp.maximum(i * (tm // SUBLANES) - 1, 0), 0))]
                 + [full(a) for a in args],
        out_specs=[row_spec] * 7,
        out_shape=[jax.ShapeDtypeStruct((m, d), F32)] * 7,
        compiler_params=_cparams("parallel"),
        name="rwkv_proj",
    )(x, x, *args)


_RWKV_PASSES = dict(score=1, inv=1, apply=1, state=1, norm=2)


def _split_bf16(a, parts):
    out = []
    rem = a
    for i in range(parts):
        hi = rem.astype(BF16)
        out.append(hi)
        if i + 1 < parts:
            rem = rem - hi.astype(F32)
    return out


def _dot_passes(a, b, dims, passes):
    if passes == 1:
        return lax.dot_general(a.astype(BF16), b.astype(BF16), dims, preferred_element_type=F32)
    a_hi, a_lo = _split_bf16(a, 2)
    b_hi, b_lo = _split_bf16(b, 2)
    dg = functools.partial(lax.dot_general, dimension_numbers=dims, preferred_element_type=F32)
    return dg(a_hi, b_hi) + dg(a_hi, b_lo) + dg(a_lo, b_hi)


def _dot_exact_rhs(a, b_bf16, parts):
    dims = (((1,), (0,)), ((), ()))
    acc = None
    for part in _split_bf16(a, parts):
        t = lax.dot_general(part, b_bf16, dims, preferred_element_type=F32)
        acc = t if acc is None else acc + t
    return acc


def _rwkv_chunk_body(r_ref, ld_ref, k_ref, v_ref, kk_ref, b_ref, gate_ref, rk_ref, gg_ref, gb_ref,
                     o_ref, st_ref, *, chunk, pairs):
    c = pl.program_id(2)

    @pl.when(c == 0)
    def _():
        st_ref[...] = jnp.zeros_like(st_ref)

    n = RWKV_HEAD
    row = lax.broadcasted_iota(jnp.int32, (chunk, LANES), 0)
    lane = lax.broadcasted_iota(jnp.int32, (chunk, LANES), 1)
    col = lane % n
    strict = row > col
    incl = row >= col
    eye2 = jnp.where(row == col, 1.0, 0.0)
    head1 = lane >= n
    rr = lax.broadcasted_iota(jnp.int32, (LANES, LANES), 0) // n
    cc = lax.broadcasted_iota(jnp.int32, (LANES, LANES), 1) // n
    bd = rr == cc
    ones_bd = jnp.where(bd, 1.0, 0.0).astype(BF16)
    nn = (((1,), (0,)), ((), ()))
    ps = _RWKV_PASSES

    def head_stack(t):
        return jnp.concatenate([jnp.where(head1, 0.0, t), jnp.where(head1, t, 0.0)], axis=0)

    def block_diag(t):
        return jnp.where(bd, jnp.concatenate([t, t], axis=0), 0.0)

    P = range(pairs)
    sls = [slice(p * LANES, (p + 1) * LANES) for p in P]
    r = [r_ref[0, :, s] for s in sls]
    ld = [ld_ref[0, :, s] for s in sls]
    k = [k_ref[0, :, s] for s in sls]
    v = [v_ref[0, :, s] for s in sls]
    bv = [b_ref[0, :, s] for s in sls]

    cum = list(ld)
    shift = 1
    while shift < chunk:
        cum = [cu + jnp.where(row >= shift, pltpu.roll(cu, shift, 0), 0.0) for cu in cum]
        shift *= 2
    cum_last = [cu[chunk - 1:chunk, :] for cu in cum]
    e_inv = [jnp.exp(-cu) for cu in cum]
    r_hat = [r[p] * jnp.exp(cum[p]) for p in P]
    a_hat = [-(kk_ref[0, :, sls[p]] * jnp.exp(cum[p] - ld[p])) for p in P]
    lhs = [jnp.concatenate([a_hat[p], r_hat[p]], axis=0) for p in P]
    sb = [_dot_passes(lhs[p], head_stack(bv[p] * e_inv[p]), _NT, ps["score"]) for p in P]
    sk = [_dot_passes(lhs[p], head_stack(k[p] * e_inv[p]), _NT, ps["score"]) for p in P]
    a_rb = [jnp.where(incl, t[chunk:], 0.0) for t in sb]
    a_ak = [jnp.where(strict, t[:chunk], 0.0) for t in sk]
    a_rk = [jnp.where(incl, t[chunk:], 0.0) for t in sk]

    xp = [jnp.where(strict, t[:chunk], 0.0) for t in sb]
    tinv = [eye2 + t for t in xp]
    power = 2
    while power < chunk:
        xp = [_dot_passes(t, block_diag(t), nn, ps["inv"]) for t in xp]
        tinv = [tinv[p] + _dot_passes(tinv[p], block_diag(xp[p]), nn, ps["inv"]) for p in P]
        power *= 2

    s0 = [st_ref[p] for p in P]
    hv = [head_stack(t) for t in v]
    z = [_dot_passes(a_hat[p], s0[p], _NT, ps["apply"]) + _dot_passes(a_ak[p], hv[p], nn, ps["apply"]) for p in P]
    u = [_dot_passes(tinv[p], head_stack(z[p]), nn, ps["apply"]) for p in P]
    y = [_dot_passes(r_hat[p], s0[p], _NT, ps["apply"]) + _dot_passes(a_rb[p], head_stack(u[p]), nn, ps["apply"])
         + _dot_passes(a_rk[p], hv[p], nn, ps["apply"]) for p in P]
    for p in P:
        e_last = jnp.exp(cum_last[p] - cum[p])
        s_new = s0[p] * jnp.exp(cum_last[p]) + _dot_passes(
            jnp.concatenate([u[p], v[p]], axis=0),
            jnp.concatenate([bv[p] * e_last, k[p] * e_last], axis=0), _TN, ps["state"])
        st_ref[p] = jnp.where(bd, s_new, 0.0)

    mu = [_dot_exact_rhs(t, ones_bd, ps["norm"]) * (1.0 / n) for t in y]
    dlt = [y[p] - mu[p] for p in P]
    var = [_dot_exact_rhs(t * t, ones_bd, ps["norm"]) * (1.0 / n) for t in dlt]
    bonus = [_dot_exact_rhs(r[p] * k[p] * rk_ref[:, sls[p]], ones_bd, ps["norm"]) * v[p] for p in P]
    for p in P:
        yn = dlt[p] * lax.rsqrt(var[p] + RWKV_GN_EPS) * gg_ref[:, sls[p]] + gb_ref[:, sls[p]]
        o_ref[0, :, sls[p]] = ((yn + bonus[p]) * gate_ref[0, :, sls[p]]).astype(o_ref.dtype)


def _rwkv_chunk(r, ld, k, v, kk, bvec, gate, r_k, gn_g, gn_b, *, pairs):
    bsz, t_len, d = r.shape
    chunk = RWKV_HEAD
    width = pairs * LANES
    seq_spec = pl.BlockSpec((1, chunk, width), lambda b, g, c: (b, c, g))
    vec_spec = pl.BlockSpec((1, width), lambda b, g, c: (0, g))
    return pl.pallas_call(
        functools.partial(_rwkv_chunk_body, chunk=chunk, pairs=pairs),
        grid=(bsz, d // width, t_len // chunk),
        in_specs=[seq_spec] * 7 + [vec_spec] * 3,
        out_specs=seq_spec,
        out_shape=jax.ShapeDtypeStruct((bsz, t_len, d), BF16),
        scratch_shapes=[pltpu.VMEM((pairs, LANES, LANES), F32)],
        compiler_params=_cparams("parallel", "parallel", "arbitrary"),
        name="rwkv_chunk",
    )(r, ld, k, v, kk, bvec, gate, r_k.reshape(1, d), gn_g.reshape(1, d), gn_b.reshape(1, d))


def _rwkv_mixer(x, bsz, t_len, mix, w_rkv, w0, w1, w2, a0, a1, a2, g1, g2, k_k, k_a, r_k, gn_g, gn_b, w_o,
                ln_g, ln_b):
    d = x.shape[1]
    outs = _rwkv_proj(x, t_len, mix, w_rkv, w0, w1, w2, a0, a1, a2, g1, g2, k_k, k_a, tm=256)
    seq = [o.reshape(bsz, t_len, d) for o in outs]
    o = _rwkv_chunk(*seq, r_k, gn_g, gn_b, pairs=8)
    return _matmul_residual_ln(o.reshape(bsz * t_len, d), w_o.astype(BF16), x, ln_g, ln_b, tm=512)


def _ret_chunk_body(q_ref, k_ref, v_ref, gate_ref, cos_ref, sin_ref, g_ref, b_ref, o_ref, st_ref, *, chunk):
    h = pl.program_id(1)
    c = pl.program_id(2)

    @pl.when(c == 0)
    def _():
        st_ref[...] = jnp.zeros_like(st_ref)

    dk = q_ref.shape[2]
    half = dk // 2
    hv = jnp.full((1, 1), h, jnp.int32).astype(F32)
    log_gamma = jnp.log(1.0 - jnp.exp2(-5.0 - hv))
    cos = cos_ref[...]
    sin = sin_ref[...]

    def rotate(t):
        te, to = t[:, :half], t[:, half:]
        return jnp.concatenate([te * cos - to * sin, to * cos + te * sin], axis=1)

    q = rotate(q_ref[0])
    k = rotate(k_ref[0]) * (dk ** -0.5)
    v = v_ref[0].astype(BF16)

    ri = lax.broadcasted_iota(jnp.int32, (chunk, chunk), 0)
    ci = lax.broadcasted_iota(jnp.int32, (chunk, chunk), 1)
    rel = (ri - ci).astype(F32)
    inner = jnp.where(rel >= 0, jnp.exp(log_gamma * jnp.maximum(rel, 0.0)), 0.0)
    pos = lax.broadcasted_iota(jnp.int32, (chunk, dk), 0).astype(F32)
    cross = jnp.exp(log_gamma * (pos + 1.0))
    sdecay = jnp.exp(log_gamma * (chunk - 1.0 - pos))
    chunk_decay = jnp.exp(log_gamma * chunk)

    s = lax.dot_general(q.astype(BF16), k.astype(BF16), _NT, preferred_element_type=F32) * inner
    st = st_ref[...]
    o = (jnp.dot(s.astype(BF16), v, preferred_element_type=F32)
         + jnp.dot((q * cross).astype(BF16), st.astype(BF16), preferred_element_type=F32))
    st_ref[...] = st * chunk_decay + lax.dot_general((k * sdecay).astype(BF16), v, _TN,
                                                     preferred_element_type=F32)

    mu = jnp.mean(o, axis=-1, keepdims=True)
    dlt = o - mu
    var = jnp.mean(dlt * dlt, axis=-1, keepdims=True)
    on = dlt * lax.rsqrt(var + RET_GN_EPS) * g_ref[...] + b_ref[...]
    gt = gate_ref[0]
    o_ref[0] = (gt * _sigmoid(gt) * on).astype(o_ref.dtype)


def _ret_chunk(proj, cos, sin, gn_g, gn_b, *, chunk):
    bsz, t_len, six_d = proj.shape
    d = six_d // 6
    h = RET_HEADS
    dk, dv = d // h, 2 * d // h
    k_off, v_off, g_off = d // dk, 2 * d // dv, 4 * d // dv
    return pl.pallas_call(
        functools.partial(_ret_chunk_body, chunk=chunk),
        grid=(bsz, h, t_len // chunk),
        in_specs=[pl.BlockSpec((1, chunk, dk), lambda b, hh, c: (b, c, hh)),
                  pl.BlockSpec((1, chunk, dk), lambda b, hh, c: (b, c, k_off + hh)),
                  pl.BlockSpec((1, chunk, dv), lambda b, hh, c: (b, c, v_off + hh)),
                  pl.BlockSpec((1, chunk, dv), lambda b, hh, c: (b, c, g_off + hh)),
                  pl.BlockSpec((chunk, dk // 2), lambda b, hh, c: (c, 0)),
                  pl.BlockSpec((chunk, dk // 2), lambda b, hh, c: (c, 0)),
                  pl.BlockSpec((1, dv), lambda b, hh, c: (0, hh)),
                  pl.BlockSpec((1, dv), lambda b, hh, c: (0, hh))],
        out_specs=pl.BlockSpec((1, chunk, dv), lambda b, hh, c: (b, c, hh)),
        out_shape=jax.ShapeDtypeStruct((bsz, t_len, 2 * d), BF16),
        scratch_shapes=[pltpu.VMEM((dk, dv), F32)],
        compiler_params=_cparams("parallel", "parallel", "arbitrary"),
        name="retention_chunk",
    )(proj, proj, proj, proj, cos, sin, gn_g.reshape(1, 2 * d), gn_b.reshape(1, 2 * d))


def _retention_mixer(x, bsz, t_len, w_in, gn_g, gn_b, w_o, ln_g, ln_b):
    d = x.shape[1]
    h = RET_HEADS
    dk = d // h
    w_qk = w_in[:, :2 * d].reshape(d, 2 * h, dk // 2, 2).transpose(0, 1, 3, 2).reshape(d, 2 * d)
    w_perm = jnp.concatenate([w_qk, w_in[:, 2 * d:]], axis=1).astype(BF16)
    proj = _matmul(x, w_perm, tm=1024, tn=512)
    inv = 1.0 / (RET_ROPE_BASE ** jnp.linspace(0.0, 1.0, dk // 2, dtype=F32))
    ang = jnp.arange(t_len, dtype=F32)[:, None] * inv[None, :]
    o = _ret_chunk(proj.reshape(bsz, t_len, 6 * d), jnp.cos(ang), jnp.sin(ang), gn_g, gn_b, chunk=128)
    return _matmul_residual_ln(o.reshape(bsz * t_len, 2 * d), w_o.astype(BF16), x, ln_g, ln_b, tm=512)


def _moba_body(qt_ref, vt_ref, k_ref, o_ref, kmean_ref, *, n_blk):
    qb = pl.program_id(2)
    blk = MOBA_BLOCK
    hd = MOBA_HEAD_DIM

    @pl.when(qb == 0)
    def _():
        kmean_ref[...] = jnp.mean(k_ref[...].reshape(n_blk, blk, LANES), axis=1)

    kmean = kmean_ref[...]
    qt = qt_ref[...]
    zeros = jnp.zeros((hd, blk), F32)
    qz = (jnp.concatenate([qt[:hd], zeros], axis=0), jnp.concatenate([zeros, qt[hd:]], axis=0))
    qz_b = tuple(t.astype(BF16) for t in qz)
    blk_id = lax.broadcasted_iota(jnp.int32, (n_blk, blk), 0)

    sels = []
    for hh in range(2):
        gate = jnp.dot(kmean, qz[hh], precision=HIGHEST, preferred_element_type=F32)
        beaten = jnp.zeros((n_blk, blk), F32)
        for m in range(n_blk):
            gm = gate[m:m + 1, :]
            wins = jnp.where(gm > gate, 1.0, jnp.where(gm == gate, jnp.where(blk_id > m, 1.0, 0.0), 0.0))
            beaten = beaten + jnp.where(m < qb, wins, 0.0)
        sels.append(jnp.where(blk_id < qb, jnp.where(beaten < MOBA_TOPK, 1.0, 0.0), 0.0))

    key_i = lax.broadcasted_iota(jnp.int32, (blk, blk), 0)
    qry_i = lax.broadcasted_iota(jnp.int32, (blk, blk), 1)
    causal = jnp.where(key_i <= qry_i, 1.0, 0.0)
    scale = hd ** -0.5

    def step(nb, carry):
        off = pl.multiple_of(nb * blk, blk)
        kb = k_ref[pl.ds(off, blk), :].astype(BF16)
        vb = vt_ref[:, pl.ds(off, blk)].astype(BF16)
        own = nb == qb
        out = []
        for hh in range(2):
            m_i, l_i, acc = carry[hh]
            s = jnp.dot(kb, qz_b[hh], preferred_element_type=F32) * scale
            sel_row = jnp.sum(jnp.where(blk_id == nb, sels[hh], 0.0), axis=0, keepdims=True)
            allowed = jnp.where(own, causal, jnp.broadcast_to(sel_row, (blk, blk))) > 0.5
            s = jnp.where(allowed, s, NEG_BIG)
            m_new = jnp.maximum(m_i, jnp.max(s, axis=0, keepdims=True))
            p = jnp.where(allowed, jnp.exp(s - m_new), 0.0)
            alpha = jnp.exp(m_i - m_new)
            l_new = alpha * l_i + jnp.sum(p, axis=0, keepdims=True)
            pv = jnp.dot(vb[hh * hd:(hh + 1) * hd], p.astype(BF16), preferred_element_type=F32)
            out.append((m_new, l_new, alpha * acc + pv))
        return tuple(out)

    init = tuple((jnp.full((1, blk), NEG_BIG, F32), jnp.zeros((1, blk), F32), jnp.zeros((hd, blk), F32))
                 for _ in range(2))
    res = lax.fori_loop(0, qb + 1, step, init)
    for hh in range(2):
        _, l_i, acc = res[hh]
        o_ref[hh * hd:(hh + 1) * hd, :] = acc / l_i


def _moba_attention(qvt, k, bsz, t_len):
    d = k.shape[1]
    blk = MOBA_BLOCK
    n_blk = t_len // blk
    n_pairs = d // LANES
    return pl.pallas_call(
        functools.partial(_moba_body, n_blk=n_blk),
        grid=(bsz, n_pairs, n_blk),
        in_specs=[pl.BlockSpec((LANES, blk), lambda b, p, q: (p, b * n_blk + q)),
                  pl.BlockSpec((LANES, t_len), lambda b, p, q: (n_pairs + p, b)),
                  pl.BlockSpec((t_len, LANES), lambda b, p, q: (b, p))],
        out_specs=pl.BlockSpec((LANES, blk), lambda b, p, q: (p, b * n_blk + q)),
        out_shape=jax.ShapeDtypeStruct((d, bsz * t_len), F32),
        scratch_shapes=[pltpu.VMEM((n_blk, LANES), F32)],
        compiler_params=_cparams("parallel", "parallel", "arbitrary"),
        name="moba_attention",
    )(qvt, qvt, k)


def _moba_mixer(x, bsz, t_len, w_qkv, w_o, ln_g, ln_b):
    d = x.shape[1]
    assert t_len % MOBA_BLOCK == 0
    w_k = w_qkv[:, d:2 * d].astype(BF16)
    w_qv_t = jnp.concatenate([w_qkv[:, :d], w_qkv[:, 2 * d:]], axis=1).T.astype(BF16)
    k = _matmul(x, w_k, tm=1024, tn=512)
    qvt = _matmul_nt(x, w_qv_t, tm=1024, tn=512)
    ot = _moba_attention(qvt, k, bsz, t_len)
    return _matmul_residual_ln(ot, w_o.astype(BF16), x, ln_g, ln_b, tm=512, lhs_transposed=True)


def kernel(x, rwkv_mix, rwkv_w_rkv, rwkv_w0, rwkv_w1, rwkv_w2, rwkv_a0, rwkv_a1, rwkv_a2, rwkv_g1, rwkv_g2,
           rwkv_k_k, rwkv_k_a, rwkv_r_k, rwkv_gn_g, rwkv_gn_b, rwkv_w_o, ret_w_in, ret_gn_g, ret_gn_b, ret_w_o,
           moba_w_qkv, moba_w_o, ffn_w_in, ffn_conv_w, ffn_conv_b, ffn_w_out, ln1_g, ln1_b, ln2_g, ln2_b):
    bsz, t_len, d = x.shape
    h = x.reshape(bsz * t_len, d)
    for i in range(DEPTH):
        kind, j = i % N_MIXERS, i // N_MIXERS
        if kind == 0:
            h = _rwkv_mixer(h, bsz, t_len, rwkv_mix[j], rwkv_w_rkv[j], rwkv_w0[j], rwkv_w1[j], rwkv_w2[j],
                            rwkv_a0[j], rwkv_a1[j], rwkv_a2[j], rwkv_g1[j], rwkv_g2[j], rwkv_k_k[j],
                            rwkv_k_a[j], rwkv_r_k[j], rwkv_gn_g[j], rwkv_gn_b[j], rwkv_w_o[j],
                            ln1_g[i], ln1_b[i])
        elif kind == 1:
            h = _retention_mixer(h, bsz, t_len, ret_w_in[j], ret_gn_g[j], ret_gn_b[j], ret_w_o[j],
                                 ln1_g[i], ln1_b[i])
        else:
            h = _moba_mixer(h, bsz, t_len, moba_w_qkv[j], moba_w_o[j], ln1_g[i], ln1_b[i])
        h = _conv_ffn_ln(h, t_len, ffn_w_in[i].astype(BF16), ffn_conv_w[i], ffn_conv_b[i],
                         ffn_w_out[i].astype(BF16), ln2_g[i], ln2_b[i], tm=512, tf=1408)
    return h.reshape(bsz, t_len, d)
```

```python
import functools

import jax
import jax.numpy as jnp
from jax import lax
from jax.experimental import pallas as pl
from jax.experimental.pallas import tpu as pltpu

F32 = jnp.float32
BF16 = jnp.bfloat16
HIGHEST = lax.Precision.HIGHEST

DEPTH = 4
N_MIXERS = 3
RWKV_HEAD = 64
RWKV_GN_EPS = 64e-5
RET_HEADS = 4
RET_ROPE_BASE = 10000.0
RET_GN_EPS = 1e-5
MOBA_HEADS = 16
MOBA_HEAD_DIM = 64
MOBA_BLOCK = 256
MOBA_TOPK = 3
LN_EPS = 1e-5
DEEPNORM_ALPHA = (2 * DEPTH) ** 0.25

LANES = 128
SUBLANES = 8
BF16_ROWS = 16
VMEM_LIMIT_BYTES = 52 * 1024 * 1024

NEG_BIG = -1e30

_NT = (((1,), (1,)), ((), ()))
_TN = (((0,), (0,)), ((), ()))


def _cparams(*sem):
    return pltpu.CompilerParams(dimension_semantics=sem, vmem_limit_bytes=VMEM_LIMIT_BYTES)


def _sigmoid(x):
    return 1.0 / (1.0 + jnp.exp(-x))


def _layer_norm_rows(y, g, b):
    mu = jnp.mean(y, axis=-1, keepdims=True)
    d = y - mu
    var = jnp.mean(d * d, axis=-1, keepdims=True)
    return d * lax.rsqrt(var + LN_EPS) * g + b


def _mm_body(x_ref, w_ref, o_ref, xb_ref):
    @pl.when(pl.program_id(1) == 0)
    def _():
        xb_ref[...] = x_ref[...].astype(BF16)

    o_ref[...] = jnp.dot(xb_ref[...], w_ref[...], preferred_element_type=F32).astype(o_ref.dtype)


def _matmul(x, w, *, tm, tn, out_dtype=F32):
    m, k = x.shape
    n = w.shape[1]
    return pl.pallas_call(
        _mm_body,
        grid=(m // tm, n // tn),
        in_specs=[pl.BlockSpec((tm, k), lambda i, j: (i, 0)),
                  pl.BlockSpec((k, tn), lambda i, j: (0, j))],
        out_specs=pl.BlockSpec((tm, tn), lambda i, j: (i, j)),
        out_shape=jax.ShapeDtypeStruct((m, n), out_dtype),
        scratch_shapes=[pltpu.VMEM((tm, k), BF16)],
        compiler_params=_cparams("parallel", "arbitrary"),
        name="matmul",
    )(x, w)


def _mm_nt_body(x_ref, wt_ref, o_ref, xb_ref):
    @pl.when(pl.program_id(1) == 0)
    def _():
        xb_ref[...] = x_ref[...].astype(BF16)

    o_ref[...] = lax.dot_general(wt_ref[...], xb_ref[...], _NT, preferred_element_type=F32)


def _matmul_nt(x, wt, *, tm, tn):
    m, k = x.shape
    n = wt.shape[0]
    return pl.pallas_call(
        _mm_nt_body,
        grid=(m // tm, n // tn),
        in_specs=[pl.BlockSpec((tm, k), lambda i, j: (i, 0)),
                  pl.BlockSpec((tn, k), lambda i, j: (j, 0))],
        out_specs=pl.BlockSpec((tn, tm), lambda i, j: (j, i)),
        out_shape=jax.ShapeDtypeStruct((n, m), F32),
        scratch_shapes=[pltpu.VMEM((tm, k), BF16)],
        compiler_params=_cparams("parallel", "arbitrary"),
        name="matmul_nt",
    )(x, wt)


def _mm_res_ln_body(o_ref, w_ref, x_ref, g_ref, b_ref, out_ref, *, lhs_transposed):
    lhs = o_ref[...].astype(BF16)
    if lhs_transposed:
        acc = lax.dot_general(lhs, w_ref[...], _TN, preferred_element_type=F32)
    else:
        acc = jnp.dot(lhs, w_ref[...], preferred_element_type=F32)
    y = DEEPNORM_ALPHA * x_ref[...] + acc
    out_ref[...] = _layer_norm_rows(y, g_ref[...], b_ref[...])


def _matmul_residual_ln(o, w, x, g, b, *, tm, lhs_transposed=False):
    m, d = x.shape
    k = w.shape[0]
    if lhs_transposed:
        o_spec = pl.BlockSpec((k, tm), lambda i: (0, i))
    else:
        o_spec = pl.BlockSpec((tm, k), lambda i: (i, 0))
    return pl.pallas_call(
        functools.partial(_mm_res_ln_body, lhs_transposed=lhs_transposed),
        grid=(m // tm,),
        in_specs=[o_spec,
                  pl.BlockSpec((k, d), lambda i: (0, 0)),
                  pl.BlockSpec((tm, d), lambda i: (i, 0)),
                  pl.BlockSpec((1, d), lambda i: (0, 0)),
                  pl.BlockSpec((1, d), lambda i: (0, 0))],
        out_specs=pl.BlockSpec((tm, d), lambda i: (i, 0)),
        out_shape=jax.ShapeDtypeStruct((m, d), F32),
        compiler_params=_cparams("parallel"),
        name="out_proj_residual_ln",
    )(o, w, x, g.reshape(1, d), b.reshape(1, d))


def _ffn_body(x_ref, xh_ref, wu_ref, wg_ref, cwu_ref, cwg_ref, cbu_ref, cbg_ref, wo_ref, g_ref, b_ref,
              out_ref, xb_ref, hu_ref, hg_ref, acc_ref, *, tm, tiles_per_seq, n_f):
    i = pl.program_id(0)
    f = pl.program_id(1)
    halo = BF16_ROWS

    @pl.when(f == 0)
    def _():
        first = (i % tiles_per_seq) == 0
        xb_ref[0:halo, :] = jnp.where(first, 0.0, xh_ref[...]).astype(BF16)
        xb_ref[halo:, :] = x_ref[...].astype(BF16)
        acc_ref[...] = jnp.zeros_like(acc_ref)

    xb = xb_ref[...]
    hu_ref[...] = jnp.dot(xb, wu_ref[...], preferred_element_type=F32)
    hg_ref[...] = jnp.dot(xb, wg_ref[...], preferred_element_type=F32)

    def conv(h_ref, cw_ref, cb_ref):
        return (cw_ref[0:1, :] * h_ref[pl.ds(halo - 2, tm), :]
                + cw_ref[1:2, :] * h_ref[pl.ds(halo - 1, tm), :]
                + cw_ref[2:3, :] * h_ref[pl.ds(halo, tm), :]
                + cb_ref[...])

    u = conv(hu_ref, cwu_ref, cbu_ref)
    gt = conv(hg_ref, cwg_ref, cbg_ref)
    act = (gt * _sigmoid(gt)) * u
    acc_ref[...] += jnp.dot(act.astype(BF16), wo_ref[...], preferred_element_type=F32)

    @pl.when(f == n_f - 1)
    def _():
        y = DEEPNORM_ALPHA * x_ref[...] + acc_ref[...]
        out_ref[...] = _layer_norm_rows(y, g_ref[...], b_ref[...])


def _conv_ffn_ln(x, seq_len, w_in, conv_w, conv_b, w_out, g, b, *, tm, tf):
    m, d = x.shape
    ff = w_out.shape[0]
    n_f = ff // tf
    halo = BF16_ROWS
    body = functools.partial(_ffn_body, tm=tm, tiles_per_seq=seq_len // tm, n_f=n_f)
    conv_b2 = conv_b.reshape(1, 2 * ff)
    return pl.pallas_call(
        body,
        grid=(m // tm, n_f),
        in_specs=[pl.BlockSpec((tm, d), lambda i, f: (i, 0)),
                  pl.BlockSpec((halo, d), lambda i, f: (jnp.maximum(i * (tm // halo) - 1, 0), 0)),
                  pl.BlockSpec((d, tf), lambda i, f: (0, f)),
                  pl.BlockSpec((d, tf), lambda i, f: (0, n_f + f)),
                  pl.BlockSpec((3, tf), lambda i, f: (0, f)),
                  pl.BlockSpec((3, tf), lambda i, f: (0, n_f + f)),
                  pl.BlockSpec((1, tf), lambda i, f: (0, f)),
                  pl.BlockSpec((1, tf), lambda i, f: (0, n_f + f)),
                  pl.BlockSpec((tf, d), lambda i, f: (f, 0)),
                  pl.BlockSpec((1, d), lambda i, f: (0, 0)),
                  pl.BlockSpec((1, d), lambda i, f: (0, 0))],
        out_specs=pl.BlockSpec((tm, d), lambda i, f: (i, 0)),
        out_shape=jax.ShapeDtypeStruct((m, d), F32),
        scratch_shapes=[pltpu.VMEM((tm + halo, d), BF16),
                        pltpu.VMEM((tm + halo, tf), F32),
                        pltpu.VMEM((tm + halo, tf), F32),
                        pltpu.VMEM((tm, d), F32)],
        compiler_params=_cparams("parallel", "arbitrary"),
        name="conv_ffn_ln",
    )(x, x, w_in, w_in, conv_w, conv_w, conv_b2, conv_b2, w_out, g.reshape(1, d), b.reshape(1, d))


def _head_pair_sum(v, ones_bd):
    hi = v.astype(BF16)
    lo = (v - hi.astype(F32)).astype(BF16)
    return (jnp.dot(hi, ones_bd, preferred_element_type=F32)
            + jnp.dot(lo, ones_bd, preferred_element_type=F32))


def _rwkv_proj_body(x_ref, xp_ref, mix_ref, wrkv_ref, w1_ref, w2_ref, a1_ref, a2_ref, g1_ref, g2_ref,
                    w0_ref, a0_ref, kk_ref, ka_ref,
                    r_out, ld_out, k_out, v_out, kk_out, b_out, gate_out, *, tm, tiles_per_seq):
    i = pl.program_id(0)
    d = x_ref.shape[1]
    x = x_ref[...]
    first = (i % tiles_per_seq) == 0
    prev = jnp.where(first, 0.0, xp_ref[SUBLANES - 1:SUBLANES, :])
    row = lax.broadcasted_iota(jnp.int32, (tm, d), 0)
    xs = jnp.where(row == 0, prev, pltpu.roll(x, 1, 0))
    xx = xs - x

    def mixed(j):
        return (x + xx * mix_ref[j:j + 1, :]).astype(BF16)

    def mm(a, w):
        return jnp.dot(a, w, preferred_element_type=F32)

    r = mm(mixed(0), wrkv_ref[0])
    k = mm(mixed(2), wrkv_ref[1])
    v = mm(mixed(3), wrkv_ref[2])
    lw = w0_ref[...] + mm(jnp.tanh(mm(mixed(1), w1_ref[...])).astype(BF16), w2_ref[...])
    softplus_neg = jnp.maximum(-lw, 0.0) + jnp.log(1.0 + jnp.exp(-jnp.abs(lw)))
    log_decay = -jnp.exp(-softplus_neg - 0.5)
    a = _sigmoid(a0_ref[...] + mm(mm(mixed(4), a1_ref[...]).astype(BF16), a2_ref[...]))
    gate = mm(_sigmoid(mm(mixed(5), g1_ref[...])).astype(BF16), g2_ref[...])

    kk = k * kk_ref[...]
    rr = lax.broadcasted_iota(jnp.int32, (LANES, LANES), 0) // RWKV_HEAD
    cc = lax.broadcasted_iota(jnp.int32, (LANES, LANES), 1) // RWKV_HEAD
    ones_bd = jnp.where(rr == cc, 1.0, 0.0).astype(BF16)
    sq = kk * kk
    ss = jnp.concatenate([_head_pair_sum(sq[:, j * LANES:(j + 1) * LANES], ones_bd)
                          for j in range(d // LANES)], axis=1)
    kk = kk / jnp.maximum(jnp.sqrt(ss), 1e-12)

    r_out[...] = r
    ld_out[...] = log_decay
    k_out[...] = k * (1.0 + (a - 1.0) * ka_ref[...])
    v_out[...] = v
    kk_out[...] = kk
    b_out[...] = kk * a
    gate_out[...] = gate


def _rwkv_proj(x, seq_len, mix, w_rkv, w0, w1, w2, a0, a1, a2, g1, g2, k_k, k_a, *, tm):
    m, d = x.shape
    full = lambda arr: pl.BlockSpec(arr.shape, lambda i: (0,) * arr.ndim)
    vec = lambda a: a.reshape(1, d)
    args = (mix, w_rkv.astype(BF16), w1.astype(BF16), w2.astype(BF16), a1.astype(BF16), a2.astype(BF16),
            g1.astype(BF16), g2.astype(BF16), vec(w0), vec(a0), vec(k_k), vec(k_a))
    row_spec = pl.BlockSpec((tm, d), lambda i: (i, 0))
    return pl.pallas_call(
        functools.partial(_rwkv_proj_body, tm=tm, tiles_per_seq=seq_len // tm),
        grid=(m // tm,),
        in_specs=[row_spec,
                  pl.BlockSpec((SUBLANES, d), lambda i: (jnp.maximum(i * (tm // SUBLANES) - 1, 0), 0))]
                 + [full(a) for a in args],
        out_specs=[row_spec] * 7,
        out_shape=[jax.ShapeDtypeStruct((m, d), F32)] * 7,
        compiler_params=_cparams("parallel"),
        name="rwkv_proj",
    )(x, x, *args)


def _bf16_dot(a, b, dims):
    return lax.dot_general(a.astype(BF16), b.astype(BF16), dims, preferred_element_type=F32)


def _head_sums(tiles, ones_bd):
    rows = tiles[0].shape[0]
    parts = []
    for t in tiles:
        hi = t.astype(BF16)
        parts += [hi, (t - hi.astype(F32)).astype(BF16)]
    res = jnp.dot(jnp.concatenate(parts, axis=0), ones_bd, preferred_element_type=F32)
    return [res[2 * i * rows:(2 * i + 1) * rows] + res[(2 * i + 1) * rows:(2 * i + 2) * rows]
            for i in range(len(tiles))]


def _rwkv_chunk_body(r_ref, ld_ref, k_ref, v_ref, kk_ref, b_ref, gate_ref, rk_ref, gg_ref, gb_ref,
                     o_ref, st_ref, *, chunk, pairs):
    c = pl.program_id(2)

    @pl.when(c == 0)
    def _():
        st_ref[...] = jnp.zeros_like(st_ref)

    n = RWKV_HEAD
    row = lax.broadcasted_iota(jnp.int32, (chunk, LANES), 0)
    lane = lax.broadcasted_iota(jnp.int32, (chunk, LANES), 1)
    col = lane % n
    eye2 = jnp.where(row == col, 1.0, 0.0)
    head1 = lane >= n
    row2 = lax.broadcasted_iota(jnp.int32, (2 * chunk, LANES), 0)
    col2 = lax.broadcasted_iota(jnp.int32, (2 * chunk, LANES), 1) % n
    tri2 = jnp.where(row2 < chunk, row2 - 1, row2 - chunk) >= col2
    rr = lax.broadcasted_iota(jnp.int32, (LANES, LANES), 0) // n
    cc = lax.broadcasted_iota(jnp.int32, (LANES, LANES), 1) // n
    bd = rr == cc
    ones_bd = jnp.where(bd, 1.0, 0.0).astype(BF16)
    nn = (((1,), (0,)), ((), ()))

    def head_stack(t):
        return jnp.concatenate([jnp.where(head1, 0.0, t), jnp.where(head1, t, 0.0)], axis=0)

    def block_diag(t):
        return jnp.where(bd, jnp.concatenate([t, t], axis=0), 0.0)

    P = range(pairs)
    sls = [slice(p * LANES, (p + 1) * LANES) for p in P]
    r = [r_ref[0, :, s] for s in sls]
    ld = [ld_ref[0, :, s] for s in sls]
    k = [k_ref[0, :, s] for s in sls]
    v = [v_ref[0, :, s] for s in sls]
    bv = [b_ref[0, :, s] for s in sls]

    cum = list(ld)
    shift = 1
    while shift < chunk:
        cum = [cu + jnp.where(row >= shift, pltpu.roll(cu, shift, 0), 0.0) for cu in cum]
        shift *= 2
    cum_last = [cu[chunk - 1:chunk, :] for cu in cum]
    e_inv = [jnp.exp(-cu) for cu in cum]
    r_hat = [r[p] * jnp.exp(cum[p]) for p in P]
    a_hat = [-(kk_ref[0, :, sls[p]] * jnp.exp(cum[p] - ld[p])) for p in P]
    lhs = [jnp.concatenate([a_hat[p], r_hat[p]], axis=0) for p in P]
    sb = [jnp.where(tri2, _bf16_dot(lhs[p], head_stack(bv[p] * e_inv[p]), _NT), 0.0) for p in P]
    sk = [jnp.where(tri2, _bf16_dot(lhs[p], head_stack(k[p] * e_inv[p]), _NT), 0.0) for p in P]
    a_ab = [t[:chunk] for t in sb]
    a_rb = [t[chunk:] for t in sb]

    xp = [_bf16_dot(t, block_diag(t), nn) for t in a_ab]
    tinv = [eye2 + t for t in a_ab]
    power = 2
    while 2 * power < chunk:
        both = [_bf16_dot(jnp.concatenate([xp[p], tinv[p]], axis=0), block_diag(xp[p]), nn) for p in P]
        xp = [t[:chunk] for t in both]
        tinv = [tinv[p] + both[p][chunk:] for p in P]
        power *= 2
    tinv = [tinv[p] + _bf16_dot(tinv[p], block_diag(xp[p]), nn) for p in P]

    s0 = [st_ref[p] for p in P]
    zy = [_bf16_dot(lhs[p], s0[p], _NT) + _bf16_dot(sk[p], head_stack(v[p]), nn) for p in P]
    u = [_bf16_dot(tinv[p], head_stack(zy[p][:chunk]), nn) for p in P]
    y = [zy[p][chunk:] + _bf16_dot(a_rb[p], head_stack(u[p]), nn) for p in P]
    for p in P:
        e_last = jnp.exp(cum_last[p] - cum[p])
        s_new = s0[p] * jnp.exp(cum_last[p]) + _bf16_dot(
            jnp.concatenate([u[p], v[p]], axis=0),
            jnp.concatenate([bv[p] * e_last, k[p] * e_last], axis=0), _TN)
        st_ref[p] = jnp.where(bd, s_new, 0.0)

    sums = _head_sums(y + [r[p] * k[p] * rk_ref[:, sls[p]] for p in P], ones_bd)
    dlt = [y[p] - sums[p] * (1.0 / n) for p in P]
    var = _head_sums([t * t for t in dlt], ones_bd)
    for p in P:
        yn = dlt[p] * lax.rsqrt(var[p] * (1.0 / n) + RWKV_GN_EPS) * gg_ref[:, sls[p]] + gb_ref[:, sls[p]]
        bonus = sums[pairs + p] * v[p]
        o_ref[0, :, sls[p]] = ((yn + bonus) * gate_ref[0, :, sls[p]]).astype(o_ref.dtype)


def _rwkv_chunk(r, ld, k, v, kk, bvec, gate, r_k, gn_g, gn_b, *, pairs):
    bsz, t_len, d = r.shape
    chunk = RWKV_HEAD
    width = pairs * LANES
    seq_spec = pl.BlockSpec((1, chunk, width), lambda b, g, c: (b, c, g))
    vec_spec = pl.BlockSpec((1, width), lambda b, g, c: (0, g))
    return pl.pallas_call(
        functools.partial(_rwkv_chunk_body, chunk=chunk, pairs=pairs),
        grid=(bsz, d // width, t_len // chunk),
        in_specs=[seq_spec] * 7 + [vec_spec] * 3,
        out_specs=seq_spec,
        out_shape=jax.ShapeDtypeStruct((bsz, t_len, d), BF16),
        scratch_shapes=[pltpu.VMEM((pairs, LANES, LANES), F32)],
        compiler_params=_cparams("parallel", "parallel", "arbitrary"),
        name="rwkv_chunk",
    )(r, ld, k, v, kk, bvec, gate, r_k.reshape(1, d), gn_g.reshape(1, d), gn_b.reshape(1, d))


def _rwkv_mixer(x, bsz, t_len, mix, w_rkv, w0, w1, w2, a0, a1, a2, g1, g2, k_k, k_a, r_k, gn_g, gn_b, w_o,
                ln_g, ln_b):
    d = x.shape[1]
    outs = _rwkv_proj(x, t_len, mix, w_rkv, w0, w1, w2, a0, a1, a2, g1, g2, k_k, k_a, tm=256)
    seq = [o.reshape(bsz, t_len, d) for o in outs]
    o = _rwkv_chunk(*seq, r_k, gn_g, gn_b, pairs=8)
    return _matmul_residual_ln(o.reshape(bsz * t_len, d), w_o.astype(BF16), x, ln_g, ln_b, tm=512)


def _ret_chunk_body(q_ref, k_ref, v_ref, gate_ref, cos_ref, sin_ref, g_ref, b_ref, o_ref, st_ref, *, chunk):
    h = pl.program_id(1)
    c = pl.program_id(2)

    @pl.when(c == 0)
    def _():
        st_ref[...] = jnp.zeros_like(st_ref)

    dk = q_ref.shape[2]
    half = dk // 2
    hv = jnp.full((1, 1), h, jnp.int32).astype(F32)
    log_gamma = jnp.log(1.0 - jnp.exp2(-5.0 - hv))
    cos = cos_ref[...]
    sin = sin_ref[...]

    def rotate(t):
        te, to = t[:, :half], t[:, half:]
        return jnp.concatenate([te * cos - to * sin, to * cos + te * sin], axis=1)

    q = rotate(q_ref[0])
    k = rotate(k_ref[0]) * (dk ** -0.5)
    v = v_ref[0].astype(BF16)

    ri = lax.broadcasted_iota(jnp.int32, (chunk, chunk), 0)
    ci = lax.broadcasted_iota(jnp.int32, (chunk, chunk), 1)
    rel = (ri - ci).astype(F32)
    inner = jnp.where(rel >= 0, jnp.exp(log_gamma * jnp.maximum(rel, 0.0)), 0.0)
    pos = lax.broadcasted_iota(jnp.int32, (chunk, dk), 0).astype(F32)
    cross = jnp.exp(log_gamma * (pos + 1.0))
    sdecay = jnp.exp(log_gamma * (chunk - 1.0 - pos))
    chunk_decay = jnp.exp(log_gamma * chunk)

    s = lax.dot_general(q.astype(BF16), k.astype(BF16), _NT, preferred_element_type=F32) * inner
    st = st_ref[...]
    o = (jnp.dot(s.astype(BF16), v, preferred_element_type=F32)
         + jnp.dot((q * cross).astype(BF16), st.astype(BF16), preferred_element_type=F32))
    st_ref[...] = st * chunk_decay + lax.dot_general((k * sdecay).astype(BF16), v, _TN,
                                                     preferred_element_type=F32)

    mu = jnp.mean(o, axis=-1, keepdims=True)
    dlt = o - mu
    var = jnp.mean(dlt * dlt, axis=-1, keepdims=True)
    on = dlt * lax.rsqrt(var + RET_GN_EPS) * g_ref[...] + b_ref[...]
    gt = gate_ref[0]
    o_ref[0] = (gt * _sigmoid(gt) * on).astype(o_ref.dtype)


def _ret_chunk(proj, cos, sin, gn_g, gn_b, *, chunk):
    bsz, t_len, six_d = proj.shape
    d = six_d // 6
    h = RET_HEADS
    dk, dv = d // h, 2 * d // h
    k_off, v_off, g_off = d // dk, 2 * d // dv, 4 * d // dv
    return pl.pallas_call(
        functools.partial(_ret_chunk_body, chunk=chunk),
        grid=(bsz, h, t_len // chunk),
        in_specs=[pl.BlockSpec((1, chunk, dk), lambda b, hh, c: (b, c, hh)),
                  pl.BlockSpec((1, chunk, dk), lambda b, hh, c: (b, c, k_off + hh)),
                  pl.BlockSpec((1, chunk, dv), lambda b, hh, c: (b, c, v_off + hh)),
                  pl.BlockSpec((1, chunk, dv), lambda b, hh, c: (b, c, g_off + hh)),
                  pl.BlockSpec((chunk, dk // 2), lambda b, hh, c: (c, 0)),
                  pl.BlockSpec((chunk, dk // 2), lambda b, hh, c: (c, 0)),
                  pl.BlockSpec((1, dv), lambda b, hh, c: (0, hh)),
                  pl.BlockSpec((1, dv), lambda b, hh, c: (0, hh))],
        out_specs=pl.BlockSpec((1, chunk, dv), lambda b, hh, c: (b, c, hh)),
        out_shape=jax.ShapeDtypeStruct((bsz, t_len, 2 * d), BF16),
        scratch_shapes=[pltpu.VMEM((dk, dv), F32)],
        compiler_params=_cparams("parallel", "parallel", "arbitrary"),
        name="retention_chunk",
    )(proj, proj, proj, proj, cos, sin, gn_g.reshape(1, 2 * d), gn_b.reshape(1, 2 * d))


def _retention_mixer(x, bsz, t_len, w_in, gn_g, gn_b, w_o, ln_g, ln_b):
    d = x.shape[1]
    h = RET_HEADS
    dk = d // h
    w_qk = w_in[:, :2 * d].reshape(d, 2 * h, dk // 2, 2).transpose(0, 1, 3, 2).reshape(d, 2 * d)
    w_perm = jnp.concatenate([w_qk, w_in[:, 2 * d:]], axis=1).astype(BF16)
    proj = _matmul(x, w_perm, tm=1024, tn=512)
    inv = 1.0 / (RET_ROPE_BASE ** jnp.linspace(0.0, 1.0, dk // 2, dtype=F32))
    ang = jnp.arange(t_len, dtype=F32)[:, None] * inv[None, :]
    o = _ret_chunk(proj.reshape(bsz, t_len, 6 * d), jnp.cos(ang), jnp.sin(ang), gn_g, gn_b, chunk=128)
    return _matmul_residual_ln(o.reshape(bsz * t_len, 2 * d), w_o.astype(BF16), x, ln_g, ln_b, tm=512)


def _moba_body(qt_ref, vt_ref, k_ref, o_ref, kmean_ref, kb_ref, va_ref, bias_ref, *, n_blk, pairs):
    qb = pl.program_id(2)
    blk = MOBA_BLOCK
    hd = MOBA_HEAD_DIM
    heads = 2 * pairs
    aug = hd + BF16_ROWS
    t_len = n_blk * blk
    H = range(heads)

    @pl.when(qb == 0)
    def _():
        kf = k_ref[...]
        kmean_ref[...] = jnp.mean(kf.reshape(n_blk, blk, pairs * LANES), axis=1)
        kb_ref[...] = kf.astype(BF16)
        ones = jnp.ones((BF16_ROWS, t_len), BF16)
        for h in H:
            va_ref[h, 0:hd, :] = vt_ref[h * hd:(h + 1) * hd, :].astype(BF16)
            va_ref[h, hd:aug, :] = ones

    kmean = kmean_ref[...]
    qt = qt_ref[...]
    zeros = jnp.zeros((hd, blk), F32)
    blk_id = lax.broadcasted_iota(jnp.int32, (n_blk, blk), 0)
    scale = hd ** -0.5

    qz_b = []
    for h in H:
        qh = qt[h * hd:(h + 1) * hd]
        qz = jnp.concatenate([qh, zeros] if h % 2 == 0 else [zeros, qh], axis=0)
        lanes = slice((h // 2) * LANES, (h // 2 + 1) * LANES)
        gate = jnp.dot(kmean[:, lanes], qz, precision=HIGHEST, preferred_element_type=F32)
        beaten = jnp.zeros((n_blk, blk), F32)
        for m in range(n_blk):
            gm = gate[m:m + 1, :]
            wins = jnp.where(gm > gate, 1.0, jnp.where(gm == gate, jnp.where(blk_id > m, 1.0, 0.0), 0.0))
            beaten = beaten + jnp.where(m < qb, wins, 0.0)
        bias_ref[h] = jnp.where(blk_id < qb, jnp.where(beaten < MOBA_TOPK, 0.0, NEG_BIG), NEG_BIG)
        qz_b.append((qz * scale).astype(BF16))

    def attend(off, biases, carry):
        kb = kb_ref[pl.ds(off, blk), :]
        s = [jnp.dot(kb[:, (h // 2) * LANES:(h // 2 + 1) * LANES], qz_b[h], preferred_element_type=F32)
             + biases[h] for h in H]
        m_new = [jnp.maximum(carry[h][0], jnp.max(s[h], axis=0, keepdims=True)) for h in H]
        p = [jnp.exp(s[h] - m_new[h]).astype(BF16) for h in H]
        alpha = [jnp.exp(carry[h][0] - m_new[h]) for h in H]
        acc = [alpha[h] * carry[h][1]
               + jnp.dot(va_ref[h, :, pl.ds(off, blk)], p[h], preferred_element_type=F32) for h in H]
        return tuple((m_new[h], acc[h]) for h in H)

    def past_block(nb, carry):
        biases = [bias_ref[h, pl.ds(nb, 1), :] for h in H]
        return attend(pl.multiple_of(nb * blk, blk), biases, carry)

    init = tuple((jnp.full((1, blk), NEG_BIG, F32), jnp.zeros((aug, blk), F32)) for _ in H)
    carry = lax.fori_loop(0, qb, past_block, init)
    key_i = lax.broadcasted_iota(jnp.int32, (blk, blk), 0)
    qry_i = lax.broadcasted_iota(jnp.int32, (blk, blk), 1)
    causal_bias = jnp.where(key_i <= qry_i, 0.0, NEG_BIG)
    res = attend(pl.multiple_of(qb * blk, blk), [causal_bias] * heads, carry)
    for h in H:
        acc = res[h][1]
        o_ref[h * hd:(h + 1) * hd, :] = acc[:hd] / acc[hd:hd + 1]


def _moba_attention(qvt, k, bsz, t_len, *, pairs):
    d = k.shape[1]
    blk = MOBA_BLOCK
    n_blk = t_len // blk
    width = pairs * LANES
    n_grp = d // width
    heads = 2 * pairs
    return pl.pallas_call(
        functools.partial(_moba_body, n_blk=n_blk, pairs=pairs),
        grid=(bsz, n_grp, n_blk),
        in_specs=[pl.BlockSpec((width, blk), lambda b, p, q: (p, b * n_blk + q)),
                  pl.BlockSpec((width, t_len), lambda b, p, q: (n_grp + p, b)),
                  pl.BlockSpec((t_len, width), lambda b, p, q: (b, p))],
        out_specs=pl.BlockSpec((width, blk), lambda b, p, q: (p, b * n_blk + q)),
        out_shape=jax.ShapeDtypeStruct((d, bsz * t_len), F32),
        scratch_shapes=[pltpu.VMEM((n_blk, width), F32),
                        pltpu.VMEM((t_len, width), BF16),
                        pltpu.VMEM((heads, MOBA_HEAD_DIM + BF16_ROWS, t_len), BF16),
                        pltpu.VMEM((heads, n_blk, blk), F32)],
        compiler_params=_cparams("parallel", "parallel", "arbitrary"),
        name="moba_attention",
    )(qvt, qvt, k)


def _moba_mixer(x, bsz, t_len, w_qkv, w_o, ln_g, ln_b):
    d = x.shape[1]
    assert t_len % MOBA_BLOCK == 0
    w_k = w_qkv[:, d:2 * d].astype(BF16)
    w_qv_t = jnp.concatenate([w_qkv[:, :d], w_qkv[:, 2 * d:]], axis=1).T.astype(BF16)
    k = _matmul(x, w_k, tm=1024, tn=512)
    qvt = _matmul_nt(x, w_qv_t, tm=1024, tn=512)
    ot = _moba_attention(qvt, k, bsz, t_len, pairs=2)
    return _matmul_residual_ln(ot, w_o.astype(BF16), x, ln_g, ln_b, tm=512, lhs_transposed=True)


def kernel(x, rwkv_mix, rwkv_w_rkv, rwkv_w0, rwkv_w1, rwkv_w2, rwkv_a0, rwkv_a1, rwkv_a2, rwkv_g1, rwkv_g2,
           rwkv_k_k, rwkv_k_a, rwkv_r_k, rwkv_gn_g, rwkv_gn_b, rwkv_w_o, ret_w_in, ret_gn_g, ret_gn_b, ret_w_o,
           moba_w_qkv, moba_w_o, ffn_w_in, ffn_conv_w, ffn_conv_b, ffn_w_out, ln1_g, ln1_b, ln2_g, ln2_b):
    bsz, t_len, d = x.shape
    h = x.reshape(bsz * t_len, d)
    for i in range(DEPTH):
        kind, j = i % N_MIXERS, i // N_MIXERS
        if kind == 0:
            h = _rwkv_mixer(h, bsz, t_len, rwkv_mix[j], rwkv_w_rkv[j], rwkv_w0[j], rwkv_w1[j], rwkv_w2[j],
                            rwkv_a0[j], rwkv_a1[j], rwkv_a2[j], rwkv_g1[j], rwkv_g2[j], rwkv_k_k[j],
                            rwkv_k_a[j], rwkv_r_k[j], rwkv_gn_g[j], rwkv_gn_b[j], rwkv_w_o[j],
                            ln1_g[i], ln1_b[i])
        elif kind == 1:
            h = _retention_mixer(h, bsz, t_len, ret_w_in[j], ret_gn_g[j], ret_gn_b[j], ret_w_o[j],
                                 ln1_g[i], ln1_b[i])
        else:
            h = _moba_mixer(h, bsz, t_len, moba_w_qkv[j], moba_w_o[j], ln1_g[i], ln1_b[i])
        h = _conv_ffn_ln(h, t_len, ffn_w_in[i].astype(BF16), ffn_conv_w[i], ffn_conv_b[i],
                         ffn_w_out[i].astype(BF16), ln2_g[i], ln2_b[i], tm=512, tf=1408)
    return h.reshape(bsz, t_len, d)
```

```python
import functools
import math

import jax
import jax.numpy as jnp
from jax import lax
from jax.experimental import pallas as pl
from jax.experimental.pallas import tpu as pltpu

F32 = jnp.float32
BF16 = jnp.bfloat16
HIGHEST = lax.Precision.HIGHEST

DEPTH = 4
N_MIXERS = 3
RWKV_HEAD = 64
RWKV_GN_EPS = 64e-5
RET_HEADS = 4
RET_ROPE_BASE = 10000.0
RET_GN_EPS = 1e-5
MOBA_HEADS = 16
MOBA_HEAD_DIM = 64
MOBA_BLOCK = 256
MOBA_TOPK = 3
LN_EPS = 1e-5
DEEPNORM_ALPHA = (2 * DEPTH) ** 0.25

LANES = 128
SUBLANES = 8
BF16_ROWS = 16
VMEM_LIMIT_BYTES = 52 * 1024 * 1024

NEG_BIG = -1e30

_NT = (((1,), (1,)), ((), ()))
_TN = (((0,), (0,)), ((), ()))


def _cparams(*sem):
    return pltpu.CompilerParams(dimension_semantics=sem, vmem_limit_bytes=VMEM_LIMIT_BYTES)


def _sigmoid(x):
    return 1.0 / (1.0 + jnp.exp(-x))


def _layer_norm_rows(y, g, b):
    mu = jnp.mean(y, axis=-1, keepdims=True)
    d = y - mu
    var = jnp.mean(d * d, axis=-1, keepdims=True)
    return d * lax.rsqrt(var + LN_EPS) * g + b


def _mm_body(x_ref, w_ref, o_ref, xb_ref):
    @pl.when(pl.program_id(1) == 0)
    def _():
        xb_ref[...] = x_ref[...].astype(BF16)

    o_ref[...] = jnp.dot(xb_ref[...], w_ref[...], preferred_element_type=F32).astype(o_ref.dtype)


def _matmul(x, w, *, tm, tn, out_dtype=F32):
    m, k = x.shape
    n = w.shape[1]
    return pl.pallas_call(
        _mm_body,
        grid=(m // tm, n // tn),
        in_specs=[pl.BlockSpec((tm, k), lambda i, j: (i, 0)),
                  pl.BlockSpec((k, tn), lambda i, j: (0, j))],
        out_specs=pl.BlockSpec((tm, tn), lambda i, j: (i, j)),
        out_shape=jax.ShapeDtypeStruct((m, n), out_dtype),
        scratch_shapes=[pltpu.VMEM((tm, k), BF16)],
        compiler_params=_cparams("parallel", "arbitrary"),
        name="matmul",
    )(x, w)


def _mm_nt_body(x_ref, wt_ref, o_ref, xb_ref):
    @pl.when(pl.program_id(1) == 0)
    def _():
        xb_ref[...] = x_ref[...].astype(BF16)

    o_ref[...] = lax.dot_general(wt_ref[...], xb_ref[...], _NT, preferred_element_type=F32)


def _matmul_nt(x, wt, *, tm, tn):
    m, k = x.shape
    n = wt.shape[0]
    return pl.pallas_call(
        _mm_nt_body,
        grid=(m // tm, n // tn),
        in_specs=[pl.BlockSpec((tm, k), lambda i, j: (i, 0)),
                  pl.BlockSpec((tn, k), lambda i, j: (j, 0))],
        out_specs=pl.BlockSpec((tn, tm), lambda i, j: (j, i)),
        out_shape=jax.ShapeDtypeStruct((n, m), F32),
        scratch_shapes=[pltpu.VMEM((tm, k), BF16)],
        compiler_params=_cparams("parallel", "arbitrary"),
        name="matmul_nt",
    )(x, wt)


def _mm_res_ln_body(o_ref, w_ref, x_ref, g_ref, b_ref, out_ref, *, lhs_transposed):
    lhs = o_ref[...].astype(BF16)
    if lhs_transposed:
        acc = lax.dot_general(lhs, w_ref[...], _TN, preferred_element_type=F32)
    else:
        acc = jnp.dot(lhs, w_ref[...], preferred_element_type=F32)
    y = DEEPNORM_ALPHA * x_ref[...] + acc
    out_ref[...] = _layer_norm_rows(y, g_ref[...], b_ref[...])


def _matmul_residual_ln(o, w, x, g, b, *, tm, lhs_transposed=False):
    m, d = x.shape
    k = w.shape[0]
    if lhs_transposed:
        o_spec = pl.BlockSpec((k, tm), lambda i: (0, i))
    else:
        o_spec = pl.BlockSpec((tm, k), lambda i: (i, 0))
    return pl.pallas_call(
        functools.partial(_mm_res_ln_body, lhs_transposed=lhs_transposed),
        grid=(m // tm,),
        in_specs=[o_spec,
                  pl.BlockSpec((k, d), lambda i: (0, 0)),
                  pl.BlockSpec((tm, d), lambda i: (i, 0)),
                  pl.BlockSpec((1, d), lambda i: (0, 0)),
                  pl.BlockSpec((1, d), lambda i: (0, 0))],
        out_specs=pl.BlockSpec((tm, d), lambda i: (i, 0)),
        out_shape=jax.ShapeDtypeStruct((m, d), F32),
        compiler_params=_cparams("parallel"),
        name="out_proj_residual_ln",
    )(o, w, x, g.reshape(1, d), b.reshape(1, d))


def _ffn_body(x_ref, xh_ref, w_in_ref, cw_ref, cb_ref, wo_ref, g_ref, b_ref,
              out_ref, xb_ref, xp_ref, hu_ref, hg_ref, act_ref, *, tm, tiles_per_seq, ff, cols):
    i = pl.program_id(0)
    d = x_ref.shape[1]
    halo = BF16_ROWS
    grp = SUBLANES
    n_grp = tm // grp
    first = (i % tiles_per_seq) == 0
    xb_ref[0:halo, :] = jnp.where(first, 0.0, xh_ref[...]).astype(BF16)
    xp = x_ref[...].reshape(grp, n_grp, d).swapaxes(0, 1).reshape(tm, d)
    xp_ref[...] = xp
    xb_ref[halo:, :] = xp.astype(BF16)
    sub = lax.broadcasted_iota(jnp.int32, (grp, cols), 0)

    def hidden(h_ref, c):
        h = jnp.dot(xb_ref[...], w_in_ref[:, c], preferred_element_type=F32)
        h_ref[halo:, :] = h[halo:]
        for back in (1, 2):
            last = h[halo + tm - back * grp:halo + tm - (back - 1) * grp]
            prev = jnp.where(sub == 0, h[halo - back:halo - back + 1], pltpu.roll(last, 1, 0))
            h_ref[halo - back * grp:halo - (back - 1) * grp, :] = prev

    def conv(h_ref, c):
        return (cw_ref[0:1, c] * h_ref[pl.ds(halo - 2 * grp, tm), :]
                + cw_ref[1:2, c] * h_ref[pl.ds(halo - grp, tm), :]
                + cw_ref[2:3, c] * h_ref[pl.ds(halo, tm), :]
                + cb_ref[:, c])

    for j in range(ff // cols):
        cu = slice(j * cols, (j + 1) * cols)
        cg = slice(ff + j * cols, ff + (j + 1) * cols)
        hu = hu_ref.at[j % 2]
        hg = hg_ref.at[j % 2]
        hidden(hu, cu)
        hidden(hg, cg)
        u = conv(hu, cu)
        gt = conv(hg, cg)
        act_ref[:, cu] = ((gt * _sigmoid(gt)) * u).astype(BF16)

    y = DEEPNORM_ALPHA * xp_ref[...] + jnp.dot(act_ref[...], wo_ref[...], preferred_element_type=F32)
    yn = _layer_norm_rows(y, g_ref[...], b_ref[...])
    out_ref[...] = yn.reshape(n_grp, grp, d).swapaxes(0, 1).reshape(tm, d)


def _conv_ffn_ln(x, seq_len, w_in, conv_w, conv_b, w_out, g, b, *, tm, cols):
    m, d = x.shape
    ff = w_out.shape[0]
    halo = BF16_ROWS
    body = functools.partial(_ffn_body, tm=tm, tiles_per_seq=seq_len // tm, ff=ff, cols=cols)
    resident = lambda shape: pl.BlockSpec(shape, lambda i: (0, 0), pipeline_mode=pl.Buffered(1))
    return pl.pallas_call(
        body,
        grid=(m // tm,),
        in_specs=[pl.BlockSpec((tm, d), lambda i: (i, 0)),
                  pl.BlockSpec((halo, d), lambda i: (jnp.maximum(i * (tm // halo) - 1, 0), 0)),
                  resident((d, 2 * ff)),
                  resident((3, 2 * ff)),
                  resident((1, 2 * ff)),
                  resident((ff, d)),
                  resident((1, d)),
                  resident((1, d))],
        out_specs=pl.BlockSpec((tm, d), lambda i: (i, 0)),
        out_shape=jax.ShapeDtypeStruct((m, d), F32),
        scratch_shapes=[pltpu.VMEM((tm + halo, d), BF16),
                        pltpu.VMEM((tm, d), F32),
                        pltpu.VMEM((2, tm + halo, cols), F32),
                        pltpu.VMEM((2, tm + halo, cols), F32),
                        pltpu.VMEM((tm, ff), BF16)],
        compiler_params=_cparams("parallel"),
        name="conv_ffn_ln",
    )(x, x, w_in, conv_w, conv_b.reshape(1, 2 * ff), w_out, g.reshape(1, d), b.reshape(1, d))


def _head_pair_sum(v, ones_bd):
    hi = v.astype(BF16)
    lo = (v - hi.astype(F32)).astype(BF16)
    return (jnp.dot(hi, ones_bd, preferred_element_type=F32)
            + jnp.dot(lo, ones_bd, preferred_element_type=F32))


def _rwkv_proj_body(x_ref, xp_ref, mix_ref, wrkv_ref, w1_ref, w2_ref, a1_ref, a2_ref, g1_ref, g2_ref,
                    w0_ref, a0_ref, kk_ref, ka_ref,
                    r_out, ld_out, k_out, v_out, kk_out, b_out, gate_out, *, tm, tiles_per_seq):
    i = pl.program_id(0)
    d = x_ref.shape[1]
    x = x_ref[...]
    first = (i % tiles_per_seq) == 0
    prev = jnp.where(first, 0.0, xp_ref[SUBLANES - 1:SUBLANES, :])
    row = lax.broadcasted_iota(jnp.int32, (tm, d), 0)
    xs = jnp.where(row == 0, prev, pltpu.roll(x, 1, 0))
    xx = xs - x

    def mixed(j):
        return (x + xx * mix_ref[j:j + 1, :]).astype(BF16)

    def mm(a, w):
        return jnp.dot(a, w, preferred_element_type=F32)

    r = mm(mixed(0), wrkv_ref[0])
    k = mm(mixed(2), wrkv_ref[1])
    v = mm(mixed(3), wrkv_ref[2])
    lw = w0_ref[...] + mm(jnp.tanh(mm(mixed(1), w1_ref[...])).astype(BF16), w2_ref[...])
    softplus_neg = jnp.maximum(-lw, 0.0) + jnp.log(1.0 + jnp.exp(-jnp.abs(lw)))
    log_decay = -jnp.exp(-softplus_neg - 0.5)
    a = _sigmoid(a0_ref[...] + mm(mm(mixed(4), a1_ref[...]).astype(BF16), a2_ref[...]))
    gate = mm(_sigmoid(mm(mixed(5), g1_ref[...])).astype(BF16), g2_ref[...])

    kk = k * kk_ref[...]
    rr = lax.broadcasted_iota(jnp.int32, (LANES, LANES), 0) // RWKV_HEAD
    cc = lax.broadcasted_iota(jnp.int32, (LANES, LANES), 1) // RWKV_HEAD
    ones_bd = jnp.where(rr == cc, 1.0, 0.0).astype(BF16)
    sq = kk * kk
    ss = jnp.concatenate([_head_pair_sum(sq[:, j * LANES:(j + 1) * LANES], ones_bd)
                          for j in range(d // LANES)], axis=1)
    kk = kk / jnp.maximum(jnp.sqrt(ss), 1e-12)

    r_out[...] = r
    ld_out[...] = log_decay
    k_out[...] = k * (1.0 + (a - 1.0) * ka_ref[...])
    v_out[...] = v
    kk_out[...] = kk
    b_out[...] = kk * a
    gate_out[...] = gate


def _rwkv_proj(x, seq_len, mix, w_rkv, w0, w1, w2, a0, a1, a2, g1, g2, k_k, k_a, *, tm):
    m, d = x.shape
    full = lambda arr: pl.BlockSpec(arr.shape, lambda i: (0,) * arr.ndim, pipeline_mode=pl.Buffered(1))
    vec = lambda a: a.reshape(1, d)
    args = (mix, w_rkv.astype(BF16), w1.astype(BF16), w2.astype(BF16), a1.astype(BF16), a2.astype(BF16),
            g1.astype(BF16), g2.astype(BF16), vec(w0), vec(a0), vec(k_k), vec(k_a))
    row_spec = pl.BlockSpec((tm, d), lambda i: (i, 0))
    return pl.pallas_call(
        functools.partial(_rwkv_proj_body, tm=tm, tiles_per_seq=seq_len // tm),
        grid=(m // tm,),
        in_specs=[row_spec,
                  pl.BlockSpec((SUBLANES, d), lambda i: (jnp.maximum(i * (tm // SUBLANES) - 1, 0), 0))]
                 + [full(a) for a in args],
        out_specs=[row_spec] * 7,
        out_shape=[jax.ShapeDtypeStruct((m, d), F32)] * 7,
        compiler_params=_cparams("parallel"),
        name="rwkv_proj",
    )(x, x, *args)


def _bf16_dot(a, b, dims):
    return lax.dot_general(a.astype(BF16), b.astype(BF16), dims, preferred_element_type=F32)


def _head_sums(tiles, ones_bd):
    rows = tiles[0].shape[0]
    parts = []
    for t in tiles:
        hi = t.astype(BF16)
        parts += [hi, (t - hi.astype(F32)).astype(BF16)]
    res = jnp.dot(jnp.concatenate(parts, axis=0), ones_bd, preferred_element_type=F32)
    return [res[2 * i * rows:(2 * i + 1) * rows] + res[(2 * i + 1) * rows:(2 * i + 2) * rows]
            for i in range(len(tiles))]


def _rwkv_chunk_body(r_ref, ld_ref, k_ref, v_ref, kk_ref, b_ref, gate_ref, rk_ref, gg_ref, gb_ref,
                     o_ref, st_ref, *, chunk, pairs):
    c = pl.program_id(2)

    @pl.when(c == 0)
    def _():
        st_ref[...] = jnp.zeros_like(st_ref)

    n = RWKV_HEAD
    row = lax.broadcasted_iota(jnp.int32, (chunk, LANES), 0)
    lane = lax.broadcasted_iota(jnp.int32, (chunk, LANES), 1)
    col = lane % n
    eye2 = jnp.where(row == col, 1.0, 0.0)
    head1 = lane >= n
    row2 = lax.broadcasted_iota(jnp.int32, (2 * chunk, LANES), 0)
    col2 = lax.broadcasted_iota(jnp.int32, (2 * chunk, LANES), 1) % n
    tri2 = jnp.where(row2 < chunk, row2 - 1, row2 - chunk) >= col2
    rr = lax.broadcasted_iota(jnp.int32, (LANES, LANES), 0) // n
    cc = lax.broadcasted_iota(jnp.int32, (LANES, LANES), 1) // n
    bd = rr == cc
    ones_bd = jnp.where(bd, 1.0, 0.0).astype(BF16)
    nn = (((1,), (0,)), ((), ()))

    def head_stack(t):
        return jnp.concatenate([jnp.where(head1, 0.0, t), jnp.where(head1, t, 0.0)], axis=0)

    def block_diag(t):
        return jnp.where(bd, jnp.concatenate([t, t], axis=0), 0.0)

    P = range(pairs)
    sls = [slice(p * LANES, (p + 1) * LANES) for p in P]
    r = [r_ref[0, :, s] for s in sls]
    ld = [ld_ref[0, :, s] for s in sls]
    k = [k_ref[0, :, s] for s in sls]
    v = [v_ref[0, :, s] for s in sls]
    bv = [b_ref[0, :, s] for s in sls]

    cum = list(ld)
    shift = 1
    while shift < chunk:
        cum = [cu + jnp.where(row >= shift, pltpu.roll(cu, shift, 0), 0.0) for cu in cum]
        shift *= 2
    cum_last = [cu[chunk - 1:chunk, :] for cu in cum]
    e_inv = [jnp.exp(-cu) for cu in cum]
    r_hat = [r[p] * jnp.exp(cum[p]) for p in P]
    a_hat = [-(kk_ref[0, :, sls[p]] * jnp.exp(cum[p] - ld[p])) for p in P]
    lhs = [jnp.concatenate([a_hat[p], r_hat[p]], axis=0) for p in P]
    sb = [jnp.where(tri2, _bf16_dot(lhs[p], head_stack(bv[p] * e_inv[p]), _NT), 0.0) for p in P]
    sk = [jnp.where(tri2, _bf16_dot(lhs[p], head_stack(k[p] * e_inv[p]), _NT), 0.0) for p in P]
    a_ab = [t[:chunk] for t in sb]
    a_rb = [t[chunk:] for t in sb]

    xp = [_bf16_dot(t, block_diag(t), nn) for t in a_ab]
    tinv = [eye2 + t for t in a_ab]
    power = 2
    while 2 * power < chunk:
        both = [_bf16_dot(jnp.concatenate([xp[p], tinv[p]], axis=0), block_diag(xp[p]), nn) for p in P]
        xp = [t[:chunk] for t in both]
        tinv = [tinv[p] + both[p][chunk:] for p in P]
        power *= 2
    tinv = [tinv[p] + _bf16_dot(tinv[p], block_diag(xp[p]), nn) for p in P]

    s0 = [st_ref[p] for p in P]
    zy = [_bf16_dot(lhs[p], s0[p], _NT) + _bf16_dot(sk[p], head_stack(v[p]), nn) for p in P]
    u = [_bf16_dot(tinv[p], head_stack(zy[p][:chunk]), nn) for p in P]
    y = [zy[p][chunk:] + _bf16_dot(a_rb[p], head_stack(u[p]), nn) for p in P]
    for p in P:
        e_last = jnp.exp(cum_last[p] - cum[p])
        s_new = s0[p] * jnp.exp(cum_last[p]) + _bf16_dot(
            jnp.concatenate([u[p], v[p]], axis=0),
            jnp.concatenate([bv[p] * e_last, k[p] * e_last], axis=0), _TN)
        st_ref[p] = jnp.where(bd, s_new, 0.0)

    sums = _head_sums(y + [r[p] * k[p] * rk_ref[:, sls[p]] for p in P], ones_bd)
    dlt = [y[p] - sums[p] * (1.0 / n) for p in P]
    var = _head_sums([t * t for t in dlt], ones_bd)
    for p in P:
        yn = dlt[p] * lax.rsqrt(var[p] * (1.0 / n) + RWKV_GN_EPS) * gg_ref[:, sls[p]] + gb_ref[:, sls[p]]
        bonus = sums[pairs + p] * v[p]
        o_ref[0, :, sls[p]] = ((yn + bonus) * gate_ref[0, :, sls[p]]).astype(o_ref.dtype)


def _rwkv_chunk(r, ld, k, v, kk, bvec, gate, r_k, gn_g, gn_b, *, pairs):
    bsz, t_len, d = r.shape
    chunk = RWKV_HEAD
    width = pairs * LANES
    seq_spec = pl.BlockSpec((1, chunk, width), lambda b, g, c: (b, c, g))
    vec_spec = pl.BlockSpec((1, width), lambda b, g, c: (0, g))
    return pl.pallas_call(
        functools.partial(_rwkv_chunk_body, chunk=chunk, pairs=pairs),
        grid=(bsz, d // width, t_len // chunk),
        in_specs=[seq_spec] * 7 + [vec_spec] * 3,
        out_specs=seq_spec,
        out_shape=jax.ShapeDtypeStruct((bsz, t_len, d), BF16),
        scratch_shapes=[pltpu.VMEM((pairs, LANES, LANES), F32)],
        compiler_params=_cparams("parallel", "parallel", "arbitrary"),
        name="rwkv_chunk",
    )(r, ld, k, v, kk, bvec, gate, r_k.reshape(1, d), gn_g.reshape(1, d), gn_b.reshape(1, d))


def _rwkv_mixer(x, bsz, t_len, mix, w_rkv, w0, w1, w2, a0, a1, a2, g1, g2, k_k, k_a, r_k, gn_g, gn_b, w_o,
                ln_g, ln_b):
    d = x.shape[1]
    outs = _rwkv_proj(x, t_len, mix, w_rkv, w0, w1, w2, a0, a1, a2, g1, g2, k_k, k_a, tm=512)
    seq = [o.reshape(bsz, t_len, d) for o in outs]
    o = _rwkv_chunk(*seq, r_k, gn_g, gn_b, pairs=8)
    return _matmul_residual_ln(o.reshape(bsz * t_len, d), w_o.astype(BF16), x, ln_g, ln_b, tm=1024)


def _ret_chunk_body(q_ref, k_ref, v_ref, gate_ref, cos_ref, sin_ref, g_ref, b_ref, o_ref, st_ref, *, chunk):
    c = pl.program_id(1)

    @pl.when(c == 0)
    def _():
        st_ref[...] = jnp.zeros_like(st_ref)

    n_heads, dk, dv = st_ref.shape
    half = dk // 2
    cos = cos_ref[...]
    sin = sin_ref[...]

    def rotate(t):
        te, to = t[:, :half], t[:, half:]
        return jnp.concatenate([te * cos - to * sin, to * cos + te * sin], axis=1)

    ri = lax.broadcasted_iota(jnp.int32, (chunk, chunk), 0)
    ci = lax.broadcasted_iota(jnp.int32, (chunk, chunk), 1)
    rel = (ri - ci).astype(F32)
    pos = lax.broadcasted_iota(jnp.int32, (chunk, dk), 0).astype(F32)

    for h in range(n_heads):
        log_gamma = math.log(1.0 - 2.0 ** (-5.0 - h))
        inner = jnp.where(rel >= 0, jnp.exp(log_gamma * jnp.maximum(rel, 0.0)), 0.0)
        cross = jnp.exp(log_gamma * (pos + 1.0))
        sdecay = jnp.exp(log_gamma * (chunk - 1.0 - pos))
        chunk_decay = jnp.exp(log_gamma * chunk)

        q = rotate(q_ref[0, :, h * dk:(h + 1) * dk])
        k = rotate(k_ref[0, :, h * dk:(h + 1) * dk]) * (dk ** -0.5)
        v = v_ref[0, :, h * dv:(h + 1) * dv].astype(BF16)
        s = lax.dot_general(q.astype(BF16), k.astype(BF16), _NT, preferred_element_type=F32) * inner
        st = st_ref[h]
        o = (jnp.dot(s.astype(BF16), v, preferred_element_type=F32)
             + jnp.dot((q * cross).astype(BF16), st.astype(BF16), preferred_element_type=F32))
        st_ref[h] = st * chunk_decay + lax.dot_general((k * sdecay).astype(BF16), v, _TN,
                                                       preferred_element_type=F32)

        mu = jnp.mean(o, axis=-1, keepdims=True)
        dlt = o - mu
        var = jnp.mean(dlt * dlt, axis=-1, keepdims=True)
        vs = slice(h * dv, (h + 1) * dv)
        on = dlt * lax.rsqrt(var + RET_GN_EPS) * g_ref[:, vs] + b_ref[:, vs]
        gt = gate_ref[0, :, vs]
        o_ref[0, :, vs] = (gt * _sigmoid(gt) * on).astype(o_ref.dtype)


def _ret_chunk(proj, cos, sin, gn_g, gn_b, *, chunk):
    bsz, t_len, six_d = proj.shape
    d = six_d // 6
    h = RET_HEADS
    dk, dv = d // h, 2 * d // h
    return pl.pallas_call(
        functools.partial(_ret_chunk_body, chunk=chunk),
        grid=(bsz, t_len // chunk),
        in_specs=[pl.BlockSpec((1, chunk, d), lambda b, c: (b, c, 0)),
                  pl.BlockSpec((1, chunk, d), lambda b, c: (b, c, 1)),
                  pl.BlockSpec((1, chunk, 2 * d), lambda b, c: (b, c, 1)),
                  pl.BlockSpec((1, chunk, 2 * d), lambda b, c: (b, c, 2)),
                  pl.BlockSpec((chunk, dk // 2), lambda b, c: (c, 0)),
                  pl.BlockSpec((chunk, dk // 2), lambda b, c: (c, 0)),
                  pl.BlockSpec((1, 2 * d), lambda b, c: (0, 0)),
                  pl.BlockSpec((1, 2 * d), lambda b, c: (0, 0))],
        out_specs=pl.BlockSpec((1, chunk, 2 * d), lambda b, c: (b, c, 0)),
        out_shape=jax.ShapeDtypeStruct((bsz, t_len, 2 * d), BF16),
        scratch_shapes=[pltpu.VMEM((h, dk, dv), F32)],
        compiler_params=_cparams("parallel", "arbitrary"),
        name="retention_chunk",
    )(proj, proj, proj, proj, cos, sin, gn_g.reshape(1, 2 * d), gn_b.reshape(1, 2 * d))


def _retention_mixer(x, bsz, t_len, w_in, gn_g, gn_b, w_o, ln_g, ln_b):
    d = x.shape[1]
    h = RET_HEADS
    dk = d // h
    w_qk = w_in[:, :2 * d].reshape(d, 2 * h, dk // 2, 2).transpose(0, 1, 3, 2).reshape(d, 2 * d)
    w_perm = jnp.concatenate([w_qk, w_in[:, 2 * d:]], axis=1).astype(BF16)
    proj = _matmul(x, w_perm, tm=2048, tn=512)
    inv = 1.0 / (RET_ROPE_BASE ** jnp.linspace(0.0, 1.0, dk // 2, dtype=F32))
    ang = jnp.arange(t_len, dtype=F32)[:, None] * inv[None, :]
    o = _ret_chunk(proj.reshape(bsz, t_len, 6 * d), jnp.cos(ang), jnp.sin(ang), gn_g, gn_b, chunk=256)
    return _matmul_residual_ln(o.reshape(bsz * t_len, 2 * d), w_o.astype(BF16), x, ln_g, ln_b, tm=1024)


def _moba_body(qt_ref, vt_ref, k_ref, o_ref, kmean_ref, kb_ref, va_ref, bias_ref, *, n_blk, pairs):
    qb = pl.program_id(2)
    blk = MOBA_BLOCK
    hd = MOBA_HEAD_DIM
    heads = 2 * pairs
    aug = hd + BF16_ROWS
    t_len = n_blk * blk
    H = range(heads)

    @pl.when(qb == 0)
    def _():
        kf = k_ref[...]
        kmean_ref[...] = jnp.mean(kf.reshape(n_blk, blk, pairs * LANES), axis=1)
        kb_ref[...] = kf.astype(BF16)
        ones = jnp.ones((BF16_ROWS, t_len), BF16)
        for h in H:
            va_ref[h, 0:hd, :] = vt_ref[h * hd:(h + 1) * hd, :].astype(BF16)
            va_ref[h, hd:aug, :] = ones

    kmean = kmean_ref[...]
    qt = qt_ref[...]
    zeros = jnp.zeros((hd, blk), F32)
    blk_id = lax.broadcasted_iota(jnp.int32, (n_blk, blk), 0)
    scale = hd ** -0.5

    qz_b = []
    for h in H:
        qh = qt[h * hd:(h + 1) * hd]
        qz = jnp.concatenate([qh, zeros] if h % 2 == 0 else [zeros, qh], axis=0)
        lanes = slice((h // 2) * LANES, (h // 2 + 1) * LANES)
        gate = jnp.dot(kmean[:, lanes], qz, precision=HIGHEST, preferred_element_type=F32)
        beaten = jnp.zeros((n_blk, blk), F32)
        for m in range(n_blk):
            gm = gate[m:m + 1, :]
            wins = jnp.where(gm > gate, 1.0, jnp.where(gm == gate, jnp.where(blk_id > m, 1.0, 0.0), 0.0))
            beaten = beaten + jnp.where(m < qb, wins, 0.0)
        bias_ref[h] = jnp.where(blk_id < qb, jnp.where(beaten < MOBA_TOPK, 0.0, NEG_BIG), NEG_BIG)
        qz_b.append((qz * scale).astype(BF16))

    def attend(off, biases, carry):
        kb = kb_ref[pl.ds(off, blk), :]
        s = [jnp.dot(kb[:, (h // 2) * LANES:(h // 2 + 1) * LANES], qz_b[h], preferred_element_type=F32)
             + biases[h] for h in H]
        m_new = [jnp.maximum(carry[h][0], jnp.max(s[h], axis=0, keepdims=True)) for h in H]
        p = [jnp.exp(s[h] - m_new[h]).astype(BF16) for h in H]
        alpha = [jnp.exp(carry[h][0] - m_new[h]) for h in H]
        acc = [alpha[h] * carry[h][1]
               + jnp.dot(va_ref[h, :, pl.ds(off, blk)], p[h], preferred_element_type=F32) for h in H]
        return tuple((m_new[h], acc[h]) for h in H)

    def past_block(nb, carry):
        biases = [bias_ref[h, pl.ds(nb, 1), :] for h in H]
        return attend(pl.multiple_of(nb * blk, blk), biases, carry)

    init = tuple((jnp.full((1, blk), NEG_BIG, F32), jnp.zeros((aug, blk), F32)) for _ in H)
    carry = lax.fori_loop(0, qb, past_block, init)
    key_i = lax.broadcasted_iota(jnp.int32, (blk, blk), 0)
    qry_i = lax.broadcasted_iota(jnp.int32, (blk, blk), 1)
    causal_bias = jnp.where(key_i <= qry_i, 0.0, NEG_BIG)
    res = attend(pl.multiple_of(qb * blk, blk), [causal_bias] * heads, carry)
    for h in H:
        acc = res[h][1]
        o_ref[h * hd:(h + 1) * hd, :] = acc[:hd] / acc[hd:hd + 1]


def _moba_attention(qvt, k, bsz, t_len, *, pairs):
    d = k.shape[1]
    blk = MOBA_BLOCK
    n_blk = t_len // blk
    width = pairs * LANES
    n_grp = d // width
    heads = 2 * pairs
    return pl.pallas_call(
        functools.partial(_moba_body, n_blk=n_blk, pairs=pairs),
        grid=(bsz, n_grp, n_blk),
        in_specs=[pl.BlockSpec((width, blk), lambda b, p, q: (p, b * n_blk + q)),
                  pl.BlockSpec((width, t_len), lambda b, p, q: (n_grp + p, b)),
                  pl.BlockSpec((t_len, width), lambda b, p, q: (b, p))],
        out_specs=pl.BlockSpec((width, blk), lambda b, p, q: (p, b * n_blk + q)),
        out_shape=jax.ShapeDtypeStruct((d, bsz * t_len), F32),
        scratch_shapes=[pltpu.VMEM((n_blk, width), F32),
                        pltpu.VMEM((t_len, width), BF16),
                        pltpu.VMEM((heads, MOBA_HEAD_DIM + BF16_ROWS, t_len), BF16),
                        pltpu.VMEM((heads, n_blk, blk), F32)],
        compiler_params=_cparams("parallel", "parallel", "arbitrary"),
        name="moba_attention",
    )(qvt, qvt, k)


def _moba_mixer(x, bsz, t_len, w_qkv, w_o, ln_g, ln_b):
    d = x.shape[1]
    assert t_len % MOBA_BLOCK == 0
    w_k = w_qkv[:, d:2 * d].astype(BF16)
    w_qv_t = jnp.concatenate([w_qkv[:, :d], w_qkv[:, 2 * d:]], axis=1).T.astype(BF16)
    k = _matmul(x, w_k, tm=2048, tn=512)
    qvt = _matmul_nt(x, w_qv_t, tm=2048, tn=512)
    ot = _moba_attention(qvt, k, bsz, t_len, pairs=2)
    return _matmul_residual_ln(ot, w_o.astype(BF16), x, ln_g, ln_b, tm=1024, lhs_transposed=True)


def kernel(x, rwkv_mix, rwkv_w_rkv, rwkv_w0, rwkv_w1, rwkv_w2, rwkv_a0, rwkv_a1, rwkv_a2, rwkv_g1, rwkv_g2,
           rwkv_k_k, rwkv_k_a, rwkv_r_k, rwkv_gn_g, rwkv_gn_b, rwkv_w_o, ret_w_in, ret_gn_g, ret_gn_b, ret_w_o,
           moba_w_qkv, moba_w_o, ffn_w_in, ffn_conv_w, ffn_conv_b, ffn_w_out, ln1_g, ln1_b, ln2_g, ln2_b):
    bsz, t_len, d = x.shape
    h = x.reshape(bsz * t_len, d)
    for i in range(DEPTH):
        kind, j = i % N_MIXERS, i // N_MIXERS
        if kind == 0:
            h = _rwkv_mixer(h, bsz, t_len, rwkv_mix[j], rwkv_w_rkv[j], rwkv_w0[j], rwkv_w1[j], rwkv_w2[j],
                            rwkv_a0[j], rwkv_a1[j], rwkv_a2[j], rwkv_g1[j], rwkv_g2[j], rwkv_k_k[j],
                            rwkv_k_a[j], rwkv_r_k[j], rwkv_gn_g[j], rwkv_gn_b[j], rwkv_w_o[j],
                            ln1_g[i], ln1_b[i])
        elif kind == 1:
            h = _retention_mixer(h, bsz, t_len, ret_w_in[j], ret_gn_g[j], ret_gn_b[j], ret_w_o[j],
                                 ln1_g[i], ln1_b[i])
        else:
            h = _moba_mixer(h, bsz, t_len, moba_w_qkv[j], moba_w_o[j], ln1_g[i], ln1_b[i])
        h = _conv_ffn_ln(h, t_len, ffn_w_in[i].astype(BF16), ffn_conv_w[i], ffn_conv_b[i],
                         ffn_w_out[i].astype(BF16), ln2_g[i], ln2_b[i], tm=512, cols=256)
    return h.reshape(bsz, t_len, d)
```

```python
import functools
import math

import jax
import jax.numpy as jnp
from jax import lax
from jax.experimental import pallas as pl
from jax.experimental.pallas import tpu as pltpu

F32 = jnp.float32
BF16 = jnp.bfloat16
HIGHEST = lax.Precision.HIGHEST

DEPTH = 4
N_MIXERS = 3
RWKV_HEAD = 64
RWKV_GN_EPS = 64e-5
RET_HEADS = 4
RET_ROPE_BASE = 10000.0
RET_GN_EPS = 1e-5
MOBA_HEADS = 16
MOBA_HEAD_DIM = 64
MOBA_BLOCK = 256
MOBA_TOPK = 3
LN_EPS = 1e-5
DEEPNORM_ALPHA = (2 * DEPTH) ** 0.25

LANES = 128
SUBLANES = 8
BF16_ROWS = 16
VMEM_LIMIT_BYTES = 52 * 1024 * 1024

NEG_BIG = -1e30

_NT = (((1,), (1,)), ((), ()))
_TN = (((0,), (0,)), ((), ()))


def _cparams(*sem):
    return pltpu.CompilerParams(dimension_semantics=sem, vmem_limit_bytes=VMEM_LIMIT_BYTES)


def _sigmoid(x):
    return 1.0 / (1.0 + jnp.exp(-x))


def _layer_norm_rows(y, g, b):
    mu = jnp.mean(y, axis=-1, keepdims=True)
    d = y - mu
    var = jnp.mean(d * d, axis=-1, keepdims=True)
    return d * lax.rsqrt(var + LN_EPS) * g + b


def _mm_body(x_ref, w_ref, o_ref, xb_ref):
    @pl.when(pl.program_id(1) == 0)
    def _():
        xb_ref[...] = x_ref[...].astype(BF16)

    o_ref[...] = jnp.dot(xb_ref[...], w_ref[...], preferred_element_type=F32).astype(o_ref.dtype)


def _matmul(x, w, *, tm, tn, out_dtype=F32):
    m, k = x.shape
    n = w.shape[1]
    return pl.pallas_call(
        _mm_body,
        grid=(m // tm, n // tn),
        in_specs=[pl.BlockSpec((tm, k), lambda i, j: (i, 0)),
                  pl.BlockSpec((k, tn), lambda i, j: (0, j))],
        out_specs=pl.BlockSpec((tm, tn), lambda i, j: (i, j)),
        out_shape=jax.ShapeDtypeStruct((m, n), out_dtype),
        scratch_shapes=[pltpu.VMEM((tm, k), BF16)],
        compiler_params=_cparams("parallel", "arbitrary"),
        name="matmul",
    )(x, w)


def _mm_nt_body(x_ref, wt_ref, o_ref, xb_ref):
    @pl.when(pl.program_id(1) == 0)
    def _():
        xb_ref[...] = x_ref[...].astype(BF16)

    o_ref[...] = lax.dot_general(wt_ref[...], xb_ref[...], _NT, preferred_element_type=F32)


def _matmul_nt(x, wt, *, tm, tn):
    m, k = x.shape
    n = wt.shape[0]
    return pl.pallas_call(
        _mm_nt_body,
        grid=(m // tm, n // tn),
        in_specs=[pl.BlockSpec((tm, k), lambda i, j: (i, 0)),
                  pl.BlockSpec((tn, k), lambda i, j: (j, 0))],
        out_specs=pl.BlockSpec((tn, tm), lambda i, j: (j, i)),
        out_shape=jax.ShapeDtypeStruct((n, m), F32),
        scratch_shapes=[pltpu.VMEM((tm, k), BF16)],
        compiler_params=_cparams("parallel", "arbitrary"),
        name="matmul_nt",
    )(x, wt)


def _mm_res_ln_body(o_ref, w_ref, x_ref, g_ref, b_ref, out_ref, *, lhs_transposed):
    lhs = o_ref[...].astype(BF16)
    if lhs_transposed:
        acc = lax.dot_general(lhs, w_ref[...], _TN, preferred_element_type=F32)
    else:
        acc = jnp.dot(lhs, w_ref[...], preferred_element_type=F32)
    y = DEEPNORM_ALPHA * x_ref[...] + acc
    out_ref[...] = _layer_norm_rows(y, g_ref[...], b_ref[...])


def _matmul_residual_ln(o, w, x, g, b, *, tm, lhs_transposed=False):
    m, d = x.shape
    k = w.shape[0]
    if lhs_transposed:
        o_spec = pl.BlockSpec((k, tm), lambda i: (0, i))
    else:
        o_spec = pl.BlockSpec((tm, k), lambda i: (i, 0))
    return pl.pallas_call(
        functools.partial(_mm_res_ln_body, lhs_transposed=lhs_transposed),
        grid=(m // tm,),
        in_specs=[o_spec,
                  pl.BlockSpec((k, d), lambda i: (0, 0)),
                  pl.BlockSpec((tm, d), lambda i: (i, 0)),
                  pl.BlockSpec((1, d), lambda i: (0, 0)),
                  pl.BlockSpec((1, d), lambda i: (0, 0))],
        out_specs=pl.BlockSpec((tm, d), lambda i: (i, 0)),
        out_shape=jax.ShapeDtypeStruct((m, d), F32),
        compiler_params=_cparams("parallel"),
        name="out_proj_residual_ln",
    )(o, w, x, g.reshape(1, d), b.reshape(1, d))


def _ffn_body(x_ref, xh_ref, w_in_ref, cw_ref, cb_ref, wo_ref, g_ref, b_ref,
              out_ref, xb_ref, xp_ref, hu_ref, hg_ref, act_ref, *, tm, tiles_per_seq, ff, cols):
    i = pl.program_id(0)
    d = x_ref.shape[1]
    halo = BF16_ROWS
    grp = SUBLANES
    n_grp = tm // grp
    first = (i % tiles_per_seq) == 0
    xb_ref[0:halo, :] = jnp.where(first, 0.0, xh_ref[...]).astype(BF16)
    xp = x_ref[...].reshape(grp, n_grp, d).swapaxes(0, 1).reshape(tm, d)
    xp_ref[...] = xp
    xb_ref[halo:, :] = xp.astype(BF16)
    sub = lax.broadcasted_iota(jnp.int32, (grp, cols), 0)

    def hidden(h_ref, c):
        h = jnp.dot(xb_ref[...], w_in_ref[:, c], preferred_element_type=F32)
        h_ref[halo:, :] = h[halo:]
        for back in (1, 2):
            last = h[halo + tm - back * grp:halo + tm - (back - 1) * grp]
            prev = jnp.where(sub == 0, h[halo - back:halo - back + 1], pltpu.roll(last, 1, 0))
            h_ref[halo - back * grp:halo - (back - 1) * grp, :] = prev

    def conv(h_ref, c):
        return (cw_ref[0:1, c] * h_ref[pl.ds(halo - 2 * grp, tm), :]
                + cw_ref[1:2, c] * h_ref[pl.ds(halo - grp, tm), :]
                + cw_ref[2:3, c] * h_ref[pl.ds(halo, tm), :]
                + cb_ref[:, c])

    for j in range(ff // cols):
        cu = slice(j * cols, (j + 1) * cols)
        cg = slice(ff + j * cols, ff + (j + 1) * cols)
        hu = hu_ref.at[j % 2]
        hg = hg_ref.at[j % 2]
        hidden(hu, cu)
        hidden(hg, cg)
        u = conv(hu, cu)
        gt = conv(hg, cg)
        act_ref[:, cu] = ((gt * _sigmoid(gt)) * u).astype(BF16)

    y = DEEPNORM_ALPHA * xp_ref[...] + jnp.dot(act_ref[...], wo_ref[...], preferred_element_type=F32)
    yn = _layer_norm_rows(y, g_ref[...], b_ref[...])
    out_ref[...] = yn.reshape(n_grp, grp, d).swapaxes(0, 1).reshape(tm, d)


def _conv_ffn_ln(x, seq_len, layer, w_in, conv_w, conv_b, w_out, g, b, *, tm, cols):
    m, d = x.shape
    ff = w_out.shape[1]
    halo = BF16_ROWS
    body = functools.partial(_ffn_body, tm=tm, tiles_per_seq=seq_len // tm, ff=ff, cols=cols)
    resident = lambda shape: pl.BlockSpec((None,) + shape, lambda i: (layer, 0, 0),
                                          pipeline_mode=pl.Buffered(1))
    n_layers = w_in.shape[0]
    conv_b = conv_b.reshape(n_layers, 1, 2 * ff)
    g = g.reshape(n_layers, 1, d)
    b = b.reshape(n_layers, 1, d)
    return pl.pallas_call(
        body,
        grid=(m // tm,),
        in_specs=[pl.BlockSpec((tm, d), lambda i: (i, 0)),
                  pl.BlockSpec((halo, d), lambda i: (jnp.maximum(i * (tm // halo) - 1, 0), 0)),
                  resident((d, 2 * ff)),
                  resident((3, 2 * ff)),
                  resident((1, 2 * ff)),
                  resident((ff, d)),
                  resident((1, d)),
                  resident((1, d))],
        out_specs=pl.BlockSpec((tm, d), lambda i: (i, 0)),
        out_shape=jax.ShapeDtypeStruct((m, d), F32),
        scratch_shapes=[pltpu.VMEM((tm + halo, d), BF16),
                        pltpu.VMEM((tm, d), F32),
                        pltpu.VMEM((2, tm + halo, cols), F32),
                        pltpu.VMEM((2, tm + halo, cols), F32),
                        pltpu.VMEM((tm, ff), BF16)],
        compiler_params=_cparams("parallel"),
        name="conv_ffn_ln",
    )(x, x, w_in, conv_w, conv_b, w_out, g, b)


def _head_pair_sum(v, ones_bd):
    hi = v.astype(BF16)
    lo = (v - hi.astype(F32)).astype(BF16)
    return (jnp.dot(hi, ones_bd, preferred_element_type=F32)
            + jnp.dot(lo, ones_bd, preferred_element_type=F32))


def _rwkv_proj_body(x_ref, xp_ref, mix_ref, wrkv_ref, w1_ref, w2_ref, a1_ref, a2_ref, g1_ref, g2_ref,
                    w0_ref, a0_ref, kk_ref, ka_ref,
                    r_out, ld_out, k_out, v_out, kk_out, b_out, gate_out, *, tm, tiles_per_seq):
    i = pl.program_id(0)
    d = x_ref.shape[1]
    x = x_ref[...]
    first = (i % tiles_per_seq) == 0
    prev = jnp.where(first, 0.0, xp_ref[SUBLANES - 1:SUBLANES, :])
    row = lax.broadcasted_iota(jnp.int32, (tm, d), 0)
    xs = jnp.where(row == 0, prev, pltpu.roll(x, 1, 0))
    xx = xs - x

    def mixed(j):
        return (x + xx * mix_ref[j:j + 1, :]).astype(BF16)

    def mm(a, w):
        return jnp.dot(a, w, preferred_element_type=F32)

    r = mm(mixed(0), wrkv_ref[0])
    k = mm(mixed(2), wrkv_ref[1])
    v = mm(mixed(3), wrkv_ref[2])
    lw = w0_ref[...] + mm(jnp.tanh(mm(mixed(1), w1_ref[...])).astype(BF16), w2_ref[...])
    softplus_neg = jnp.maximum(-lw, 0.0) + jnp.log(1.0 + jnp.exp(-jnp.abs(lw)))
    log_decay = -jnp.exp(-softplus_neg - 0.5)
    a = _sigmoid(a0_ref[...] + mm(mm(mixed(4), a1_ref[...]).astype(BF16), a2_ref[...]))
    gate = mm(_sigmoid(mm(mixed(5), g1_ref[...])).astype(BF16), g2_ref[...])

    kk = k * kk_ref[...]
    rr = lax.broadcasted_iota(jnp.int32, (LANES, LANES), 0) // RWKV_HEAD
    cc = lax.broadcasted_iota(jnp.int32, (LANES, LANES), 1) // RWKV_HEAD
    ones_bd = jnp.where(rr == cc, 1.0, 0.0).astype(BF16)
    sq = kk * kk
    ss = jnp.concatenate([_head_pair_sum(sq[:, j * LANES:(j + 1) * LANES], ones_bd)
                          for j in range(d // LANES)], axis=1)
    kk = kk / jnp.maximum(jnp.sqrt(ss), 1e-12)

    r_out[...] = r
    ld_out[...] = log_decay
    k_out[...] = k * (1.0 + (a - 1.0) * ka_ref[...])
    v_out[...] = v
    kk_out[...] = kk
    b_out[...] = kk * a
    gate_out[...] = gate


def _rwkv_proj(x, seq_len, mix, w_rkv, w0, w1, w2, a0, a1, a2, g1, g2, k_k, k_a, *, tm):
    m, d = x.shape
    full = lambda arr: pl.BlockSpec(arr.shape, lambda i: (0,) * arr.ndim, pipeline_mode=pl.Buffered(1))
    vec = lambda a: a.reshape(1, d)
    args = (mix, w_rkv.astype(BF16), w1.astype(BF16), w2.astype(BF16), a1.astype(BF16), a2.astype(BF16),
            g1.astype(BF16), g2.astype(BF16), vec(w0), vec(a0), vec(k_k), vec(k_a))
    row_spec = pl.BlockSpec((tm, d), lambda i: (i, 0))
    return pl.pallas_call(
        functools.partial(_rwkv_proj_body, tm=tm, tiles_per_seq=seq_len // tm),
        grid=(m // tm,),
        in_specs=[row_spec,
                  pl.BlockSpec((SUBLANES, d), lambda i: (jnp.maximum(i * (tm // SUBLANES) - 1, 0), 0))]
                 + [full(a) for a in args],
        out_specs=[row_spec] * 7,
        out_shape=[jax.ShapeDtypeStruct((m, d), F32)] * 7,
        compiler_params=_cparams("parallel"),
        name="rwkv_proj",
    )(x, x, *args)


def _bf16_dot(a, b, dims):
    return lax.dot_general(a.astype(BF16), b.astype(BF16), dims, preferred_element_type=F32)


def _head_sums(tiles, ones_bd):
    rows = tiles[0].shape[0]
    parts = []
    for t in tiles:
        hi = t.astype(BF16)
        parts += [hi, (t - hi.astype(F32)).astype(BF16)]
    res = jnp.dot(jnp.concatenate(parts, axis=0), ones_bd, preferred_element_type=F32)
    return [res[2 * i * rows:(2 * i + 1) * rows] + res[(2 * i + 1) * rows:(2 * i + 2) * rows]
            for i in range(len(tiles))]


def _rwkv_chunk_body(r_ref, ld_ref, k_ref, v_ref, kk_ref, b_ref, gate_ref, rk_ref, gg_ref, gb_ref,
                     o_ref, st_ref, *, chunk, pairs, subs):
    c = pl.program_id(2)

    @pl.when(c == 0)
    def _():
        st_ref[...] = jnp.zeros_like(st_ref)

    n = RWKV_HEAD
    row = lax.broadcasted_iota(jnp.int32, (chunk, LANES), 0)
    lane = lax.broadcasted_iota(jnp.int32, (chunk, LANES), 1)
    col = lane % n
    eye2 = jnp.where(row == col, 1.0, 0.0)
    head1 = lane >= n
    row2 = lax.broadcasted_iota(jnp.int32, (2 * chunk, LANES), 0)
    col2 = lax.broadcasted_iota(jnp.int32, (2 * chunk, LANES), 1) % n
    tri2 = jnp.where(row2 < chunk, row2 - 1, row2 - chunk) >= col2
    rr = lax.broadcasted_iota(jnp.int32, (LANES, LANES), 0) // n
    cc = lax.broadcasted_iota(jnp.int32, (LANES, LANES), 1) // n
    bd = rr == cc
    ones_bd = jnp.where(bd, 1.0, 0.0).astype(BF16)
    nn = (((1,), (0,)), ((), ()))

    def head_stack(t):
        return jnp.concatenate([jnp.where(head1, 0.0, t), jnp.where(head1, t, 0.0)], axis=0)

    def block_diag(t):
        return jnp.where(bd, jnp.concatenate([t, t], axis=0), 0.0)

    units = [(cc, p) for cc in range(subs) for p in range(pairs)]
    U = range(len(units))
    rows = [slice(cc * chunk, (cc + 1) * chunk) for cc, _ in units]
    sls = [slice(p * LANES, (p + 1) * LANES) for _, p in units]
    r = [r_ref[0, rows[i], sls[i]] for i in U]
    ld = [ld_ref[0, rows[i], sls[i]] for i in U]
    k = [k_ref[0, rows[i], sls[i]] for i in U]
    v = [v_ref[0, rows[i], sls[i]] for i in U]
    bv = [b_ref[0, rows[i], sls[i]] for i in U]

    cum = list(ld)
    shift = 1
    while shift < chunk:
        cum = [cu + jnp.where(row >= shift, pltpu.roll(cu, shift, 0), 0.0) for cu in cum]
        shift *= 2
    cum_last = [cu[chunk - 1:chunk, :] for cu in cum]
    e_inv = [jnp.exp(-cu) for cu in cum]
    r_hat = [r[i] * jnp.exp(cum[i]) for i in U]
    a_hat = [-(kk_ref[0, rows[i], sls[i]] * jnp.exp(cum[i] - ld[i])) for i in U]
    lhs = [jnp.concatenate([a_hat[i], r_hat[i]], axis=0) for i in U]
    sbk = [_bf16_dot(lhs[i], jnp.concatenate([head_stack(bv[i] * e_inv[i]), head_stack(k[i] * e_inv[i])], axis=0),
                     _NT) for i in U]
    sb = [jnp.where(tri2, t[:, :LANES], 0.0) for t in sbk]
    sk = [jnp.where(tri2, t[:, LANES:], 0.0) for t in sbk]
    a_ab = [t[:chunk] for t in sb]
    a_rb = [t[chunk:] for t in sb]

    xp = [_bf16_dot(t, block_diag(t), nn) for t in a_ab]
    tinv = [eye2 + t for t in a_ab]
    power = 2
    while 2 * power < chunk:
        both = [_bf16_dot(jnp.concatenate([xp[i], tinv[i]], axis=0), block_diag(xp[i]), nn) for i in U]
        xp = [t[:chunk] for t in both]
        tinv = [tinv[i] + both[i][chunk:] for i in U]
        power *= 2
    tinv = [tinv[i] + _bf16_dot(tinv[i], block_diag(xp[i]), nn) for i in U]

    state = [st_ref[p] for p in range(pairs)]
    y = [None] * len(units)
    for cc in range(subs):
        ids = [cc * pairs + p for p in range(pairs)]
        zy = [_bf16_dot(jnp.concatenate([lhs[i], sk[i]], axis=1),
                        jnp.concatenate([state[p].T, head_stack(v[i])], axis=0), nn)
              for p, i in enumerate(ids)]
        u = [_bf16_dot(tinv[i], head_stack(zy[p][:chunk]), nn) for p, i in enumerate(ids)]
        for p, i in enumerate(ids):
            y[i] = zy[p][chunk:] + _bf16_dot(a_rb[i], head_stack(u[p]), nn)
            e_last = jnp.exp(cum_last[i] - cum[i])
            s_new = state[p] * jnp.exp(cum_last[i]) + _bf16_dot(
                jnp.concatenate([u[p], v[i]], axis=0),
                jnp.concatenate([bv[i] * e_last, k[i] * e_last], axis=0), _TN)
            state[p] = jnp.where(bd, s_new, 0.0)
    for p in range(pairs):
        st_ref[p] = state[p]

    sums = _head_sums(y + [r[i] * k[i] * rk_ref[:, sls[i]] for i in U], ones_bd)
    dlt = [y[i] - sums[i] * (1.0 / n) for i in U]
    var = _head_sums([t * t for t in dlt], ones_bd)
    for i in U:
        yn = dlt[i] * lax.rsqrt(var[i] * (1.0 / n) + RWKV_GN_EPS) * gg_ref[:, sls[i]] + gb_ref[:, sls[i]]
        bonus = sums[len(units) + i] * v[i]
        o_ref[0, rows[i], sls[i]] = ((yn + bonus) * gate_ref[0, rows[i], sls[i]]).astype(o_ref.dtype)


def _rwkv_chunk(r, ld, k, v, kk, bvec, gate, r_k, gn_g, gn_b, *, pairs, subs):
    bsz, t_len, d = r.shape
    chunk = RWKV_HEAD
    width = pairs * LANES
    seq_spec = pl.BlockSpec((1, subs * chunk, width), lambda b, g, c: (b, c, g))
    vec_spec = pl.BlockSpec((1, width), lambda b, g, c: (0, g))
    return pl.pallas_call(
        functools.partial(_rwkv_chunk_body, chunk=chunk, pairs=pairs, subs=subs),
        grid=(bsz, d // width, t_len // (subs * chunk)),
        in_specs=[seq_spec] * 7 + [vec_spec] * 3,
        out_specs=seq_spec,
        out_shape=jax.ShapeDtypeStruct((bsz, t_len, d), BF16),
        scratch_shapes=[pltpu.VMEM((pairs, LANES, LANES), F32)],
        compiler_params=_cparams("parallel", "parallel", "arbitrary"),
        name="rwkv_chunk",
    )(r, ld, k, v, kk, bvec, gate, r_k.reshape(1, d), gn_g.reshape(1, d), gn_b.reshape(1, d))


def _rwkv_mixer(x, bsz, t_len, mix, w_rkv, w0, w1, w2, a0, a1, a2, g1, g2, k_k, k_a, r_k, gn_g, gn_b, w_o,
                ln_g, ln_b):
    d = x.shape[1]
    outs = _rwkv_proj(x, t_len, mix, w_rkv, w0, w1, w2, a0, a1, a2, g1, g2, k_k, k_a, tm=512)
    seq = [o.reshape(bsz, t_len, d) for o in outs]
    o = _rwkv_chunk(*seq, r_k, gn_g, gn_b, pairs=8, subs=2)
    return _matmul_residual_ln(o.reshape(bsz * t_len, d), w_o.astype(BF16), x, ln_g, ln_b, tm=1024)


def _ret_chunk_body(q_ref, k_ref, v_ref, gate_ref, cos_ref, sin_ref, g_ref, b_ref, o_ref, st_ref, *, chunk):
    c = pl.program_id(1)

    @pl.when(c == 0)
    def _():
        st_ref[...] = jnp.zeros_like(st_ref)

    n_heads, dk, dv = st_ref.shape
    half = dk // 2
    cos = cos_ref[...]
    sin = sin_ref[...]

    def rotate(t):
        te, to = t[:, :half], t[:, half:]
        return jnp.concatenate([te * cos - to * sin, to * cos + te * sin], axis=1)

    ri = lax.broadcasted_iota(jnp.int32, (chunk, chunk), 0)
    ci = lax.broadcasted_iota(jnp.int32, (chunk, chunk), 1)
    rel = (ri - ci).astype(F32)
    pos = lax.broadcasted_iota(jnp.int32, (chunk, dk), 0).astype(F32)

    for h in range(n_heads):
        log_gamma = math.log(1.0 - 2.0 ** (-5.0 - h))
        inner = jnp.where(rel >= 0, jnp.exp(log_gamma * jnp.maximum(rel, 0.0)), 0.0)
        cross = jnp.exp(log_gamma * (pos + 1.0))
        sdecay = jnp.exp(log_gamma * (chunk - 1.0 - pos))
        chunk_decay = jnp.exp(log_gamma * chunk)

        q = rotate(q_ref[0, :, h * dk:(h + 1) * dk])
        k = rotate(k_ref[0, :, h * dk:(h + 1) * dk]) * (dk ** -0.5)
        v = v_ref[0, :, h * dv:(h + 1) * dv].astype(BF16)
        s = lax.dot_general(q.astype(BF16), k.astype(BF16), _NT, preferred_element_type=F32) * inner
        st = st_ref[h]
        o = (jnp.dot(s.astype(BF16), v, preferred_element_type=F32)
             + jnp.dot((q * cross).astype(BF16), st.astype(BF16), preferred_element_type=F32))
        st_ref[h] = st * chunk_decay + lax.dot_general((k * sdecay).astype(BF16), v, _TN,
                                                       preferred_element_type=F32)

        mu = jnp.mean(o, axis=-1, keepdims=True)
        dlt = o - mu
        var = jnp.mean(dlt * dlt, axis=-1, keepdims=True)
        vs = slice(h * dv, (h + 1) * dv)
        on = dlt * lax.rsqrt(var + RET_GN_EPS) * g_ref[:, vs] + b_ref[:, vs]
        gt = gate_ref[0, :, vs]
        o_ref[0, :, vs] = (gt * _sigmoid(gt) * on).astype(o_ref.dtype)


def _ret_chunk(proj, cos, sin, gn_g, gn_b, *, chunk):
    bsz, t_len, six_d = proj.shape
    d = six_d // 6
    h = RET_HEADS
    dk, dv = d // h, 2 * d // h
    return pl.pallas_call(
        functools.partial(_ret_chunk_body, chunk=chunk),
        grid=(bsz, t_len // chunk),
        in_specs=[pl.BlockSpec((1, chunk, d), lambda b, c: (b, c, 0)),
                  pl.BlockSpec((1, chunk, d), lambda b, c: (b, c, 1)),
                  pl.BlockSpec((1, chunk, 2 * d), lambda b, c: (b, c, 1)),
                  pl.BlockSpec((1, chunk, 2 * d), lambda b, c: (b, c, 2)),
                  pl.BlockSpec((chunk, dk // 2), lambda b, c: (c, 0)),
                  pl.BlockSpec((chunk, dk // 2), lambda b, c: (c, 0)),
                  pl.BlockSpec((1, 2 * d), lambda b, c: (0, 0)),
                  pl.BlockSpec((1, 2 * d), lambda b, c: (0, 0))],
        out_specs=pl.BlockSpec((1, chunk, 2 * d), lambda b, c: (b, c, 0)),
        out_shape=jax.ShapeDtypeStruct((bsz, t_len, 2 * d), BF16),
        scratch_shapes=[pltpu.VMEM((h, dk, dv), F32)],
        compiler_params=_cparams("parallel", "arbitrary"),
        name="retention_chunk",
    )(proj, proj, proj, proj, cos, sin, gn_g.reshape(1, 2 * d), gn_b.reshape(1, 2 * d))


def _retention_mixer(x, bsz, t_len, w_in, gn_g, gn_b, w_o, ln_g, ln_b):
    d = x.shape[1]
    h = RET_HEADS
    dk = d // h
    w_qk = w_in[:, :2 * d].reshape(d, 2 * h, dk // 2, 2).transpose(0, 1, 3, 2).reshape(d, 2 * d)
    w_perm = jnp.concatenate([w_qk, w_in[:, 2 * d:]], axis=1).astype(BF16)
    proj = _matmul(x, w_perm, tm=2048, tn=512)
    inv = 1.0 / (RET_ROPE_BASE ** jnp.linspace(0.0, 1.0, dk // 2, dtype=F32))
    ang = jnp.arange(t_len, dtype=F32)[:, None] * inv[None, :]
    o = _ret_chunk(proj.reshape(bsz, t_len, 6 * d), jnp.cos(ang), jnp.sin(ang), gn_g, gn_b, chunk=256)
    return _matmul_residual_ln(o.reshape(bsz * t_len, 2 * d), w_o.astype(BF16), x, ln_g, ln_b, tm=1024)


def _moba_body(qt_ref, vt_ref, k_ref, o_ref, kmean_ref, kb_ref, va_ref, bias_ref, *, n_blk, pairs):
    qb = pl.program_id(2)
    blk = MOBA_BLOCK
    hd = MOBA_HEAD_DIM
    heads = 2 * pairs
    aug = hd + BF16_ROWS
    t_len = n_blk * blk
    H = range(heads)

    @pl.when(qb == 0)
    def _():
        kf = k_ref[...]
        kmean_ref[...] = jnp.mean(kf.reshape(n_blk, blk, pairs * LANES), axis=1)
        kb_ref[...] = kf.astype(BF16)
        ones = jnp.ones((BF16_ROWS, t_len), BF16)
        for h in H:
            va_ref[h, 0:hd, :] = vt_ref[h * hd:(h + 1) * hd, :].astype(BF16)
            va_ref[h, hd:aug, :] = ones

    kmean = kmean_ref[...]
    qt = qt_ref[...]
    zeros = jnp.zeros((hd, blk), F32)
    blk_id = lax.broadcasted_iota(jnp.int32, (n_blk, blk), 0)
    scale = hd ** -0.5

    qz_b = []
    for h in H:
        qh = qt[h * hd:(h + 1) * hd]
        qz = jnp.concatenate([qh, zeros] if h % 2 == 0 else [zeros, qh], axis=0)
        lanes = slice((h // 2) * LANES, (h // 2 + 1) * LANES)
        gate = jnp.dot(kmean[:, lanes], qz, precision=HIGHEST, preferred_element_type=F32)
        beaten = jnp.zeros((n_blk, blk), F32)
        for m in range(n_blk):
            gm = gate[m:m + 1, :]
            wins = jnp.where(gm > gate, 1.0, jnp.where(gm == gate, jnp.where(blk_id > m, 1.0, 0.0), 0.0))
            beaten = beaten + jnp.where(m < qb, wins, 0.0)
        bias_ref[h] = jnp.where(blk_id < qb, jnp.where(beaten < MOBA_TOPK, 0.0, NEG_BIG), NEG_BIG)
        qz_b.append((qz * scale).astype(BF16))

    def scores(off):
        kb = kb_ref[pl.ds(off, blk), :]
        return tuple(jnp.dot(kb[:, (h // 2) * LANES:(h // 2 + 1) * LANES], qz_b[h],
                             preferred_element_type=F32) for h in H)

    def attend(off, s, carry):
        m_new = [jnp.maximum(carry[h][0], jnp.max(s[h], axis=0, keepdims=True)) for h in H]
        p = [jnp.exp(s[h] - m_new[h]).astype(BF16) for h in H]
        alpha = [jnp.exp(carry[h][0] - m_new[h]) for h in H]
        acc = [alpha[h] * carry[h][1]
               + jnp.dot(va_ref[h, :, pl.ds(off, blk)], p[h], preferred_element_type=F32) for h in H]
        return tuple((m_new[h], acc[h]) for h in H)

    def past_block(nb, carry):
        off = pl.multiple_of(nb * blk, blk)
        s = scores(off)
        s = [s[h] + bias_ref[h, pl.ds(nb, 1), :] for h in H]
        return attend(off, s, carry)

    init = tuple((jnp.full((1, blk), NEG_BIG, F32), jnp.zeros((aug, blk), F32)) for _ in H)
    carry = lax.fori_loop(0, qb, past_block, init)
    key_i = lax.broadcasted_iota(jnp.int32, (blk, blk), 0)
    qry_i = lax.broadcasted_iota(jnp.int32, (blk, blk), 1)
    causal_bias = jnp.where(key_i <= qry_i, 0.0, NEG_BIG)
    own = pl.multiple_of(qb * blk, blk)
    s_own = scores(own)
    res = attend(own, [s_own[h] + causal_bias for h in H], carry)
    for h in H:
        acc = res[h][1]
        o_ref[h * hd:(h + 1) * hd, :] = acc[:hd] / acc[hd:hd + 1]


def _moba_attention(qvt, k, bsz, t_len, *, pairs):
    d = k.shape[1]
    blk = MOBA_BLOCK
    n_blk = t_len // blk
    width = pairs * LANES
    n_grp = d // width
    heads = 2 * pairs
    return pl.pallas_call(
        functools.partial(_moba_body, n_blk=n_blk, pairs=pairs),
        grid=(bsz, n_grp, n_blk),
        in_specs=[pl.BlockSpec((width, blk), lambda b, p, q: (p, b * n_blk + q)),
                  pl.BlockSpec((width, t_len), lambda b, p, q: (n_grp + p, b)),
                  pl.BlockSpec((t_len, width), lambda b, p, q: (b, p))],
        out_specs=pl.BlockSpec((width, blk), lambda b, p, q: (p, b * n_blk + q)),
        out_shape=jax.ShapeDtypeStruct((d, bsz * t_len), F32),
        scratch_shapes=[pltpu.VMEM((n_blk, width), F32),
                        pltpu.VMEM((t_len, width), BF16),
                        pltpu.VMEM((heads, MOBA_HEAD_DIM + BF16_ROWS, t_len), BF16),
                        pltpu.VMEM((heads, n_blk, blk), F32)],
        compiler_params=_cparams("parallel", "parallel", "arbitrary"),
        name="moba_attention",
    )(qvt, qvt, k)


def _moba_mixer(x, bsz, t_len, w_qkv, w_o, ln_g, ln_b):
    d = x.shape[1]
    assert t_len % MOBA_BLOCK == 0
    w_k = w_qkv[:, d:2 * d].astype(BF16)
    w_qv_t = jnp.concatenate([w_qkv[:, :d], w_qkv[:, 2 * d:]], axis=1).T.astype(BF16)
    k = _matmul(x, w_k, tm=2048, tn=512)
    qvt = _matmul_nt(x, w_qv_t, tm=2048, tn=512)
    ot = _moba_attention(qvt, k, bsz, t_len, pairs=2)
    return _matmul_residual_ln(ot, w_o.astype(BF16), x, ln_g, ln_b, tm=1024, lhs_transposed=True)


def kernel(x, rwkv_mix, rwkv_w_rkv, rwkv_w0, rwkv_w1, rwkv_w2, rwkv_a0, rwkv_a1, rwkv_a2, rwkv_g1, rwkv_g2,
           rwkv_k_k, rwkv_k_a, rwkv_r_k, rwkv_gn_g, rwkv_gn_b, rwkv_w_o, ret_w_in, ret_gn_g, ret_gn_b, ret_w_o,
           moba_w_qkv, moba_w_o, ffn_w_in, ffn_conv_w, ffn_conv_b, ffn_w_out, ln1_g, ln1_b, ln2_g, ln2_b):
    bsz, t_len, d = x.shape
    h = x.reshape(bsz * t_len, d)
    ffn_w_in_b = ffn_w_in.astype(BF16)
    ffn_w_out_b = ffn_w_out.astype(BF16)
    for i in range(DEPTH):
        kind, j = i % N_MIXERS, i // N_MIXERS
        if kind == 0:
            h = _rwkv_mixer(h, bsz, t_len, rwkv_mix[j], rwkv_w_rkv[j], rwkv_w0[j], rwkv_w1[j], rwkv_w2[j],
                            rwkv_a0[j], rwkv_a1[j], rwkv_a2[j], rwkv_g1[j], rwkv_g2[j], rwkv_k_k[j],
                            rwkv_k_a[j], rwkv_r_k[j], rwkv_gn_g[j], rwkv_gn_b[j], rwkv_w_o[j],
                            ln1_g[i], ln1_b[i])
        elif kind == 1:
            h = _retention_mixer(h, bsz, t_len, ret_w_in[j], ret_gn_g[j], ret_gn_b[j], ret_w_o[j],
                                 ln1_g[i], ln1_b[i])
        else:
            h = _moba_mixer(h, bsz, t_len, moba_w_qkv[j], moba_w_o[j], ln1_g[i], ln1_b[i])
        h = _conv_ffn_ln(h, t_len, i, ffn_w_in_b, ffn_conv_w, ffn_conv_b, ffn_w_out_b, ln2_g, ln2_b,
                         tm=512, cols=256)
    return h.reshape(bsz, t_len, d)
```

```python
import functools
import math

import jax
import jax.numpy as jnp
from jax import lax
from jax.experimental import pallas as pl
from jax.experimental.pallas import tpu as pltpu

F32 = jnp.float32
BF16 = jnp.bfloat16
HIGHEST = lax.Precision.HIGHEST

DEPTH = 4
N_MIXERS = 3
RWKV_HEAD = 64
RWKV_GN_EPS = 64e-5
RET_HEADS = 4
RET_ROPE_BASE = 10000.0
RET_GN_EPS = 1e-5
MOBA_HEADS = 16
MOBA_HEAD_DIM = 64
MOBA_BLOCK = 256
MOBA_TOPK = 3
LN_EPS = 1e-5
DEEPNORM_ALPHA = (2 * DEPTH) ** 0.25

LANES = 128
SUBLANES = 8
BF16_ROWS = 16
VMEM_LIMIT_BYTES = 52 * 1024 * 1024

NEG_BIG = -1e30

_NT = (((1,), (1,)), ((), ()))
_TN = (((0,), (0,)), ((), ()))


def _cparams(*sem):
    return pltpu.CompilerParams(dimension_semantics=sem, vmem_limit_bytes=VMEM_LIMIT_BYTES)


def _sigmoid(x):
    return 1.0 / (1.0 + jnp.exp(-x))


def _layer_norm_rows(y, g, b):
    mu = jnp.mean(y, axis=-1, keepdims=True)
    d = y - mu
    var = jnp.mean(d * d, axis=-1, keepdims=True)
    return d * lax.rsqrt(var + LN_EPS) * g + b


def _mm_body(x_ref, w_ref, o_ref, xb_ref):
    @pl.when(pl.program_id(1) == 0)
    def _():
        xb_ref[...] = x_ref[...].astype(BF16)

    o_ref[...] = jnp.dot(xb_ref[...], w_ref[...], preferred_element_type=F32).astype(o_ref.dtype)


def _matmul(x, w, *, tm, tn, out_dtype=F32):
    m, k = x.shape
    n = w.shape[1]
    return pl.pallas_call(
        _mm_body,
        grid=(m // tm, n // tn),
        in_specs=[pl.BlockSpec((tm, k), lambda i, j: (i, 0)),
                  pl.BlockSpec((k, tn), lambda i, j: (0, j))],
        out_specs=pl.BlockSpec((tm, tn), lambda i, j: (i, j)),
        out_shape=jax.ShapeDtypeStruct((m, n), out_dtype),
        scratch_shapes=[pltpu.VMEM((tm, k), BF16)],
        compiler_params=_cparams("parallel", "arbitrary"),
        name="matmul",
    )(x, w)


def _mm_nt_body(x_ref, wt_ref, o_ref, xb_ref):
    @pl.when(pl.program_id(1) == 0)
    def _():
        xb_ref[...] = x_ref[...].astype(BF16)

    o_ref[...] = lax.dot_general(wt_ref[...], xb_ref[...], _NT, preferred_element_type=F32)


def _matmul_nt(x, wt, *, tm, tn):
    m, k = x.shape
    n = wt.shape[0]
    return pl.pallas_call(
        _mm_nt_body,
        grid=(m // tm, n // tn),
        in_specs=[pl.BlockSpec((tm, k), lambda i, j: (i, 0)),
                  pl.BlockSpec((tn, k), lambda i, j: (j, 0))],
        out_specs=pl.BlockSpec((tn, tm), lambda i, j: (j, i)),
        out_shape=jax.ShapeDtypeStruct((n, m), F32),
        scratch_shapes=[pltpu.VMEM((tm, k), BF16)],
        compiler_params=_cparams("parallel", "arbitrary"),
        name="matmul_nt",
    )(x, wt)


def _mm_res_ln_body(o_ref, w_ref, x_ref, g_ref, b_ref, out_ref, *, lhs_transposed):
    lhs = o_ref[...].astype(BF16)
    if lhs_transposed:
        acc = lax.dot_general(lhs, w_ref[...], _TN, preferred_element_type=F32)
    else:
        acc = jnp.dot(lhs, w_ref[...], preferred_element_type=F32)
    y = DEEPNORM_ALPHA * x_ref[...] + acc
    out_ref[...] = _layer_norm_rows(y, g_ref[...], b_ref[...])


def _matmul_residual_ln(o, w, x, g, b, *, tm, lhs_transposed=False):
    m, d = x.shape
    k = w.shape[0]
    if lhs_transposed:
        o_spec = pl.BlockSpec((k, tm), lambda i: (0, i))
    else:
        o_spec = pl.BlockSpec((tm, k), lambda i: (i, 0))
    return pl.pallas_call(
        functools.partial(_mm_res_ln_body, lhs_transposed=lhs_transposed),
        grid=(m // tm,),
        in_specs=[o_spec,
                  pl.BlockSpec((k, d), lambda i: (0, 0)),
                  pl.BlockSpec((tm, d), lambda i: (i, 0)),
                  pl.BlockSpec((1, d), lambda i: (0, 0)),
                  pl.BlockSpec((1, d), lambda i: (0, 0))],
        out_specs=pl.BlockSpec((tm, d), lambda i: (i, 0)),
        out_shape=jax.ShapeDtypeStruct((m, d), F32),
        compiler_params=_cparams("parallel"),
        name="out_proj_residual_ln",
    )(o, w, x, g.reshape(1, d), b.reshape(1, d))


def _ffn_body(x_ref, xh_ref, w_in_ref, cw_ref, cb_ref, wo_ref, g_ref, b_ref,
              out_ref, xb_ref, xp_ref, hu_ref, hg_ref, act_ref, *, tm, tiles_per_seq, ff, cols):
    i = pl.program_id(0)
    d = x_ref.shape[1]
    halo = BF16_ROWS
    grp = SUBLANES
    n_grp = tm // grp
    first = (i % tiles_per_seq) == 0
    xb_ref[0:halo, :] = jnp.where(first, 0.0, xh_ref[...]).astype(BF16)
    xp = x_ref[...].reshape(grp, n_grp, d).swapaxes(0, 1).reshape(tm, d)
    xp_ref[...] = xp
    xb_ref[halo:, :] = xp.astype(BF16)
    sub = lax.broadcasted_iota(jnp.int32, (grp, cols), 0)

    def hidden(h_ref, c):
        h = jnp.dot(xb_ref[...], w_in_ref[:, c], preferred_element_type=F32)
        h_ref[halo:, :] = h[halo:]
        for back in (1, 2):
            last = h[halo + tm - back * grp:halo + tm - (back - 1) * grp]
            prev = jnp.where(sub == 0, h[halo - back:halo - back + 1], pltpu.roll(last, 1, 0))
            h_ref[halo - back * grp:halo - (back - 1) * grp, :] = prev

    def conv(h_ref, c):
        return (cw_ref[0:1, c] * h_ref[pl.ds(halo - 2 * grp, tm), :]
                + cw_ref[1:2, c] * h_ref[pl.ds(halo - grp, tm), :]
                + cw_ref[2:3, c] * h_ref[pl.ds(halo, tm), :]
                + cb_ref[:, c])

    for j in range(ff // cols):
        cu = slice(j * cols, (j + 1) * cols)
        cg = slice(ff + j * cols, ff + (j + 1) * cols)
        hu = hu_ref.at[j % 2]
        hg = hg_ref.at[j % 2]
        hidden(hu, cu)
        hidden(hg, cg)
        u = conv(hu, cu)
        gt = conv(hg, cg)
        act_ref[:, cu] = ((gt * _sigmoid(gt)) * u).astype(BF16)

    y = DEEPNORM_ALPHA * xp_ref[...] + jnp.dot(act_ref[...], wo_ref[...], preferred_element_type=F32)
    yn = _layer_norm_rows(y, g_ref[...], b_ref[...])
    out_ref[...] = yn.reshape(n_grp, grp, d).swapaxes(0, 1).reshape(tm, d)


def _conv_ffn_ln(x, seq_len, layer, w_in, conv_w, conv_b, w_out, g, b, *, tm, cols):
    m, d = x.shape
    ff = w_out.shape[1]
    halo = BF16_ROWS
    body = functools.partial(_ffn_body, tm=tm, tiles_per_seq=seq_len // tm, ff=ff, cols=cols)
    resident = lambda shape: pl.BlockSpec((None,) + shape, lambda i: (layer, 0, 0),
                                          pipeline_mode=pl.Buffered(1))
    n_layers = w_in.shape[0]
    conv_b = conv_b.reshape(n_layers, 1, 2 * ff)
    g = g.reshape(n_layers, 1, d)
    b = b.reshape(n_layers, 1, d)
    return pl.pallas_call(
        body,
        grid=(m // tm,),
        in_specs=[pl.BlockSpec((tm, d), lambda i: (i, 0)),
                  pl.BlockSpec((halo, d), lambda i: (jnp.maximum(i * (tm // halo) - 1, 0), 0)),
                  resident((d, 2 * ff)),
                  resident((3, 2 * ff)),
                  resident((1, 2 * ff)),
                  resident((ff, d)),
                  resident((1, d)),
                  resident((1, d))],
        out_specs=pl.BlockSpec((tm, d), lambda i: (i, 0)),
        out_shape=jax.ShapeDtypeStruct((m, d), F32),
        scratch_shapes=[pltpu.VMEM((tm + halo, d), BF16),
                        pltpu.VMEM((tm, d), F32),
                        pltpu.VMEM((2, tm + halo, cols), F32),
                        pltpu.VMEM((2, tm + halo, cols), F32),
                        pltpu.VMEM((tm, ff), BF16)],
        compiler_params=_cparams("parallel"),
        name="conv_ffn_ln",
    )(x, x, w_in, conv_w, conv_b, w_out, g, b)


def _head_pair_sum(v, ones_bd):
    hi = v.astype(BF16)
    lo = (v - hi.astype(F32)).astype(BF16)
    return (jnp.dot(hi, ones_bd, preferred_element_type=F32)
            + jnp.dot(lo, ones_bd, preferred_element_type=F32))


def _rwkv_proj_body(x_ref, xp_ref, mix_ref, wrkv_ref, w1_ref, w2_ref, a1_ref, a2_ref, g1_ref, g2_ref,
                    w0_ref, a0_ref, kk_ref, ka_ref,
                    r_out, ld_out, k_out, v_out, kk_out, b_out, gate_out, *, tm, tiles_per_seq):
    i = pl.program_id(0)
    d = x_ref.shape[1]
    x = x_ref[...]
    first = (i % tiles_per_seq) == 0
    prev = jnp.where(first, 0.0, xp_ref[SUBLANES - 1:SUBLANES, :])
    row = lax.broadcasted_iota(jnp.int32, (tm, d), 0)
    xs = jnp.where(row == 0, prev, pltpu.roll(x, 1, 0))
    xx = xs - x

    x_b = x.astype(BF16)
    xx_b = xx.astype(BF16)

    def mixed(j):
        return x_b + xx_b * mix_ref[j:j + 1, :].astype(BF16)

    def mm(a, w):
        return jnp.dot(a, w, preferred_element_type=F32)

    r = mm(mixed(0), wrkv_ref[0])
    k = mm(mixed(2), wrkv_ref[1])
    v = mm(mixed(3), wrkv_ref[2])
    lw = w0_ref[...] + mm(jnp.tanh(mm(mixed(1), w1_ref[...])).astype(BF16), w2_ref[...])
    softplus_neg = jnp.maximum(-lw, 0.0) + jnp.log(1.0 + jnp.exp(-jnp.abs(lw)))
    log_decay = -jnp.exp(-softplus_neg - 0.5)
    a = _sigmoid(a0_ref[...] + mm(mm(mixed(4), a1_ref[...]).astype(BF16), a2_ref[...]))
    gate = mm(_sigmoid(mm(mixed(5), g1_ref[...])).astype(BF16), g2_ref[...])

    kk = k * kk_ref[...]
    rr = lax.broadcasted_iota(jnp.int32, (LANES, LANES), 0) // RWKV_HEAD
    cc = lax.broadcasted_iota(jnp.int32, (LANES, LANES), 1) // RWKV_HEAD
    ones_bd = jnp.where(rr == cc, 1.0, 0.0).astype(BF16)
    sq = kk * kk
    ss = jnp.concatenate([_head_pair_sum(sq[:, j * LANES:(j + 1) * LANES], ones_bd)
                          for j in range(d // LANES)], axis=1)
    kk = kk / jnp.maximum(jnp.sqrt(ss), 1e-12)

    r_out[...] = r
    ld_out[...] = log_decay
    k_out[...] = k * (1.0 + (a - 1.0) * ka_ref[...])
    v_out[...] = v
    kk_out[...] = kk
    b_out[...] = kk * a
    gate_out[...] = gate


def _rwkv_proj(x, seq_len, mix, w_rkv, w0, w1, w2, a0, a1, a2, g1, g2, k_k, k_a, *, tm):
    m, d = x.shape
    full = lambda arr: pl.BlockSpec(arr.shape, lambda i: (0,) * arr.ndim, pipeline_mode=pl.Buffered(1))
    vec = lambda a: a.reshape(1, d)
    args = (mix, w_rkv.astype(BF16), w1.astype(BF16), w2.astype(BF16), a1.astype(BF16), a2.astype(BF16),
            g1.astype(BF16), g2.astype(BF16), vec(w0), vec(a0), vec(k_k), vec(k_a))
    row_spec = pl.BlockSpec((tm, d), lambda i: (i, 0))
    return pl.pallas_call(
        functools.partial(_rwkv_proj_body, tm=tm, tiles_per_seq=seq_len // tm),
        grid=(m // tm,),
        in_specs=[row_spec,
                  pl.BlockSpec((SUBLANES, d), lambda i: (jnp.maximum(i * (tm // SUBLANES) - 1, 0), 0))]
                 + [full(a) for a in args],
        out_specs=[row_spec] * 7,
        out_shape=[jax.ShapeDtypeStruct((m, d), F32)] * 7,
        compiler_params=_cparams("parallel"),
        name="rwkv_proj",
    )(x, x, *args)


def _bf16_dot(a, b, dims):
    return lax.dot_general(a.astype(BF16), b.astype(BF16), dims, preferred_element_type=F32)


def _head_sums(tiles, ones_bd):
    rows = tiles[0].shape[0]
    parts = []
    for t in tiles:
        hi = t.astype(BF16)
        parts += [hi, (t - hi.astype(F32)).astype(BF16)]
    res = jnp.dot(jnp.concatenate(parts, axis=0), ones_bd, preferred_element_type=F32)
    return [res[2 * i * rows:(2 * i + 1) * rows] + res[(2 * i + 1) * rows:(2 * i + 2) * rows]
            for i in range(len(tiles))]


def _rwkv_chunk_body(r_ref, ld_ref, k_ref, v_ref, kk_ref, b_ref, gate_ref, rk_ref, gg_ref, gb_ref,
                     o_ref, st_ref, *, chunk, pairs, subs):
    c = pl.program_id(2)

    @pl.when(c == 0)
    def _():
        st_ref[...] = jnp.zeros_like(st_ref)

    n = RWKV_HEAD
    row = lax.broadcasted_iota(jnp.int32, (chunk, LANES), 0)
    lane = lax.broadcasted_iota(jnp.int32, (chunk, LANES), 1)
    col = lane % n
    eye2 = jnp.where(row == col, 1.0, 0.0)
    head1 = lane >= n
    row2 = lax.broadcasted_iota(jnp.int32, (2 * chunk, LANES), 0)
    col2 = lax.broadcasted_iota(jnp.int32, (2 * chunk, LANES), 1) % n
    tri2 = jnp.where(row2 < chunk, row2 - 1, row2 - chunk) >= col2
    rr = lax.broadcasted_iota(jnp.int32, (LANES, LANES), 0) // n
    cc = lax.broadcasted_iota(jnp.int32, (LANES, LANES), 1) // n
    bd = rr == cc
    ones_bd = jnp.where(bd, 1.0, 0.0).astype(BF16)
    nn = (((1,), (0,)), ((), ()))

    def head_stack(t):
        return jnp.concatenate([jnp.where(head1, 0.0, t), jnp.where(head1, t, 0.0)], axis=0)

    def block_diag(t):
        return jnp.where(bd, jnp.concatenate([t, t], axis=0), 0.0)

    units = [(cc, p) for cc in range(subs) for p in range(pairs)]
    U = range(len(units))
    rows = [slice(cc * chunk, (cc + 1) * chunk) for cc, _ in units]
    sls = [slice(p * LANES, (p + 1) * LANES) for _, p in units]
    r = [r_ref[0, rows[i], sls[i]] for i in U]
    ld = [ld_ref[0, rows[i], sls[i]] for i in U]
    k = [k_ref[0, rows[i], sls[i]] for i in U]
    v = [v_ref[0, rows[i], sls[i]] for i in U]
    bv = [b_ref[0, rows[i], sls[i]] for i in U]

    cum = list(ld)
    shift = 1
    while shift < chunk:
        cum = [cu + jnp.where(row >= shift, pltpu.roll(cu, shift, 0), 0.0) for cu in cum]
        shift *= 2
    cum_last = [cu[chunk - 1:chunk, :] for cu in cum]
    e_inv = [jnp.exp(-cu) for cu in cum]
    r_hat = [r[i] * jnp.exp(cum[i]) for i in U]
    a_hat = [-(kk_ref[0, rows[i], sls[i]] * jnp.exp(cum[i] - ld[i])) for i in U]
    lhs = [jnp.concatenate([a_hat[i], r_hat[i]], axis=0) for i in U]
    sbk = [_bf16_dot(lhs[i], jnp.concatenate([head_stack(bv[i] * e_inv[i]), head_stack(k[i] * e_inv[i])], axis=0),
                     _NT) for i in U]
    sb = [jnp.where(tri2, t[:, :LANES], 0.0) for t in sbk]
    sk = [jnp.where(tri2, t[:, LANES:], 0.0) for t in sbk]
    a_ab = [t[:chunk] for t in sb]
    a_rb = [t[chunk:] for t in sb]

    xp = [_bf16_dot(t, block_diag(t), nn) for t in a_ab]
    tinv = [eye2 + t for t in a_ab]
    power = 2
    while 2 * power < chunk:
        both = [_bf16_dot(jnp.concatenate([xp[i], tinv[i]], axis=0), block_diag(xp[i]), nn) for i in U]
        xp = [t[:chunk] for t in both]
        tinv = [tinv[i] + both[i][chunk:] for i in U]
        power *= 2
    tinv = [tinv[i] + _bf16_dot(tinv[i], block_diag(xp[i]), nn) for i in U]

    state = [st_ref[p] for p in range(pairs)]
    y = [None] * len(units)
    for cc in range(subs):
        ids = [cc * pairs + p for p in range(pairs)]
        zy = [_bf16_dot(jnp.concatenate([lhs[i], sk[i]], axis=1),
                        jnp.concatenate([state[p].T, head_stack(v[i])], axis=0), nn)
              for p, i in enumerate(ids)]
        u = [_bf16_dot(tinv[i], head_stack(zy[p][:chunk]), nn) for p, i in enumerate(ids)]
        for p, i in enumerate(ids):
            y[i] = zy[p][chunk:] + _bf16_dot(a_rb[i], head_stack(u[p]), nn)
            e_last = jnp.exp(cum_last[i] - cum[i])
            s_new = state[p] * jnp.exp(cum_last[i]) + _bf16_dot(
                jnp.concatenate([u[p], v[i]], axis=0),
                jnp.concatenate([bv[i] * e_last, k[i] * e_last], axis=0), _TN)
            state[p] = jnp.where(bd, s_new, 0.0)
    for p in range(pairs):
        st_ref[p] = state[p]

    sums = _head_sums(y + [r[i] * k[i] * rk_ref[:, sls[i]] for i in U], ones_bd)
    dlt = [y[i] - sums[i] * (1.0 / n) for i in U]
    var = _head_sums([t * t for t in dlt], ones_bd)
    for i in U:
        yn = dlt[i] * lax.rsqrt(var[i] * (1.0 / n) + RWKV_GN_EPS) * gg_ref[:, sls[i]] + gb_ref[:, sls[i]]
        bonus = sums[len(units) + i] * v[i]
        o_ref[0, rows[i], sls[i]] = ((yn + bonus) * gate_ref[0, rows[i], sls[i]]).astype(o_ref.dtype)


def _rwkv_chunk(r, ld, k, v, kk, bvec, gate, r_k, gn_g, gn_b, *, pairs, subs):
    bsz, t_len, d = r.shape
    chunk = RWKV_HEAD
    width = pairs * LANES
    seq_spec = pl.BlockSpec((1, subs * chunk, width), lambda b, g, c: (b, c, g))
    vec_spec = pl.BlockSpec((1, width), lambda b, g, c: (0, g))
    return pl.pallas_call(
        functools.partial(_rwkv_chunk_body, chunk=chunk, pairs=pairs, subs=subs),
        grid=(bsz, d // width, t_len // (subs * chunk)),
        in_specs=[seq_spec] * 7 + [vec_spec] * 3,
        out_specs=seq_spec,
        out_shape=jax.ShapeDtypeStruct((bsz, t_len, d), BF16),
        scratch_shapes=[pltpu.VMEM((pairs, LANES, LANES), F32)],
        compiler_params=_cparams("parallel", "parallel", "arbitrary"),
        name="rwkv_chunk",
    )(r, ld, k, v, kk, bvec, gate, r_k.reshape(1, d), gn_g.reshape(1, d), gn_b.reshape(1, d))


def _rwkv_mixer(x, bsz, t_len, mix, w_rkv, w0, w1, w2, a0, a1, a2, g1, g2, k_k, k_a, r_k, gn_g, gn_b, w_o,
                ln_g, ln_b):
    d = x.shape[1]
    outs = _rwkv_proj(x, t_len, mix, w_rkv, w0, w1, w2, a0, a1, a2, g1, g2, k_k, k_a, tm=512)
    seq = [o.reshape(bsz, t_len, d) for o in outs]
    o = _rwkv_chunk(*seq, r_k, gn_g, gn_b, pairs=8, subs=2)
    return _matmul_residual_ln(o.reshape(bsz * t_len, d), w_o.astype(BF16), x, ln_g, ln_b, tm=1024)


def _ret_chunk_body(q_ref, k_ref, v_ref, gate_ref, cos_ref, sin_ref, g_ref, b_ref, o_ref, st_ref, *, chunk):
    c = pl.program_id(1)

    @pl.when(c == 0)
    def _():
        st_ref[...] = jnp.zeros_like(st_ref)

    n_heads, dk, dv = st_ref.shape
    half = dk // 2
    cos = cos_ref[...]
    sin = sin_ref[...]

    def rotate(t):
        te, to = t[:, :half], t[:, half:]
        return jnp.concatenate([te * cos - to * sin, to * cos + te * sin], axis=1)

    ri = lax.broadcasted_iota(jnp.int32, (chunk, chunk), 0)
    ci = lax.broadcasted_iota(jnp.int32, (chunk, chunk), 1)
    rel = (ri - ci).astype(F32)
    pos = lax.broadcasted_iota(jnp.int32, (chunk, dk), 0).astype(F32)

    for h in range(n_heads):
        log_gamma = math.log(1.0 - 2.0 ** (-5.0 - h))
        inner = jnp.where(rel >= 0, jnp.exp(log_gamma * jnp.maximum(rel, 0.0)), 0.0)
        cross = jnp.exp(log_gamma * (pos + 1.0))
        sdecay = jnp.exp(log_gamma * (chunk - 1.0 - pos))
        chunk_decay = jnp.exp(log_gamma * chunk)

        q = rotate(q_ref[0, :, h * dk:(h + 1) * dk].astype(F32))
        k = rotate(k_ref[0, :, h * dk:(h + 1) * dk].astype(F32)) * (dk ** -0.5)
        v = v_ref[0, :, h * dv:(h + 1) * dv].astype(BF16)
        s = lax.dot_general(q.astype(BF16), k.astype(BF16), _NT, preferred_element_type=F32) * inner
        st = st_ref[h]
        o = (jnp.dot(s.astype(BF16), v, preferred_element_type=F32)
             + jnp.dot((q * cross).astype(BF16), st.astype(BF16), preferred_element_type=F32))
        st_ref[h] = st * chunk_decay + lax.dot_general((k * sdecay).astype(BF16), v, _TN,
                                                       preferred_element_type=F32)

        mu = jnp.mean(o, axis=-1, keepdims=True)
        dlt = o - mu
        var = jnp.mean(dlt * dlt, axis=-1, keepdims=True)
        vs = slice(h * dv, (h + 1) * dv)
        on = dlt * lax.rsqrt(var + RET_GN_EPS) * g_ref[:, vs] + b_ref[:, vs]
        gt = gate_ref[0, :, vs].astype(F32)
        o_ref[0, :, vs] = (gt * _sigmoid(gt) * on).astype(o_ref.dtype)


def _ret_chunk(proj, cos, sin, gn_g, gn_b, *, chunk):
    bsz, t_len, six_d = proj.shape
    d = six_d // 6
    h = RET_HEADS
    dk, dv = d // h, 2 * d // h
    return pl.pallas_call(
        functools.partial(_ret_chunk_body, chunk=chunk),
        grid=(bsz, t_len // chunk),
        in_specs=[pl.BlockSpec((1, chunk, d), lambda b, c: (b, c, 0)),
                  pl.BlockSpec((1, chunk, d), lambda b, c: (b, c, 1)),
                  pl.BlockSpec((1, chunk, 2 * d), lambda b, c: (b, c, 1)),
                  pl.BlockSpec((1, chunk, 2 * d), lambda b, c: (b, c, 2)),
                  pl.BlockSpec((chunk, dk // 2), lambda b, c: (c, 0)),
                  pl.BlockSpec((chunk, dk // 2), lambda b, c: (c, 0)),
                  pl.BlockSpec((1, 2 * d), lambda b, c: (0, 0)),
                  pl.BlockSpec((1, 2 * d), lambda b, c: (0, 0))],
        out_specs=pl.BlockSpec((1, chunk, 2 * d), lambda b, c: (b, c, 0)),
        out_shape=jax.ShapeDtypeStruct((bsz, t_len, 2 * d), BF16),
        scratch_shapes=[pltpu.VMEM((h, dk, dv), F32)],
        compiler_params=_cparams("parallel", "arbitrary"),
        name="retention_chunk",
    )(proj, proj, proj, proj, cos, sin, gn_g.reshape(1, 2 * d), gn_b.reshape(1, 2 * d))


def _retention_mixer(x, bsz, t_len, w_in, gn_g, gn_b, w_o, ln_g, ln_b):
    d = x.shape[1]
    h = RET_HEADS
    dk = d // h
    w_qk = w_in[:, :2 * d].reshape(d, 2 * h, dk // 2, 2).transpose(0, 1, 3, 2).reshape(d, 2 * d)
    w_perm = jnp.concatenate([w_qk, w_in[:, 2 * d:]], axis=1).astype(BF16)
    proj = _matmul(x, w_perm, tm=2048, tn=512, out_dtype=BF16)
    inv = 1.0 / (RET_ROPE_BASE ** jnp.linspace(0.0, 1.0, dk // 2, dtype=F32))
    ang = jnp.arange(t_len, dtype=F32)[:, None] * inv[None, :]
    o = _ret_chunk(proj.reshape(bsz, t_len, 6 * d), jnp.cos(ang), jnp.sin(ang), gn_g, gn_b, chunk=256)
    return _matmul_residual_ln(o.reshape(bsz * t_len, 2 * d), w_o.astype(BF16), x, ln_g, ln_b, tm=1024)


def _moba_body(qt_ref, vt_ref, k_ref, o_ref, kmean_ref, kb_ref, va_ref, bias_ref, *, n_blk, pairs):
    qb = pl.program_id(2)
    blk = MOBA_BLOCK
    hd = MOBA_HEAD_DIM
    heads = 2 * pairs
    aug = hd + BF16_ROWS
    t_len = n_blk * blk
    H = range(heads)

    @pl.when(qb == 0)
    def _():
        kf = k_ref[...]
        kmean_ref[...] = jnp.mean(kf.reshape(n_blk, blk, pairs * LANES), axis=1)
        kb_ref[...] = kf.astype(BF16)
        ones = jnp.ones((BF16_ROWS, t_len), BF16)
        for h in H:
            va_ref[h, 0:hd, :] = vt_ref[h * hd:(h + 1) * hd, :].astype(BF16)
            va_ref[h, hd:aug, :] = ones

    kmean = kmean_ref[...]
    qt = qt_ref[...]
    zeros = jnp.zeros((hd, blk), F32)
    blk_id = lax.broadcasted_iota(jnp.int32, (n_blk, blk), 0)
    scale = hd ** -0.5

    qz_b = []
    for h in H:
        qh = qt[h * hd:(h + 1) * hd]
        qz = jnp.concatenate([qh, zeros] if h % 2 == 0 else [zeros, qh], axis=0)
        lanes = slice((h // 2) * LANES, (h // 2 + 1) * LANES)
        gate = jnp.dot(kmean[:, lanes], qz, precision=HIGHEST, preferred_element_type=F32)
        beaten = jnp.zeros((n_blk, blk), F32)
        for m in range(n_blk):
            gm = gate[m:m + 1, :]
            wins = jnp.where(gm > gate, 1.0, jnp.where(gm == gate, jnp.where(blk_id > m, 1.0, 0.0), 0.0))
            beaten = beaten + jnp.where(m < qb, wins, 0.0)
        bias_ref[h] = jnp.where(blk_id < qb, jnp.where(beaten < MOBA_TOPK, 0.0, NEG_BIG), NEG_BIG)
        qz_b.append((qz * scale).astype(BF16))

    def scores(off):
        kb = kb_ref[pl.ds(off, blk), :]
        return tuple(jnp.dot(kb[:, (h // 2) * LANES:(h // 2 + 1) * LANES], qz_b[h],
                             preferred_element_type=F32) for h in H)

    def attend(off, s, carry):
        m_new = [jnp.maximum(carry[h][0], jnp.max(s[h], axis=0, keepdims=True)) for h in H]
        p = [jnp.exp(s[h] - m_new[h]).astype(BF16) for h in H]
        alpha = [jnp.exp(carry[h][0] - m_new[h]) for h in H]
        acc = [alpha[h] * carry[h][1]
               + jnp.dot(va_ref[h, :, pl.ds(off, blk)], p[h], preferred_element_type=F32) for h in H]
        return tuple((m_new[h], acc[h]) for h in H)

    def past_block(nb, carry):
        off = pl.multiple_of(nb * blk, blk)
        s = scores(off)
        s = [s[h] + bias_ref[h, pl.ds(nb, 1), :] for h in H]
        return attend(off, s, carry)

    init = tuple((jnp.full((1, blk), NEG_BIG, F32), jnp.zeros((aug, blk), F32)) for _ in H)
    carry = lax.fori_loop(0, qb, past_block, init)
    key_i = lax.broadcasted_iota(jnp.int32, (blk, blk), 0)
    qry_i = lax.broadcasted_iota(jnp.int32, (blk, blk), 1)
    causal_bias = jnp.where(key_i <= qry_i, 0.0, NEG_BIG)
    own = pl.multiple_of(qb * blk, blk)
    s_own = scores(own)
    res = attend(own, [s_own[h] + causal_bias for h in H], carry)
    for h in H:
        acc = res[h][1]
        o_ref[h * hd:(h + 1) * hd, :] = acc[:hd] / acc[hd:hd + 1]


def _moba_attention(qvt, k, bsz, t_len, *, pairs):
    d = k.shape[1]
    blk = MOBA_BLOCK
    n_blk = t_len // blk
    width = pairs * LANES
    n_grp = d // width
    heads = 2 * pairs
    return pl.pallas_call(
        functools.partial(_moba_body, n_blk=n_blk, pairs=pairs),
        grid=(bsz, n_grp, n_blk),
        in_specs=[pl.BlockSpec((width, blk), lambda b, p, q: (p, b * n_blk + q)),
                  pl.BlockSpec((width, t_len), lambda b, p, q: (n_grp + p, b)),
                  pl.BlockSpec((t_len, width), lambda b, p, q: (b, p))],
        out_specs=pl.BlockSpec((width, blk), lambda b, p, q: (p, b * n_blk + q)),
        out_shape=jax.ShapeDtypeStruct((d, bsz * t_len), F32),
        scratch_shapes=[pltpu.VMEM((n_blk, width), F32),
                        pltpu.VMEM((t_len, width), BF16),
                        pltpu.VMEM((heads, MOBA_HEAD_DIM + BF16_ROWS, t_len), BF16),
                        pltpu.VMEM((heads, n_blk, blk), F32)],
        compiler_params=_cparams("parallel", "parallel", "arbitrary"),
        name="moba_attention",
    )(qvt, qvt, k)


def _moba_mixer(x, bsz, t_len, w_qkv, w_o, ln_g, ln_b):
    d = x.shape[1]
    assert t_len % MOBA_BLOCK == 0
    w_k = w_qkv[:, d:2 * d].astype(BF16)
    w_qv_t = jnp.concatenate([w_qkv[:, :d], w_qkv[:, 2 * d:]], axis=1).T.astype(BF16)
    k = _matmul(x, w_k, tm=2048, tn=512)
    qvt = _matmul_nt(x, w_qv_t, tm=2048, tn=512)
    ot = _moba_attention(qvt, k, bsz, t_len, pairs=4)
    return _matmul_residual_ln(ot, w_o.astype(BF16), x, ln_g, ln_b, tm=1024, lhs_transposed=True)


def kernel(x, rwkv_mix, rwkv_w_rkv, rwkv_w0, rwkv_w1, rwkv_w2, rwkv_a0, rwkv_a1, rwkv_a2, rwkv_g1, rwkv_g2,
           rwkv_k_k, rwkv_k_a, rwkv_r_k, rwkv_gn_g, rwkv_gn_b, rwkv_w_o, ret_w_in, ret_gn_g, ret_gn_b, ret_w_o,
           moba_w_qkv, moba_w_o, ffn_w_in, ffn_conv_w, ffn_conv_b, ffn_w_out, ln1_g, ln1_b, ln2_g, ln2_b):
    bsz, t_len, d = x.shape
    h = x.reshape(bsz * t_len, d)
    ffn_w_in_b = ffn_w_in.astype(BF16)
    ffn_w_out_b = ffn_w_out.astype(BF16)
    for i in range(DEPTH):
        kind, j = i % N_MIXERS, i // N_MIXERS
        if kind == 0:
            h = _rwkv_mixer(h, bsz, t_len, rwkv_mix[j], rwkv_w_rkv[j], rwkv_w0[j], rwkv_w1[j], rwkv_w2[j],
                            rwkv_a0[j], rwkv_a1[j], rwkv_a2[j], rwkv_g1[j], rwkv_g2[j], rwkv_k_k[j],
                            rwkv_k_a[j], rwkv_r_k[j], rwkv_gn_g[j], rwkv_gn_b[j], rwkv_w_o[j],
                            ln1_g[i], ln1_b[i])
        elif kind == 1:
            h = _retention_mixer(h, bsz, t_len, ret_w_in[j], ret_gn_g[j], ret_gn_b[j], ret_w_o[j],
                                 ln1_g[i], ln1_b[i])
        else:
            h = _moba_mixer(h, bsz, t_len, moba_w_qkv[j], moba_w_o[j], ln1_g[i], ln1_b[i])
        h = _conv_ffn_ln(h, t_len, i, ffn_w_in_b, ffn_conv_w, ffn_conv_b, ffn_w_out_b, ln2_g, ln2_b,
                         tm=512, cols=256)
    return h.reshape(bsz, t_len, d)
```

```python
import functools
import math

import jax
import jax.numpy as jnp
from jax import lax
from jax.experimental import pallas as pl
from jax.experimental.pallas import tpu as pltpu

F32 = jnp.float32
BF16 = jnp.bfloat16
HIGHEST = lax.Precision.HIGHEST

DEPTH = 4
N_MIXERS = 3
RWKV_HEAD = 64
RWKV_GN_EPS = 64e-5
RET_HEADS = 4
RET_ROPE_BASE = 10000.0
RET_GN_EPS = 1e-5
MOBA_HEADS = 16
MOBA_HEAD_DIM = 64
MOBA_BLOCK = 256
MOBA_TOPK = 3
LN_EPS = 1e-5
DEEPNORM_ALPHA = (2 * DEPTH) ** 0.25

LANES = 128
SUBLANES = 8
BF16_ROWS = 16
VMEM_LIMIT_BYTES = 52 * 1024 * 1024

NEG_BIG = -1e30
LOG2_E = math.log2(math.e)

_NT = (((1,), (1,)), ((), ()))
_TN = (((0,), (0,)), ((), ()))


def _cparams(*sem):
    return pltpu.CompilerParams(dimension_semantics=sem, vmem_limit_bytes=VMEM_LIMIT_BYTES)


def _sigmoid(x):
    return 1.0 / (1.0 + jnp.exp(-x))


def _layer_norm_rows(y, g, b):
    mu = jnp.mean(y, axis=-1, keepdims=True)
    d = y - mu
    var = jnp.mean(d * d, axis=-1, keepdims=True)
    return d * lax.rsqrt(var + LN_EPS) * g + b


def _mm_body(x_ref, w_ref, o_ref, xb_ref):
    @pl.when(pl.program_id(1) == 0)
    def _():
        xb_ref[...] = x_ref[...].astype(BF16)

    o_ref[...] = jnp.dot(xb_ref[...], w_ref[...], preferred_element_type=F32).astype(o_ref.dtype)


def _matmul(x, w, *, tm, tn, out_dtype=F32):
    m, k = x.shape
    n = w.shape[1]
    return pl.pallas_call(
        _mm_body,
        grid=(m // tm, n // tn),
        in_specs=[pl.BlockSpec((tm, k), lambda i, j: (i, 0)),
                  pl.BlockSpec((k, tn), lambda i, j: (0, j))],
        out_specs=pl.BlockSpec((tm, tn), lambda i, j: (i, j)),
        out_shape=jax.ShapeDtypeStruct((m, n), out_dtype),
        scratch_shapes=[pltpu.VMEM((tm, k), BF16)],
        compiler_params=_cparams("parallel", "arbitrary"),
        name="matmul",
    )(x, w)


def _mm_nt_body(x_ref, wt_ref, o_ref, xb_ref):
    @pl.when(pl.program_id(1) == 0)
    def _():
        xb_ref[...] = x_ref[...].astype(BF16)

    o_ref[...] = lax.dot_general(wt_ref[...], xb_ref[...], _NT, preferred_element_type=F32)


def _matmul_nt(x, wt, *, tm, tn):
    m, k = x.shape
    n = wt.shape[0]
    return pl.pallas_call(
        _mm_nt_body,
        grid=(m // tm, n // tn),
        in_specs=[pl.BlockSpec((tm, k), lambda i, j: (i, 0)),
                  pl.BlockSpec((tn, k), lambda i, j: (j, 0))],
        out_specs=pl.BlockSpec((tn, tm), lambda i, j: (j, i)),
        out_shape=jax.ShapeDtypeStruct((n, m), F32),
        scratch_shapes=[pltpu.VMEM((tm, k), BF16)],
        compiler_params=_cparams("parallel", "arbitrary"),
        name="matmul_nt",
    )(x, wt)


def _mm_res_ln_body(o_ref, w_ref, x_ref, g_ref, b_ref, out_ref, *, lhs_transposed):
    lhs = o_ref[...].astype(BF16)
    if lhs_transposed:
        acc = lax.dot_general(lhs, w_ref[...], _TN, preferred_element_type=F32)
    else:
        acc = jnp.dot(lhs, w_ref[...], preferred_element_type=F32)
    y = DEEPNORM_ALPHA * x_ref[...] + acc
    out_ref[...] = _layer_norm_rows(y, g_ref[...], b_ref[...])


def _matmul_residual_ln(o, w, x, g, b, *, tm, lhs_transposed=False):
    m, d = x.shape
    k = w.shape[0]
    if lhs_transposed:
        o_spec = pl.BlockSpec((k, tm), lambda i: (0, i))
    else:
        o_spec = pl.BlockSpec((tm, k), lambda i: (i, 0))
    return pl.pallas_call(
        functools.partial(_mm_res_ln_body, lhs_transposed=lhs_transposed),
        grid=(m // tm,),
        in_specs=[o_spec,
                  pl.BlockSpec((k, d), lambda i: (0, 0)),
                  pl.BlockSpec((tm, d), lambda i: (i, 0)),
                  pl.BlockSpec((1, d), lambda i: (0, 0)),
                  pl.BlockSpec((1, d), lambda i: (0, 0))],
        out_specs=pl.BlockSpec((tm, d), lambda i: (i, 0)),
        out_shape=jax.ShapeDtypeStruct((m, d), F32),
        compiler_params=_cparams("parallel"),
        name="out_proj_residual_ln",
    )(o, w, x, g.reshape(1, d), b.reshape(1, d))


def _ffn_body(x_ref, xh_ref, w_in_ref, cw_ref, cb_ref, wo_ref, g_ref, b_ref,
              out_ref, xb_ref, xp_ref, hu_ref, hg_ref, act_ref, *, tm, tiles_per_seq, ff, cols):
    i = pl.program_id(0)
    d = x_ref.shape[1]
    halo = BF16_ROWS
    grp = SUBLANES
    n_grp = tm // grp
    first = (i % tiles_per_seq) == 0
    xb_ref[0:halo, :] = jnp.where(first, 0.0, xh_ref[...]).astype(BF16)
    xp = x_ref[...].reshape(grp, n_grp, d).swapaxes(0, 1).reshape(tm, d)
    xp_ref[...] = xp
    xb_ref[halo:, :] = xp.astype(BF16)
    sub = lax.broadcasted_iota(jnp.int32, (grp, cols), 0)

    def hidden(h_ref, c):
        h = jnp.dot(xb_ref[...], w_in_ref[:, c], preferred_element_type=F32)
        h_ref[halo:, :] = h[halo:]
        for back in (1, 2):
            last = h[halo + tm - back * grp:halo + tm - (back - 1) * grp]
            prev = jnp.where(sub == 0, h[halo - back:halo - back + 1], pltpu.roll(last, 1, 0))
            h_ref[halo - back * grp:halo - (back - 1) * grp, :] = prev

    def conv(h_ref, c):
        return (cw_ref[0:1, c] * h_ref[pl.ds(halo - 2 * grp, tm), :]
                + cw_ref[1:2, c] * h_ref[pl.ds(halo - grp, tm), :]
                + cw_ref[2:3, c] * h_ref[pl.ds(halo, tm), :]
                + cb_ref[:, c])

    for j in range(ff // cols):
        cu = slice(j * cols, (j + 1) * cols)
        cg = slice(ff + j * cols, ff + (j + 1) * cols)
        hu = hu_ref.at[j % 2]
        hg = hg_ref.at[j % 2]
        hidden(hu, cu)
        hidden(hg, cg)
        u = conv(hu, cu)
        gt = conv(hg, cg)
        act_ref[:, cu] = ((gt * _sigmoid(gt)) * u).astype(BF16)

    y = DEEPNORM_ALPHA * xp_ref[...] + jnp.dot(act_ref[...], wo_ref[...], preferred_element_type=F32)
    yn = _layer_norm_rows(y, g_ref[...], b_ref[...])
    out_ref[...] = yn.reshape(n_grp, grp, d).swapaxes(0, 1).reshape(tm, d)


def _conv_ffn_ln(x, seq_len, layer, w_in, conv_w, conv_b, w_out, g, b, *, tm, cols):
    m, d = x.shape
    ff = w_out.shape[1]
    halo = BF16_ROWS
    body = functools.partial(_ffn_body, tm=tm, tiles_per_seq=seq_len // tm, ff=ff, cols=cols)
    resident = lambda shape: pl.BlockSpec((None,) + shape, lambda i: (layer, 0, 0),
                                          pipeline_mode=pl.Buffered(1))
    n_layers = w_in.shape[0]
    conv_b = conv_b.reshape(n_layers, 1, 2 * ff)
    g = g.reshape(n_layers, 1, d)
    b = b.reshape(n_layers, 1, d)
    return pl.pallas_call(
        body,
        grid=(m // tm,),
        in_specs=[pl.BlockSpec((tm, d), lambda i: (i, 0)),
                  pl.BlockSpec((halo, d), lambda i: (jnp.maximum(i * (tm // halo) - 1, 0), 0)),
                  resident((d, 2 * ff)),
                  resident((3, 2 * ff)),
                  resident((1, 2 * ff)),
                  resident((ff, d)),
                  resident((1, d)),
                  resident((1, d))],
        out_specs=pl.BlockSpec((tm, d), lambda i: (i, 0)),
        out_shape=jax.ShapeDtypeStruct((m, d), F32),
        scratch_shapes=[pltpu.VMEM((tm + halo, d), BF16),
                        pltpu.VMEM((tm, d), F32),
                        pltpu.VMEM((2, tm + halo, cols), F32),
                        pltpu.VMEM((2, tm + halo, cols), F32),
                        pltpu.VMEM((tm, ff), BF16)],
        compiler_params=_cparams("parallel"),
        name="conv_ffn_ln",
    )(x, x, w_in, conv_w, conv_b, w_out, g, b)


def _head_pair_sum(v, ones_bd):
    hi = v.astype(BF16)
    lo = (v - hi.astype(F32)).astype(BF16)
    return (jnp.dot(hi, ones_bd, preferred_element_type=F32)
            + jnp.dot(lo, ones_bd, preferred_element_type=F32))


def _rwkv_proj_body(x_ref, xp_ref, mix_ref, wrkv_ref, w1_ref, w2_ref, a1_ref, a2_ref, g1_ref, g2_ref,
                    w0_ref, a0_ref, kk_ref, ka_ref,
                    r_out, ld_out, k_out, v_out, kk_out, b_out, gate_out, *, tm, tiles_per_seq):
    i = pl.program_id(0)
    d = x_ref.shape[1]
    x = x_ref[...]
    first = (i % tiles_per_seq) == 0
    prev = jnp.where(first, 0.0, xp_ref[SUBLANES - 1:SUBLANES, :])
    row = lax.broadcasted_iota(jnp.int32, (tm, d), 0)
    xs = jnp.where(row == 0, prev, pltpu.roll(x, 1, 0))
    xx = xs - x

    x_b = x.astype(BF16)
    xx_b = xx.astype(BF16)

    def mixed(j):
        return x_b + xx_b * mix_ref[j:j + 1, :].astype(BF16)

    def mm(a, w):
        return jnp.dot(a, w, preferred_element_type=F32)

    r = mm(mixed(0), wrkv_ref[0])
    k = mm(mixed(2), wrkv_ref[1])
    v = mm(mixed(3), wrkv_ref[2])
    lw = w0_ref[...] + mm(jnp.tanh(mm(mixed(1), w1_ref[...])).astype(BF16), w2_ref[...])
    softplus_neg = jnp.maximum(-lw, 0.0) + jnp.log(1.0 + jnp.exp(-jnp.abs(lw)))
    log_decay = -jnp.exp(-softplus_neg - 0.5)
    a = _sigmoid(a0_ref[...] + mm(mm(mixed(4), a1_ref[...]).astype(BF16), a2_ref[...]))
    gate = mm(_sigmoid(mm(mixed(5), g1_ref[...])).astype(BF16), g2_ref[...])

    kk = k * kk_ref[...]
    rr = lax.broadcasted_iota(jnp.int32, (LANES, LANES), 0) // RWKV_HEAD
    cc = lax.broadcasted_iota(jnp.int32, (LANES, LANES), 1) // RWKV_HEAD
    ones_bd = jnp.where(rr == cc, 1.0, 0.0).astype(BF16)
    sq = kk * kk
    ss = jnp.concatenate([_head_pair_sum(sq[:, j * LANES:(j + 1) * LANES], ones_bd)
                          for j in range(d // LANES)], axis=1)
    kk = kk / jnp.maximum(jnp.sqrt(ss), 1e-12)

    r_out[...] = r
    ld_out[...] = log_decay
    k_out[...] = k * (1.0 + (a - 1.0) * ka_ref[...])
    v_out[...] = v
    kk_out[...] = kk
    b_out[...] = kk * a
    gate_out[...] = gate


def _rwkv_proj(x, seq_len, mix, w_rkv, w0, w1, w2, a0, a1, a2, g1, g2, k_k, k_a, *, tm):
    m, d = x.shape
    full = lambda arr: pl.BlockSpec(arr.shape, lambda i: (0,) * arr.ndim, pipeline_mode=pl.Buffered(1))
    vec = lambda a: a.reshape(1, d)
    args = (mix, w_rkv.astype(BF16), w1.astype(BF16), w2.astype(BF16), a1.astype(BF16), a2.astype(BF16),
            g1.astype(BF16), g2.astype(BF16), vec(w0), vec(a0), vec(k_k), vec(k_a))
    row_spec = pl.BlockSpec((tm, d), lambda i: (i, 0))
    return pl.pallas_call(
        functools.partial(_rwkv_proj_body, tm=tm, tiles_per_seq=seq_len // tm),
        grid=(m // tm,),
        in_specs=[row_spec,
                  pl.BlockSpec((SUBLANES, d), lambda i: (jnp.maximum(i * (tm // SUBLANES) - 1, 0), 0))]
                 + [full(a) for a in args],
        out_specs=[row_spec] * 7,
        out_shape=[jax.ShapeDtypeStruct((m, d), F32)] * 7,
        compiler_params=_cparams("parallel"),
        name="rwkv_proj",
    )(x, x, *args)


def _bf16_dot(a, b, dims):
    return lax.dot_general(a.astype(BF16), b.astype(BF16), dims, preferred_element_type=F32)


def _head_sums(tiles, ones_bd):
    rows = tiles[0].shape[0]
    parts = []
    for t in tiles:
        hi = t.astype(BF16)
        parts += [hi, (t - hi.astype(F32)).astype(BF16)]
    res = jnp.dot(jnp.concatenate(parts, axis=0), ones_bd, preferred_element_type=F32)
    return [res[2 * i * rows:(2 * i + 1) * rows] + res[(2 * i + 1) * rows:(2 * i + 2) * rows]
            for i in range(len(tiles))]


def _rwkv_chunk_body(r_ref, ld_ref, k_ref, v_ref, kk_ref, b_ref, gate_ref, rk_ref, gg_ref, gb_ref,
                     o_ref, st_ref, *, chunk, pairs, subs):
    c = pl.program_id(2)

    @pl.when(c == 0)
    def _():
        st_ref[...] = jnp.zeros_like(st_ref)

    n = RWKV_HEAD
    row = lax.broadcasted_iota(jnp.int32, (chunk, LANES), 0)
    lane = lax.broadcasted_iota(jnp.int32, (chunk, LANES), 1)
    col = lane % n
    eye2 = jnp.where(row == col, 1.0, 0.0)
    head1 = lane >= n
    row2 = lax.broadcasted_iota(jnp.int32, (2 * chunk, LANES), 0)
    col2 = lax.broadcasted_iota(jnp.int32, (2 * chunk, LANES), 1) % n
    tri2 = jnp.where(row2 < chunk, row2 - 1, row2 - chunk) >= col2
    rr = lax.broadcasted_iota(jnp.int32, (LANES, LANES), 0) // n
    cc = lax.broadcasted_iota(jnp.int32, (LANES, LANES), 1) // n
    bd = rr == cc
    ones_bd = jnp.where(bd, 1.0, 0.0).astype(BF16)
    nn = (((1,), (0,)), ((), ()))

    def head_stack(t):
        return jnp.concatenate([jnp.where(head1, 0.0, t), jnp.where(head1, t, 0.0)], axis=0)

    def block_diag(t):
        return jnp.where(bd, jnp.concatenate([t, t], axis=0), 0.0)

    units = [(cc, p) for cc in range(subs) for p in range(pairs)]
    U = range(len(units))
    rows = [slice(cc * chunk, (cc + 1) * chunk) for cc, _ in units]
    sls = [slice(p * LANES, (p + 1) * LANES) for _, p in units]
    r = [r_ref[0, rows[i], sls[i]] for i in U]
    ld = [ld_ref[0, rows[i], sls[i]] for i in U]
    k = [k_ref[0, rows[i], sls[i]] for i in U]
    v = [v_ref[0, rows[i], sls[i]] for i in U]
    bv = [b_ref[0, rows[i], sls[i]] for i in U]

    cum = list(ld)
    shift = 1
    while shift < chunk:
        cum = [cu + jnp.where(row >= shift, pltpu.roll(cu, shift, 0), 0.0) for cu in cum]
        shift *= 2
    cum_last = [cu[chunk - 1:chunk, :] for cu in cum]
    e_inv = [jnp.exp(-cu) for cu in cum]
    r_hat = [r[i] * jnp.exp(cum[i]) for i in U]
    a_hat = [-(kk_ref[0, rows[i], sls[i]] * jnp.exp(cum[i] - ld[i])) for i in U]
    lhs = [jnp.concatenate([a_hat[i], r_hat[i]], axis=0) for i in U]
    sbk = [_bf16_dot(lhs[i], jnp.concatenate([head_stack(bv[i] * e_inv[i]), head_stack(k[i] * e_inv[i])], axis=0),
                     _NT) for i in U]
    sb = [jnp.where(tri2, t[:, :LANES], 0.0) for t in sbk]
    sk = [jnp.where(tri2, t[:, LANES:], 0.0) for t in sbk]
    a_ab = [t[:chunk] for t in sb]
    a_rb = [t[chunk:] for t in sb]

    xp = [_bf16_dot(t, block_diag(t), nn) for t in a_ab]
    tinv = [eye2 + t for t in a_ab]
    power = 2
    while 2 * power < chunk:
        both = [_bf16_dot(jnp.concatenate([xp[i], tinv[i]], axis=0), block_diag(xp[i]), nn) for i in U]
        xp = [t[:chunk] for t in both]
        tinv = [tinv[i] + both[i][chunk:] for i in U]
        power *= 2
    tinv = [tinv[i] + _bf16_dot(tinv[i], block_diag(xp[i]), nn) for i in U]

    state = [st_ref[p] for p in range(pairs)]
    y = [None] * len(units)
    for cc in range(subs):
        ids = [cc * pairs + p for p in range(pairs)]
        zy = [_bf16_dot(jnp.concatenate([lhs[i], sk[i]], axis=1),
                        jnp.concatenate([state[p].T, head_stack(v[i])], axis=0), nn)
              for p, i in enumerate(ids)]
        u = [_bf16_dot(tinv[i], head_stack(zy[p][:chunk]), nn) for p, i in enumerate(ids)]
        for p, i in enumerate(ids):
            y[i] = zy[p][chunk:] + _bf16_dot(a_rb[i], head_stack(u[p]), nn)
            e_last = jnp.exp(cum_last[i] - cum[i])
            s_new = state[p] * jnp.exp(cum_last[i]) + _bf16_dot(
                jnp.concatenate([u[p], v[i]], axis=0),
                jnp.concatenate([bv[i] * e_last, k[i] * e_last], axis=0), _TN)
            state[p] = jnp.where(bd, s_new, 0.0)
    for p in range(pairs):
        st_ref[p] = state[p]

    sums = _head_sums(y + [r[i] * k[i] * rk_ref[:, sls[i]] for i in U], ones_bd)
    dlt = [y[i] - sums[i] * (1.0 / n) for i in U]
    var = _head_sums([t * t for t in dlt], ones_bd)
    for i in U:
        yn = dlt[i] * lax.rsqrt(var[i] * (1.0 / n) + RWKV_GN_EPS) * gg_ref[:, sls[i]] + gb_ref[:, sls[i]]
        bonus = sums[len(units) + i] * v[i]
        o_ref[0, rows[i], sls[i]] = ((yn + bonus) * gate_ref[0, rows[i], sls[i]]).astype(o_ref.dtype)


def _rwkv_chunk(r, ld, k, v, kk, bvec, gate, r_k, gn_g, gn_b, *, pairs, subs):
    bsz, t_len, d = r.shape
    chunk = RWKV_HEAD
    width = pairs * LANES
    seq_spec = pl.BlockSpec((1, subs * chunk, width), lambda b, g, c: (b, c, g))
    vec_spec = pl.BlockSpec((1, width), lambda b, g, c: (0, g))
    return pl.pallas_call(
        functools.partial(_rwkv_chunk_body, chunk=chunk, pairs=pairs, subs=subs),
        grid=(bsz, d // width, t_len // (subs * chunk)),
        in_specs=[seq_spec] * 7 + [vec_spec] * 3,
        out_specs=seq_spec,
        out_shape=jax.ShapeDtypeStruct((bsz, t_len, d), BF16),
        scratch_shapes=[pltpu.VMEM((pairs, LANES, LANES), F32)],
        compiler_params=_cparams("parallel", "parallel", "arbitrary"),
        name="rwkv_chunk",
    )(r, ld, k, v, kk, bvec, gate, r_k.reshape(1, d), gn_g.reshape(1, d), gn_b.reshape(1, d))


def _rwkv_mixer(x, bsz, t_len, mix, w_rkv, w0, w1, w2, a0, a1, a2, g1, g2, k_k, k_a, r_k, gn_g, gn_b, w_o,
                ln_g, ln_b):
    d = x.shape[1]
    outs = _rwkv_proj(x, t_len, mix, w_rkv, w0, w1, w2, a0, a1, a2, g1, g2, k_k, k_a, tm=512)
    seq = [o.reshape(bsz, t_len, d) for o in outs]
    o = _rwkv_chunk(*seq, r_k, gn_g, gn_b, pairs=8, subs=2)
    return _matmul_residual_ln(o.reshape(bsz * t_len, d), w_o.astype(BF16), x, ln_g, ln_b, tm=1024)


def _ret_chunk_body(q_ref, k_ref, v_ref, gate_ref, cos_ref, sin_ref, g_ref, b_ref, o_ref, st_ref, *, chunk):
    c = pl.program_id(1)

    @pl.when(c == 0)
    def _():
        st_ref[...] = jnp.zeros_like(st_ref)

    n_heads, dk, dv = st_ref.shape
    half = dk // 2
    cos = cos_ref[...]
    sin = sin_ref[...]

    def rotate(t):
        te, to = t[:, :half], t[:, half:]
        return jnp.concatenate([te * cos - to * sin, to * cos + te * sin], axis=1)

    ri = lax.broadcasted_iota(jnp.int32, (chunk, chunk), 0)
    ci = lax.broadcasted_iota(jnp.int32, (chunk, chunk), 1)
    rel = (ri - ci).astype(F32)
    pos = lax.broadcasted_iota(jnp.int32, (chunk, dk), 0).astype(F32)

    for h in range(n_heads):
        log_gamma = math.log(1.0 - 2.0 ** (-5.0 - h))
        inner = jnp.where(rel >= 0, jnp.exp(log_gamma * jnp.maximum(rel, 0.0)), 0.0)
        cross = jnp.exp(log_gamma * (pos + 1.0))
        sdecay = jnp.exp(log_gamma * (chunk - 1.0 - pos))
        chunk_decay = jnp.exp(log_gamma * chunk)

        q = rotate(q_ref[0, :, h * dk:(h + 1) * dk].astype(F32))
        k = rotate(k_ref[0, :, h * dk:(h + 1) * dk].astype(F32)) * (dk ** -0.5)
        v = v_ref[0, :, h * dv:(h + 1) * dv].astype(BF16)
        s = lax.dot_general(q.astype(BF16), k.astype(BF16), _NT, preferred_element_type=F32) * inner
        st = st_ref[h]
        o = (jnp.dot(s.astype(BF16), v, preferred_element_type=F32)
             + jnp.dot((q * cross).astype(BF16), st.astype(BF16), preferred_element_type=F32))
        st_ref[h] = st * chunk_decay + lax.dot_general((k * sdecay).astype(BF16), v, _TN,
                                                       preferred_element_type=F32)

        mu = jnp.mean(o, axis=-1, keepdims=True)
        dlt = o - mu
        var = jnp.mean(dlt * dlt, axis=-1, keepdims=True)
        vs = slice(h * dv, (h + 1) * dv)
        on = dlt * lax.rsqrt(var + RET_GN_EPS) * g_ref[:, vs] + b_ref[:, vs]
        gt = gate_ref[0, :, vs].astype(F32)
        o_ref[0, :, vs] = (gt * _sigmoid(gt) * on).astype(o_ref.dtype)


def _ret_chunk(proj, cos, sin, gn_g, gn_b, *, chunk):
    bsz, t_len, six_d = proj.shape
    d = six_d // 6
    h = RET_HEADS
    dk, dv = d // h, 2 * d // h
    return pl.pallas_call(
        functools.partial(_ret_chunk_body, chunk=chunk),
        grid=(bsz, t_len // chunk),
        in_specs=[pl.BlockSpec((1, chunk, d), lambda b, c: (b, c, 0)),
                  pl.BlockSpec((1, chunk, d), lambda b, c: (b, c, 1)),
                  pl.BlockSpec((1, chunk, 2 * d), lambda b, c: (b, c, 1)),
                  pl.BlockSpec((1, chunk, 2 * d), lambda b, c: (b, c, 2)),
                  pl.BlockSpec((chunk, dk // 2), lambda b, c: (c, 0)),
                  pl.BlockSpec((chunk, dk // 2), lambda b, c: (c, 0)),
                  pl.BlockSpec((1, 2 * d), lambda b, c: (0, 0)),
                  pl.BlockSpec((1, 2 * d), lambda b, c: (0, 0))],
        out_specs=pl.BlockSpec((1, chunk, 2 * d), lambda b, c: (b, c, 0)),
        out_shape=jax.ShapeDtypeStruct((bsz, t_len, 2 * d), BF16),
        scratch_shapes=[pltpu.VMEM((h, dk, dv), F32)],
        compiler_params=_cparams("parallel", "arbitrary"),
        name="retention_chunk",
    )(proj, proj, proj, proj, cos, sin, gn_g.reshape(1, 2 * d), gn_b.reshape(1, 2 * d))


def _retention_mixer(x, bsz, t_len, w_in, gn_g, gn_b, w_o, ln_g, ln_b):
    d = x.shape[1]
    h = RET_HEADS
    dk = d // h
    w_qk = w_in[:, :2 * d].reshape(d, 2 * h, dk // 2, 2).transpose(0, 1, 3, 2).reshape(d, 2 * d)
    w_perm = jnp.concatenate([w_qk, w_in[:, 2 * d:]], axis=1).astype(BF16)
    proj = _matmul(x, w_perm, tm=2048, tn=512, out_dtype=BF16)
    inv = 1.0 / (RET_ROPE_BASE ** jnp.linspace(0.0, 1.0, dk // 2, dtype=F32))
    ang = jnp.arange(t_len, dtype=F32)[:, None] * inv[None, :]
    o = _ret_chunk(proj.reshape(bsz, t_len, 6 * d), jnp.cos(ang), jnp.sin(ang), gn_g, gn_b, chunk=256)
    return _matmul_residual_ln(o.reshape(bsz * t_len, 2 * d), w_o.astype(BF16), x, ln_g, ln_b, tm=1024)


def _moba_body(qt_ref, vt_ref, k_ref, o_ref, kmean_ref, kb_ref, va_ref, bias_ref, *, n_blk, pairs):
    qb = pl.program_id(2)
    blk = MOBA_BLOCK
    hd = MOBA_HEAD_DIM
    heads = 2 * pairs
    aug = hd + BF16_ROWS
    t_len = n_blk * blk
    H = range(heads)

    @pl.when(qb == 0)
    def _():
        kf = k_ref[...]
        kmean_ref[...] = jnp.mean(kf.reshape(n_blk, blk, pairs * LANES), axis=1)
        kb_ref[...] = kf.astype(BF16)
        ones = jnp.ones((BF16_ROWS, t_len), BF16)
        for h in H:
            va_ref[h, 0:hd, :] = vt_ref[h * hd:(h + 1) * hd, :].astype(BF16)
            va_ref[h, hd:aug, :] = ones

    kmean = kmean_ref[...]
    qt = qt_ref[...]
    zeros = jnp.zeros((hd, blk), F32)
    blk_id = lax.broadcasted_iota(jnp.int32, (n_blk, blk), 0)
    scale = hd ** -0.5

    qz_b = []
    for h in H:
        qh = qt[h * hd:(h + 1) * hd]
        qz = jnp.concatenate([qh, zeros] if h % 2 == 0 else [zeros, qh], axis=0)
        lanes = slice((h // 2) * LANES, (h // 2 + 1) * LANES)
        gate = jnp.dot(kmean[:, lanes], qz, precision=HIGHEST, preferred_element_type=F32)
        beaten = jnp.zeros((n_blk, blk), F32)
        for m in range(n_blk):
            gm = gate[m:m + 1, :]
            wins = jnp.where(gm > gate, 1.0, jnp.where(gm == gate, jnp.where(blk_id > m, 1.0, 0.0), 0.0))
            beaten = beaten + jnp.where(m < qb, wins, 0.0)
        bias_ref[h] = jnp.where(blk_id < qb, jnp.where(beaten < MOBA_TOPK, 0.0, NEG_BIG), NEG_BIG)
        qz_b.append((qz * (scale * LOG2_E)).astype(BF16))

    def scores(off):
        kb = kb_ref[pl.ds(off, blk), :]
        return tuple(jnp.dot(kb[:, (h // 2) * LANES:(h // 2 + 1) * LANES], qz_b[h],
                             preferred_element_type=F32) for h in H)

    def accumulate(off, s, m_new, shift, carry):
        p = [jnp.exp2(s[h] - shift[h]).astype(BF16) for h in H]
        alpha = [jnp.exp2(carry[h][0] - m_new[h]) for h in H]
        acc = [alpha[h] * carry[h][1]
               + jnp.dot(va_ref[h, :, pl.ds(off, blk)], p[h], preferred_element_type=F32) for h in H]
        return tuple((m_new[h], acc[h]) for h in H)

    def attend(off, s, carry):
        m_new = [jnp.maximum(carry[h][0], jnp.max(s[h], axis=0, keepdims=True)) for h in H]
        return accumulate(off, s, m_new, m_new, carry)

    def past_block(nb, carry):
        off = pl.multiple_of(nb * blk, blk)
        s = scores(off)
        chosen = [bias_ref[h, pl.ds(nb, 1), :] >= 0.0 for h in H]
        m_new = [jnp.where(chosen[h], jnp.maximum(carry[h][0], jnp.max(s[h], axis=0, keepdims=True)),
                           carry[h][0]) for h in H]
        shift = [jnp.where(chosen[h], m_new[h], -NEG_BIG) for h in H]
        return accumulate(off, s, m_new, shift, carry)

    init = tuple((jnp.full((1, blk), NEG_BIG, F32), jnp.zeros((aug, blk), F32)) for _ in H)
    carry = lax.fori_loop(0, qb, past_block, init)
    key_i = lax.broadcasted_iota(jnp.int32, (blk, blk), 0)
    qry_i = lax.broadcasted_iota(jnp.int32, (blk, blk), 1)
    causal_bias = jnp.where(key_i <= qry_i, 0.0, NEG_BIG)
    own = pl.multiple_of(qb * blk, blk)
    s_own = scores(own)
    res = attend(own, [s_own[h] + causal_bias for h in H], carry)
    for h in H:
        acc = res[h][1]
        o_ref[h * hd:(h + 1) * hd, :] = acc[:hd] / acc[hd:hd + 1]


def _moba_attention(qvt, k, bsz, t_len, *, pairs):
    d = k.shape[1]
    blk = MOBA_BLOCK
    n_blk = t_len // blk
    width = pairs * LANES
    n_grp = d // width
    heads = 2 * pairs
    return pl.pallas_call(
        functools.partial(_moba_body, n_blk=n_blk, pairs=pairs),
        grid=(bsz, n_grp, n_blk),
        in_specs=[pl.BlockSpec((width, blk), lambda b, p, q: (p, b * n_blk + q)),
                  pl.BlockSpec((width, t_len), lambda b, p, q: (n_grp + p, b)),
                  pl.BlockSpec((t_len, width), lambda b, p, q: (b, p))],
        out_specs=pl.BlockSpec((width, blk), lambda b, p, q: (p, b * n_blk + q)),
        out_shape=jax.ShapeDtypeStruct((d, bsz * t_len), F32),
        scratch_shapes=[pltpu.VMEM((n_blk, width), F32),
                        pltpu.VMEM((t_len, width), BF16),
                        pltpu.VMEM((heads, MOBA_HEAD_DIM + BF16_ROWS, t_len), BF16),
                        pltpu.VMEM((heads, n_blk, blk), F32)],
        compiler_params=_cparams("parallel", "parallel", "arbitrary"),
        name="moba_attention",
    )(qvt, qvt, k)


def _moba_mixer(x, bsz, t_len, w_qkv, w_o, ln_g, ln_b):
    d = x.shape[1]
    assert t_len % MOBA_BLOCK == 0
    w_k = w_qkv[:, d:2 * d].astype(BF16)
    w_qv_t = jnp.concatenate([w_qkv[:, :d], w_qkv[:, 2 * d:]], axis=1).T.astype(BF16)
    k = _matmul(x, w_k, tm=2048, tn=512)
    qvt = _matmul_nt(x, w_qv_t, tm=2048, tn=512)
    ot = _moba_attention(qvt, k, bsz, t_len, pairs=4)
    return _matmul_residual_ln(ot, w_o.astype(BF16), x, ln_g, ln_b, tm=1024, lhs_transposed=True)


def kernel(x, rwkv_mix, rwkv_w_rkv, rwkv_w0, rwkv_w1, rwkv_w2, rwkv_a0, rwkv_a1, rwkv_a2, rwkv_g1, rwkv_g2,
           rwkv_k_k, rwkv_k_a, rwkv_r_k, rwkv_gn_g, rwkv_gn_b, rwkv_w_o, ret_w_in, ret_gn_g, ret_gn_b, ret_w_o,
           moba_w_qkv, moba_w_o, ffn_w_in, ffn_conv_w, ffn_conv_b, ffn_w_out, ln1_g, ln1_b, ln2_g, ln2_b):
    bsz, t_len, d = x.shape
    h = x.reshape(bsz * t_len, d)
    ffn_w_in_b = ffn_w_in.astype(BF16)
    ffn_w_out_b = ffn_w_out.astype(BF16)
    for i in range(DEPTH):
        kind, j = i % N_MIXERS, i // N_MIXERS
        if kind == 0:
            h = _rwkv_mixer(h, bsz, t_len, rwkv_mix[j], rwkv_w_rkv[j], rwkv_w0[j], rwkv_w1[j], rwkv_w2[j],
                            rwkv_a0[j], rwkv_a1[j], rwkv_a2[j], rwkv_g1[j], rwkv_g2[j], rwkv_k_k[j],
                            rwkv_k_a[j], rwkv_r_k[j], rwkv_gn_g[j], rwkv_gn_b[j], rwkv_w_o[j],
                            ln1_g[i], ln1_b[i])
        elif kind == 1:
            h = _retention_mixer(h, bsz, t_len, ret_w_in[j], ret_gn_g[j], ret_gn_b[j], ret_w_o[j],
                                 ln1_g[i], ln1_b[i])
        else:
            h = _moba_mixer(h, bsz, t_len, moba_w_qkv[j], moba_w_o[j], ln1_g[i], ln1_b[i])
        h = _conv_ffn_ln(h, t_len, i, ffn_w_in_b, ffn_conv_w, ffn_conv_b, ffn_w_out_b, ln2_g, ln2_b,
                         tm=512, cols=256)
    return h.reshape(bsz, t_len, d)
```

```python
import functools
import math

import jax
import jax.numpy as jnp
from jax import lax
from jax.experimental import pallas as pl
from jax.experimental.pallas import tpu as pltpu

F32 = jnp.float32
BF16 = jnp.bfloat16
HIGHEST = lax.Precision.HIGHEST

DEPTH = 4
N_MIXERS = 3
RWKV_HEAD = 64
RWKV_GN_EPS = 64e-5
RET_HEADS = 4
RET_ROPE_BASE = 10000.0
RET_GN_EPS = 1e-5
MOBA_HEADS = 16
MOBA_HEAD_DIM = 64
MOBA_BLOCK = 256
MOBA_TOPK = 3
LN_EPS = 1e-5
DEEPNORM_ALPHA = (2 * DEPTH) ** 0.25

LANES = 128
SUBLANES = 8
BF16_ROWS = 16
VMEM_LIMIT_BYTES = 52 * 1024 * 1024

NEG_BIG = -1e30
LOG2_E = math.log2(math.e)

PROJ_ROWS = 2048
PROJ_COLS = 512
OUT_PROJ_ROWS = 1024
FFN_ROWS = 512
FFN_COL_SLICE = 256
RWKV_PROJ_ROWS = 512
RWKV_HEAD_PAIRS = 8
RWKV_SUBCHUNKS = 2
RET_CHUNK = 256
MOBA_HEAD_PAIRS = 4

_NT = (((1,), (1,)), ((), ()))
_TN = (((0,), (0,)), ((), ()))


def _cparams(*sem):
    return pltpu.CompilerParams(dimension_semantics=sem, vmem_limit_bytes=VMEM_LIMIT_BYTES)


def _sigmoid(x):
    return 1.0 / (1.0 + jnp.exp(-x))


def _layer_norm_rows(y, g, b):
    mu = jnp.mean(y, axis=-1, keepdims=True)
    d = y - mu
    var = jnp.mean(d * d, axis=-1, keepdims=True)
    return d * lax.rsqrt(var + LN_EPS) * g + b


def _mm_body(x_ref, w_ref, o_ref, xb_ref):
    @pl.when(pl.program_id(1) == 0)
    def _():
        xb_ref[...] = x_ref[...].astype(BF16)

    o_ref[...] = jnp.dot(xb_ref[...], w_ref[...], preferred_element_type=F32).astype(o_ref.dtype)


def _matmul(x, w, *, tm, tn, out_dtype=F32):
    m, k = x.shape
    n = w.shape[1]
    return pl.pallas_call(
        _mm_body,
        grid=(m // tm, n // tn),
        in_specs=[pl.BlockSpec((tm, k), lambda i, j: (i, 0)),
                  pl.BlockSpec((k, tn), lambda i, j: (0, j))],
        out_specs=pl.BlockSpec((tm, tn), lambda i, j: (i, j)),
        out_shape=jax.ShapeDtypeStruct((m, n), out_dtype),
        scratch_shapes=[pltpu.VMEM((tm, k), BF16)],
        compiler_params=_cparams("parallel", "arbitrary"),
        name="matmul",
    )(x, w)


def _mm_nt_body(x_ref, wt_ref, o_ref, xb_ref):
    @pl.when(pl.program_id(1) == 0)
    def _():
        xb_ref[...] = x_ref[...].astype(BF16)

    o_ref[...] = lax.dot_general(wt_ref[...], xb_ref[...], _NT, preferred_element_type=F32).astype(o_ref.dtype)


def _matmul_nt(x, wt, *, tm, tn, out_dtype=F32):
    m, k = x.shape
    n = wt.shape[0]
    return pl.pallas_call(
        _mm_nt_body,
        grid=(m // tm, n // tn),
        in_specs=[pl.BlockSpec((tm, k), lambda i, j: (i, 0)),
                  pl.BlockSpec((tn, k), lambda i, j: (j, 0))],
        out_specs=pl.BlockSpec((tn, tm), lambda i, j: (j, i)),
        out_shape=jax.ShapeDtypeStruct((n, m), out_dtype),
        scratch_shapes=[pltpu.VMEM((tm, k), BF16)],
        compiler_params=_cparams("parallel", "arbitrary"),
        name="matmul_nt",
    )(x, wt)


def _mm_res_ln_body(o_ref, w_ref, x_ref, g_ref, b_ref, out_ref, *, lhs_transposed):
    lhs = o_ref[...].astype(BF16)
    if lhs_transposed:
        acc = lax.dot_general(lhs, w_ref[...], _TN, preferred_element_type=F32)
    else:
        acc = jnp.dot(lhs, w_ref[...], preferred_element_type=F32)
    y = DEEPNORM_ALPHA * x_ref[...] + acc
    out_ref[...] = _layer_norm_rows(y, g_ref[...], b_ref[...])


def _matmul_residual_ln(o, w, x, g, b, *, tm, lhs_transposed=False):
    m, d = x.shape
    k = w.shape[0]
    if lhs_transposed:
        o_spec = pl.BlockSpec((k, tm), lambda i: (0, i))
    else:
        o_spec = pl.BlockSpec((tm, k), lambda i: (i, 0))
    return pl.pallas_call(
        functools.partial(_mm_res_ln_body, lhs_transposed=lhs_transposed),
        grid=(m // tm,),
        in_specs=[o_spec,
                  pl.BlockSpec((k, d), lambda i: (0, 0)),
                  pl.BlockSpec((tm, d), lambda i: (i, 0)),
                  pl.BlockSpec((1, d), lambda i: (0, 0)),
                  pl.BlockSpec((1, d), lambda i: (0, 0))],
        out_specs=pl.BlockSpec((tm, d), lambda i: (i, 0)),
        out_shape=jax.ShapeDtypeStruct((m, d), F32),
        compiler_params=_cparams("parallel"),
        name="out_proj_residual_ln",
    )(o, w, x, g.reshape(1, d), b.reshape(1, d))


def _ffn_body(x_ref, xh_ref, w_in_ref, cw_ref, cb_ref, wo_ref, g_ref, b_ref,
              out_ref, xb_ref, xp_ref, hu_ref, hg_ref, act_ref, *, tm, tiles_per_seq, ff, cols):
    i = pl.program_id(0)
    d = x_ref.shape[1]
    halo = BF16_ROWS
    grp = SUBLANES
    n_grp = tm // grp
    first = (i % tiles_per_seq) == 0
    xb_ref[0:halo, :] = jnp.where(first, 0.0, xh_ref[...]).astype(BF16)
    xp = x_ref[...].reshape(grp, n_grp, d).swapaxes(0, 1).reshape(tm, d)
    xp_ref[...] = xp
    xb_ref[halo:, :] = xp.astype(BF16)
    sub = lax.broadcasted_iota(jnp.int32, (grp, cols), 0)

    def hidden(h_ref, c):
        h = jnp.dot(xb_ref[...], w_in_ref[:, c], preferred_element_type=F32)
        h_ref[halo:, :] = h[halo:]
        for back in (1, 2):
            last = h[halo + tm - back * grp:halo + tm - (back - 1) * grp]
            prev = jnp.where(sub == 0, h[halo - back:halo - back + 1], pltpu.roll(last, 1, 0))
            h_ref[halo - back * grp:halo - (back - 1) * grp, :] = prev

    def conv(h_ref, c):
        return (cw_ref[0:1, c] * h_ref[pl.ds(halo - 2 * grp, tm), :]
                + cw_ref[1:2, c] * h_ref[pl.ds(halo - grp, tm), :]
                + cw_ref[2:3, c] * h_ref[pl.ds(halo, tm), :]
                + cb_ref[:, c])

    for j in range(ff // cols):
        cu = slice(j * cols, (j + 1) * cols)
        cg = slice(ff + j * cols, ff + (j + 1) * cols)
        hu = hu_ref.at[j % 2]
        hg = hg_ref.at[j % 2]
        hidden(hu, cu)
        hidden(hg, cg)
        u = conv(hu, cu)
        gt = conv(hg, cg)
        act_ref[:, cu] = ((gt * _sigmoid(gt)) * u).astype(BF16)

    y = DEEPNORM_ALPHA * xp_ref[...] + jnp.dot(act_ref[...], wo_ref[...], preferred_element_type=F32)
    yn = _layer_norm_rows(y, g_ref[...], b_ref[...])
    out_ref[...] = yn.reshape(n_grp, grp, d).swapaxes(0, 1).reshape(tm, d)


def _conv_ffn_ln(x, seq_len, layer, w_in, conv_w, conv_b, w_out, g, b, *, tm, cols):
    m, d = x.shape
    ff = w_out.shape[1]
    halo = BF16_ROWS
    body = functools.partial(_ffn_body, tm=tm, tiles_per_seq=seq_len // tm, ff=ff, cols=cols)
    resident = lambda shape: pl.BlockSpec((None,) + shape, lambda i: (layer, 0, 0),
                                          pipeline_mode=pl.Buffered(1))
    n_layers = w_in.shape[0]
    conv_b = conv_b.reshape(n_layers, 1, 2 * ff)
    g = g.reshape(n_layers, 1, d)
    b = b.reshape(n_layers, 1, d)
    return pl.pallas_call(
        body,
        grid=(m // tm,),
        in_specs=[pl.BlockSpec((tm, d), lambda i: (i, 0)),
                  pl.BlockSpec((halo, d), lambda i: (jnp.maximum(i * (tm // halo) - 1, 0), 0)),
                  resident((d, 2 * ff)),
                  resident((3, 2 * ff)),
                  resident((1, 2 * ff)),
                  resident((ff, d)),
                  resident((1, d)),
                  resident((1, d))],
        out_specs=pl.BlockSpec((tm, d), lambda i: (i, 0)),
        out_shape=jax.ShapeDtypeStruct((m, d), F32),
        scratch_shapes=[pltpu.VMEM((tm + halo, d), BF16),
                        pltpu.VMEM((tm, d), F32),
                        pltpu.VMEM((2, tm + halo, cols), F32),
                        pltpu.VMEM((2, tm + halo, cols), F32),
                        pltpu.VMEM((tm, ff), BF16)],
        compiler_params=_cparams("parallel"),
        name="conv_ffn_ln",
    )(x, x, w_in, conv_w, conv_b, w_out, g, b)


def _head_pair_sum(v, ones_bd):
    hi = v.astype(BF16)
    lo = (v - hi.astype(F32)).astype(BF16)
    return (jnp.dot(hi, ones_bd, preferred_element_type=F32)
            + jnp.dot(lo, ones_bd, preferred_element_type=F32))


def _rwkv_proj_body(x_ref, xp_ref, mix_ref, wrkv_ref, w1_ref, w2_ref, a1_ref, a2_ref, g1_ref, g2_ref,
                    w0_ref, a0_ref, kk_ref, ka_ref,
                    r_out, ld_out, k_out, v_out, kk_out, b_out, gate_out, *, tm, tiles_per_seq):
    i = pl.program_id(0)
    d = x_ref.shape[1]
    x = x_ref[...]
    first = (i % tiles_per_seq) == 0
    prev = jnp.where(first, 0.0, xp_ref[SUBLANES - 1:SUBLANES, :])
    row = lax.broadcasted_iota(jnp.int32, (tm, d), 0)
    xs = jnp.where(row == 0, prev, pltpu.roll(x, 1, 0))
    xx = xs - x

    x_b = x.astype(BF16)
    xx_b = xx.astype(BF16)

    def mixed(j):
        return x_b + xx_b * mix_ref[j:j + 1, :].astype(BF16)

    def mm(a, w):
        return jnp.dot(a, w, preferred_element_type=F32)

    r = mm(mixed(0), wrkv_ref[0])
    k = mm(mixed(2), wrkv_ref[1])
    v = mm(mixed(3), wrkv_ref[2])
    lw = w0_ref[...] + mm(jnp.tanh(mm(mixed(1), w1_ref[...])).astype(BF16), w2_ref[...])
    softplus_neg = jnp.maximum(-lw, 0.0) + jnp.log(1.0 + jnp.exp(-jnp.abs(lw)))
    log_decay = -jnp.exp(-softplus_neg - 0.5)
    a = _sigmoid(a0_ref[...] + mm(mm(mixed(4), a1_ref[...]).astype(BF16), a2_ref[...]))
    gate = mm(_sigmoid(mm(mixed(5), g1_ref[...])).astype(BF16), g2_ref[...])

    kk = k * kk_ref[...]
    rr = lax.broadcasted_iota(jnp.int32, (LANES, LANES), 0) // RWKV_HEAD
    cc = lax.broadcasted_iota(jnp.int32, (LANES, LANES), 1) // RWKV_HEAD
    ones_bd = jnp.where(rr == cc, 1.0, 0.0).astype(BF16)
    sq = kk * kk
    ss = jnp.concatenate([_head_pair_sum(sq[:, j * LANES:(j + 1) * LANES], ones_bd)
                          for j in range(d // LANES)], axis=1)
    kk = kk / jnp.maximum(jnp.sqrt(ss), 1e-12)

    r_out[...] = r
    ld_out[...] = log_decay
    k_out[...] = k * (1.0 + (a - 1.0) * ka_ref[...])
    v_out[...] = v
    kk_out[...] = kk
    b_out[...] = kk * a
    gate_out[...] = gate


def _rwkv_proj(x, seq_len, mix, w_rkv, w0, w1, w2, a0, a1, a2, g1, g2, k_k, k_a, *, tm):
    m, d = x.shape
    full = lambda arr: pl.BlockSpec(arr.shape, lambda i: (0,) * arr.ndim, pipeline_mode=pl.Buffered(1))
    vec = lambda a: a.reshape(1, d)
    args = (mix, w_rkv.astype(BF16), w1.astype(BF16), w2.astype(BF16), a1.astype(BF16), a2.astype(BF16),
            g1.astype(BF16), g2.astype(BF16), vec(w0), vec(a0), vec(k_k), vec(k_a))
    row_spec = pl.BlockSpec((tm, d), lambda i: (i, 0))
    return pl.pallas_call(
        functools.partial(_rwkv_proj_body, tm=tm, tiles_per_seq=seq_len // tm),
        grid=(m // tm,),
        in_specs=[row_spec,
                  pl.BlockSpec((SUBLANES, d), lambda i: (jnp.maximum(i * (tm // SUBLANES) - 1, 0), 0))]
                 + [full(a) for a in args],
        out_specs=[row_spec] * 7,
        out_shape=[jax.ShapeDtypeStruct((m, d), F32)] * 7,
        compiler_params=_cparams("parallel"),
        name="rwkv_proj",
    )(x, x, *args)


def _bf16_dot(a, b, dims):
    return lax.dot_general(a.astype(BF16), b.astype(BF16), dims, preferred_element_type=F32)


def _head_sums(tiles, ones_bd):
    rows = tiles[0].shape[0]
    parts = []
    for t in tiles:
        hi = t.astype(BF16)
        parts += [hi, (t - hi.astype(F32)).astype(BF16)]
    res = jnp.dot(jnp.concatenate(parts, axis=0), ones_bd, preferred_element_type=F32)
    return [res[2 * i * rows:(2 * i + 1) * rows] + res[(2 * i + 1) * rows:(2 * i + 2) * rows]
            for i in range(len(tiles))]


def _rwkv_chunk_body(r_ref, ld_ref, k_ref, v_ref, kk_ref, b_ref, gate_ref, rk_ref, gg_ref, gb_ref,
                     o_ref, st_ref, *, chunk, pairs, subs):
    c = pl.program_id(2)

    @pl.when(c == 0)
    def _():
        st_ref[...] = jnp.zeros_like(st_ref)

    n = RWKV_HEAD
    row = lax.broadcasted_iota(jnp.int32, (chunk, LANES), 0)
    lane = lax.broadcasted_iota(jnp.int32, (chunk, LANES), 1)
    col = lane % n
    eye2 = jnp.where(row == col, 1.0, 0.0)
    head1 = lane >= n
    row2 = lax.broadcasted_iota(jnp.int32, (2 * chunk, LANES), 0)
    col2 = lax.broadcasted_iota(jnp.int32, (2 * chunk, LANES), 1) % n
    tri2 = jnp.where(row2 < chunk, row2 - 1, row2 - chunk) >= col2
    rr = lax.broadcasted_iota(jnp.int32, (LANES, LANES), 0) // n
    cc = lax.broadcasted_iota(jnp.int32, (LANES, LANES), 1) // n
    bd = rr == cc
    ones_bd = jnp.where(bd, 1.0, 0.0).astype(BF16)
    nn = (((1,), (0,)), ((), ()))

    def head_stack(t):
        return jnp.concatenate([jnp.where(head1, 0.0, t), jnp.where(head1, t, 0.0)], axis=0)

    def block_diag(t):
        return jnp.where(bd, jnp.concatenate([t, t], axis=0), 0.0)

    units = [(cc, p) for cc in range(subs) for p in range(pairs)]
    U = range(len(units))
    rows = [slice(cc * chunk, (cc + 1) * chunk) for cc, _ in units]
    sls = [slice(p * LANES, (p + 1) * LANES) for _, p in units]
    r = [r_ref[0, rows[i], sls[i]] for i in U]
    ld = [ld_ref[0, rows[i], sls[i]] for i in U]
    k = [k_ref[0, rows[i], sls[i]] for i in U]
    v = [v_ref[0, rows[i], sls[i]] for i in U]
    bv = [b_ref[0, rows[i], sls[i]] for i in U]

    cum = list(ld)
    shift = 1
    while shift < chunk:
        cum = [cu + jnp.where(row >= shift, pltpu.roll(cu, shift, 0), 0.0) for cu in cum]
        shift *= 2
    cum_last = [cu[chunk - 1:chunk, :] for cu in cum]
    e_inv = [jnp.exp(-cu) for cu in cum]
    r_hat = [r[i] * jnp.exp(cum[i]) for i in U]
    a_hat = [-(kk_ref[0, rows[i], sls[i]] * jnp.exp(cum[i] - ld[i])) for i in U]
    lhs = [jnp.concatenate([a_hat[i], r_hat[i]], axis=0) for i in U]
    sbk = [_bf16_dot(lhs[i], jnp.concatenate([head_stack(bv[i] * e_inv[i]), head_stack(k[i] * e_inv[i])], axis=0),
                     _NT) for i in U]
    sb = [jnp.where(tri2, t[:, :LANES], 0.0) for t in sbk]
    sk = [jnp.where(tri2, t[:, LANES:], 0.0) for t in sbk]
    a_ab = [t[:chunk] for t in sb]
    a_rb = [t[chunk:] for t in sb]

    xp = [_bf16_dot(t, block_diag(t), nn) for t in a_ab]
    tinv = [eye2 + t for t in a_ab]
    power = 2
    while 2 * power < chunk:
        both = [_bf16_dot(jnp.concatenate([xp[i], tinv[i]], axis=0), block_diag(xp[i]), nn) for i in U]
        xp = [t[:chunk] for t in both]
        tinv = [tinv[i] + both[i][chunk:] for i in U]
        power *= 2
    tinv = [tinv[i] + _bf16_dot(tinv[i], block_diag(xp[i]), nn) for i in U]

    state = [st_ref[p] for p in range(pairs)]
    y = [None] * len(units)
    for cc in range(subs):
        ids = [cc * pairs + p for p in range(pairs)]
        zy = [_bf16_dot(jnp.concatenate([lhs[i], sk[i]], axis=1),
                        jnp.concatenate([state[p].T, head_stack(v[i])], axis=0), nn)
              for p, i in enumerate(ids)]
        u = [_bf16_dot(tinv[i], head_stack(zy[p][:chunk]), nn) for p, i in enumerate(ids)]
        for p, i in enumerate(ids):
            y[i] = zy[p][chunk:] + _bf16_dot(a_rb[i], head_stack(u[p]), nn)
            e_last = jnp.exp(cum_last[i] - cum[i])
            s_new = state[p] * jnp.exp(cum_last[i]) + _bf16_dot(
                jnp.concatenate([u[p], v[i]], axis=0),
                jnp.concatenate([bv[i] * e_last, k[i] * e_last], axis=0), _TN)
            state[p] = jnp.where(bd, s_new, 0.0)
    for p in range(pairs):
        st_ref[p] = state[p]

    sums = _head_sums(y + [r[i] * k[i] * rk_ref[:, sls[i]] for i in U], ones_bd)
    dlt = [y[i] - sums[i] * (1.0 / n) for i in U]
    var = _head_sums([t * t for t in dlt], ones_bd)
    for i in U:
        yn = dlt[i] * lax.rsqrt(var[i] * (1.0 / n) + RWKV_GN_EPS) * gg_ref[:, sls[i]] + gb_ref[:, sls[i]]
        bonus = sums[len(units) + i] * v[i]
        o_ref[0, rows[i], sls[i]] = ((yn + bonus) * gate_ref[0, rows[i], sls[i]]).astype(o_ref.dtype)


def _rwkv_chunk(r, ld, k, v, kk, bvec, gate, r_k, gn_g, gn_b, *, pairs, subs):
    bsz, t_len, d = r.shape
    chunk = RWKV_HEAD
    width = pairs * LANES
    seq_spec = pl.BlockSpec((1, subs * chunk, width), lambda b, g, c: (b, c, g))
    vec_spec = pl.BlockSpec((1, width), lambda b, g, c: (0, g))
    return pl.pallas_call(
        functools.partial(_rwkv_chunk_body, chunk=chunk, pairs=pairs, subs=subs),
        grid=(bsz, d // width, t_len // (subs * chunk)),
        in_specs=[seq_spec] * 7 + [vec_spec] * 3,
        out_specs=seq_spec,
        out_shape=jax.ShapeDtypeStruct((bsz, t_len, d), BF16),
        scratch_shapes=[pltpu.VMEM((pairs, LANES, LANES), F32)],
        compiler_params=_cparams("parallel", "parallel", "arbitrary"),
        name="rwkv_chunk",
    )(r, ld, k, v, kk, bvec, gate, r_k.reshape(1, d), gn_g.reshape(1, d), gn_b.reshape(1, d))


def _rwkv_mixer(x, bsz, t_len, mix, w_rkv, w0, w1, w2, a0, a1, a2, g1, g2, k_k, k_a, r_k, gn_g, gn_b, w_o,
                ln_g, ln_b):
    d = x.shape[1]
    outs = _rwkv_proj(x, t_len, mix, w_rkv, w0, w1, w2, a0, a1, a2, g1, g2, k_k, k_a, tm=RWKV_PROJ_ROWS)
    seq = [o.reshape(bsz, t_len, d) for o in outs]
    o = _rwkv_chunk(*seq, r_k, gn_g, gn_b, pairs=RWKV_HEAD_PAIRS, subs=RWKV_SUBCHUNKS)
    return _matmul_residual_ln(o.reshape(bsz * t_len, d), w_o.astype(BF16), x, ln_g, ln_b, tm=OUT_PROJ_ROWS)


def _ret_chunk_body(q_ref, k_ref, v_ref, gate_ref, cos_ref, sin_ref, g_ref, b_ref, o_ref, st_ref, *, chunk):
    c = pl.program_id(1)

    @pl.when(c == 0)
    def _():
        st_ref[...] = jnp.zeros_like(st_ref)

    n_heads, dk, dv = st_ref.shape
    half = dk // 2
    cos = cos_ref[...]
    sin = sin_ref[...]

    def rotate(t):
        te, to = t[:, :half], t[:, half:]
        return jnp.concatenate([te * cos - to * sin, to * cos + te * sin], axis=1)

    ri = lax.broadcasted_iota(jnp.int32, (chunk, chunk), 0)
    ci = lax.broadcasted_iota(jnp.int32, (chunk, chunk), 1)
    rel = (ri - ci).astype(F32)
    pos = lax.broadcasted_iota(jnp.int32, (chunk, dk), 0).astype(F32)

    for h in range(n_heads):
        log_gamma = math.log(1.0 - 2.0 ** (-5.0 - h))
        inner = jnp.where(rel >= 0, jnp.exp(log_gamma * jnp.maximum(rel, 0.0)), 0.0)
        cross = jnp.exp(log_gamma * (pos + 1.0))
        sdecay = jnp.exp(log_gamma * (chunk - 1.0 - pos))
        chunk_decay = jnp.exp(log_gamma * chunk)

        q = rotate(q_ref[0, :, h * dk:(h + 1) * dk].astype(F32))
        k = rotate(k_ref[0, :, h * dk:(h + 1) * dk].astype(F32)) * (dk ** -0.5)
        v = v_ref[0, :, h * dv:(h + 1) * dv].astype(BF16)
        s = lax.dot_general(q.astype(BF16), k.astype(BF16), _NT, preferred_element_type=F32) * inner
        st = st_ref[h]
        o = (jnp.dot(s.astype(BF16), v, preferred_element_type=F32)
             + jnp.dot((q * cross).astype(BF16), st.astype(BF16), preferred_element_type=F32))
        st_ref[h] = st * chunk_decay + lax.dot_general((k * sdecay).astype(BF16), v, _TN,
                                                       preferred_element_type=F32)

        mu = jnp.mean(o, axis=-1, keepdims=True)
        dlt = o - mu
        var = jnp.mean(dlt * dlt, axis=-1, keepdims=True)
        vs = slice(h * dv, (h + 1) * dv)
        on = dlt * lax.rsqrt(var + RET_GN_EPS) * g_ref[:, vs] + b_ref[:, vs]
        gt = gate_ref[0, :, vs].astype(F32)
        o_ref[0, :, vs] = (gt * _sigmoid(gt) * on).astype(o_ref.dtype)


def _ret_chunk(proj, cos, sin, gn_g, gn_b, *, chunk):
    bsz, t_len, six_d = proj.shape
    d = six_d // 6
    h = RET_HEADS
    dk, dv = d // h, 2 * d // h
    return pl.pallas_call(
        functools.partial(_ret_chunk_body, chunk=chunk),
        grid=(bsz, t_len // chunk),
        in_specs=[pl.BlockSpec((1, chunk, d), lambda b, c: (b, c, 0)),
                  pl.BlockSpec((1, chunk, d), lambda b, c: (b, c, 1)),
                  pl.BlockSpec((1, chunk, 2 * d), lambda b, c: (b, c, 1)),
                  pl.BlockSpec((1, chunk, 2 * d), lambda b, c: (b, c, 2)),
                  pl.BlockSpec((chunk, dk // 2), lambda b, c: (c, 0)),
                  pl.BlockSpec((chunk, dk // 2), lambda b, c: (c, 0)),
                  pl.BlockSpec((1, 2 * d), lambda b, c: (0, 0)),
                  pl.BlockSpec((1, 2 * d), lambda b, c: (0, 0))],
        out_specs=pl.BlockSpec((1, chunk, 2 * d), lambda b, c: (b, c, 0)),
        out_shape=jax.ShapeDtypeStruct((bsz, t_len, 2 * d), BF16),
        scratch_shapes=[pltpu.VMEM((h, dk, dv), F32)],
        compiler_params=_cparams("parallel", "arbitrary"),
        name="retention_chunk",
    )(proj, proj, proj, proj, cos, sin, gn_g.reshape(1, 2 * d), gn_b.reshape(1, 2 * d))


def _retention_mixer(x, bsz, t_len, w_in, gn_g, gn_b, w_o, ln_g, ln_b):
    d = x.shape[1]
    h = RET_HEADS
    dk = d // h
    w_qk = w_in[:, :2 * d].reshape(d, 2 * h, dk // 2, 2).transpose(0, 1, 3, 2).reshape(d, 2 * d)
    w_perm = jnp.concatenate([w_qk, w_in[:, 2 * d:]], axis=1).astype(BF16)
    proj = _matmul(x, w_perm, tm=PROJ_ROWS, tn=PROJ_COLS, out_dtype=BF16)
    inv = 1.0 / (RET_ROPE_BASE ** jnp.linspace(0.0, 1.0, dk // 2, dtype=F32))
    ang = jnp.arange(t_len, dtype=F32)[:, None] * inv[None, :]
    o = _ret_chunk(proj.reshape(bsz, t_len, 6 * d), jnp.cos(ang), jnp.sin(ang), gn_g, gn_b, chunk=RET_CHUNK)
    return _matmul_residual_ln(o.reshape(bsz * t_len, 2 * d), w_o.astype(BF16), x, ln_g, ln_b,
                               tm=OUT_PROJ_ROWS)


def _moba_body(qt_ref, vt_ref, k_ref, o_ref, kmean_ref, va_ref, bias_ref, *, n_blk, pairs):
    qb = pl.program_id(2)
    blk = MOBA_BLOCK
    hd = MOBA_HEAD_DIM
    heads = 2 * pairs
    aug = hd + BF16_ROWS
    t_len = n_blk * blk
    H = range(heads)

    @pl.when(qb == 0)
    def _():
        kf = k_ref[...].astype(F32)
        kmean_ref[...] = jnp.mean(kf.reshape(n_blk, blk, pairs * LANES), axis=1)
        ones = jnp.ones((BF16_ROWS, t_len), BF16)
        for h in H:
            va_ref[h, 0:hd, :] = vt_ref[h * hd:(h + 1) * hd, :]
            va_ref[h, hd:aug, :] = ones

    kmean = kmean_ref[...]
    qt = qt_ref[...].astype(F32)
    zeros = jnp.zeros((hd, blk), F32)
    blk_id = lax.broadcasted_iota(jnp.int32, (n_blk, blk), 0)
    scale = hd ** -0.5

    qz_b = []
    for h in H:
        qh = qt[h * hd:(h + 1) * hd]
        qz = jnp.concatenate([qh, zeros] if h % 2 == 0 else [zeros, qh], axis=0)
        lanes = slice((h // 2) * LANES, (h // 2 + 1) * LANES)
        gate = jnp.dot(kmean[:, lanes], qz, precision=HIGHEST, preferred_element_type=F32)
        beaten = jnp.zeros((n_blk, blk), F32)
        for m in range(n_blk):
            gm = gate[m:m + 1, :]
            wins = jnp.where(gm > gate, 1.0, jnp.where(gm == gate, jnp.where(blk_id > m, 1.0, 0.0), 0.0))
            beaten = beaten + jnp.where(m < qb, wins, 0.0)
        bias_ref[h] = jnp.where(blk_id < qb, jnp.where(beaten < MOBA_TOPK, 0.0, NEG_BIG), NEG_BIG)
        qz_b.append((qz * (scale * LOG2_E)).astype(BF16))

    def scores(off):
        kb = k_ref[pl.ds(off, blk), :]
        return tuple(jnp.dot(kb[:, (h // 2) * LANES:(h // 2 + 1) * LANES], qz_b[h],
                             preferred_element_type=F32) for h in H)

    def accumulate(off, s, m_new, shift, carry):
        p = [jnp.exp2(s[h] - shift[h]).astype(BF16) for h in H]
        alpha = [jnp.exp2(carry[h][0] - m_new[h]) for h in H]
        acc = [alpha[h] * carry[h][1]
               + jnp.dot(va_ref[h, :, pl.ds(off, blk)], p[h], preferred_element_type=F32) for h in H]
        return tuple((m_new[h], acc[h]) for h in H)

    def attend(off, s, carry):
        m_new = [jnp.maximum(carry[h][0], jnp.max(s[h], axis=0, keepdims=True)) for h in H]
        return accumulate(off, s, m_new, m_new, carry)

    def past_block(nb, carry):
        off = pl.multiple_of(nb * blk, blk)
        s = scores(off)
        chosen = [bias_ref[h, pl.ds(nb, 1), :] >= 0.0 for h in H]
        m_new = [jnp.where(chosen[h], jnp.maximum(carry[h][0], jnp.max(s[h], axis=0, keepdims=True)),
                           carry[h][0]) for h in H]
        shift = [jnp.where(chosen[h], m_new[h], -NEG_BIG) for h in H]
        return accumulate(off, s, m_new, shift, carry)

    init = tuple((jnp.full((1, blk), NEG_BIG, F32), jnp.zeros((aug, blk), F32)) for _ in H)
    carry = lax.fori_loop(0, qb, past_block, init)
    key_i = lax.broadcasted_iota(jnp.int32, (blk, blk), 0)
    qry_i = lax.broadcasted_iota(jnp.int32, (blk, blk), 1)
    causal_bias = jnp.where(key_i <= qry_i, 0.0, NEG_BIG)
    own = pl.multiple_of(qb * blk, blk)
    s_own = scores(own)
    res = attend(own, [s_own[h] + causal_bias for h in H], carry)
    for h in H:
        acc = res[h][1]
        o_ref[h * hd:(h + 1) * hd, :] = acc[:hd] / acc[hd:hd + 1]


def _moba_attention(qvt, k, bsz, t_len, *, pairs):
    d = k.shape[1]
    blk = MOBA_BLOCK
    n_blk = t_len // blk
    width = pairs * LANES
    n_grp = d // width
    heads = 2 * pairs
    return pl.pallas_call(
        functools.partial(_moba_body, n_blk=n_blk, pairs=pairs),
        grid=(bsz, n_grp, n_blk),
        in_specs=[pl.BlockSpec((width, blk), lambda b, p, q: (p, b * n_blk + q)),
                  pl.BlockSpec((width, t_len), lambda b, p, q: (n_grp + p, b)),
                  pl.BlockSpec((t_len, width), lambda b, p, q: (b, p))],
        out_specs=pl.BlockSpec((width, blk), lambda b, p, q: (p, b * n_blk + q)),
        out_shape=jax.ShapeDtypeStruct((d, bsz * t_len), F32),
        scratch_shapes=[pltpu.VMEM((n_blk, width), F32),
                        pltpu.VMEM((heads, MOBA_HEAD_DIM + BF16_ROWS, t_len), BF16),
                        pltpu.VMEM((heads, n_blk, blk), F32)],
        compiler_params=_cparams("parallel", "parallel", "arbitrary"),
        name="moba_attention",
    )(qvt, qvt, k)


def _moba_mixer(x, bsz, t_len, w_qkv, w_o, ln_g, ln_b):
    d = x.shape[1]
    assert t_len % MOBA_BLOCK == 0 and d == MOBA_HEADS * MOBA_HEAD_DIM
    w_k = w_qkv[:, d:2 * d].astype(BF16)
    w_qv_t = jnp.concatenate([w_qkv[:, :d], w_qkv[:, 2 * d:]], axis=1).T.astype(BF16)
    k = _matmul(x, w_k, tm=PROJ_ROWS, tn=PROJ_COLS, out_dtype=BF16)
    qvt = _matmul_nt(x, w_qv_t, tm=PROJ_ROWS, tn=PROJ_COLS, out_dtype=BF16)
    ot = _moba_attention(qvt, k, bsz, t_len, pairs=MOBA_HEAD_PAIRS)
    return _matmul_residual_ln(ot, w_o.astype(BF16), x, ln_g, ln_b, tm=OUT_PROJ_ROWS, lhs_transposed=True)


def kernel(x, rwkv_mix, rwkv_w_rkv, rwkv_w0, rwkv_w1, rwkv_w2, rwkv_a0, rwkv_a1, rwkv_a2, rwkv_g1, rwkv_g2,
           rwkv_k_k, rwkv_k_a, rwkv_r_k, rwkv_gn_g, rwkv_gn_b, rwkv_w_o, ret_w_in, ret_gn_g, ret_gn_b, ret_w_o,
           moba_w_qkv, moba_w_o, ffn_w_in, ffn_conv_w, ffn_conv_b, ffn_w_out, ln1_g, ln1_b, ln2_g, ln2_b):
    bsz, t_len, d = x.shape
    h = x.reshape(bsz * t_len, d)
    ffn_w_in_b = ffn_w_in.astype(BF16)
    ffn_w_out_b = ffn_w_out.astype(BF16)
    for i in range(DEPTH):
        kind, j = i % N_MIXERS, i // N_MIXERS
        if kind == 0:
            h = _rwkv_mixer(h, bsz, t_len, rwkv_mix[j], rwkv_w_rkv[j], rwkv_w0[j], rwkv_w1[j], rwkv_w2[j],
                            rwkv_a0[j], rwkv_a1[j], rwkv_a2[j], rwkv_g1[j], rwkv_g2[j], rwkv_k_k[j],
                            rwkv_k_a[j], rwkv_r_k[j], rwkv_gn_g[j], rwkv_gn_b[j], rwkv_w_o[j],
                            ln1_g[i], ln1_b[i])
        elif kind == 1:
            h = _retention_mixer(h, bsz, t_len, ret_w_in[j], ret_gn_g[j], ret_gn_b[j], ret_w_o[j],
                                 ln1_g[i], ln1_b[i])
        else:
            h = _moba_mixer(h, bsz, t_len, moba_w_qkv[j], moba_w_o[j], ln1_g[i], ln1_b[i])
        h = _conv_ffn_ln(h, t_len, i, ffn_w_in_b, ffn_conv_w, ffn_conv_b, ffn_w_out_b, ln2_g, ln2_b,
                         tm=FFN_ROWS, cols=FFN_COL_SLICE)
    return h.reshape(bsz, t_len, d)
```

```python
import functools
import math

import jax
import jax.numpy as jnp
from jax import lax
from jax.experimental import pallas as pl
from jax.experimental.pallas import tpu as pltpu

F32 = jnp.float32
BF16 = jnp.bfloat16
HIGHEST = lax.Precision.HIGHEST

DEPTH = 4
N_MIXERS = 3
RWKV_HEAD = 64
RWKV_GN_EPS = 64e-5
RET_HEADS = 4
RET_ROPE_BASE = 10000.0
RET_GN_EPS = 1e-5
MOBA_HEADS = 16
MOBA_HEAD_DIM = 64
MOBA_BLOCK = 256
MOBA_TOPK = 3
LN_EPS = 1e-5
DEEPNORM_ALPHA = (2 * DEPTH) ** 0.25

LANES = 128
SUBLANES = 8
BF16_ROWS = 16
VMEM_LIMIT_BYTES = 52 * 1024 * 1024

NEG_BIG = -1e30
LOG2_E = math.log2(math.e)

PROJ_ROWS = 512
PROJ_COLS = 512
OUT_PROJ_ROWS = 1024
FFN_ROWS = 512
FFN_COL_SLICE = 256
RWKV_PROJ_ROWS = 512
RWKV_HEAD_PAIRS = 8
RWKV_SUBCHUNKS = 2
RET_CHUNK = 256
MOBA_HEAD_PAIRS = 4

_NT = (((1,), (1,)), ((), ()))
_TN = (((0,), (0,)), ((), ()))


def _cparams(*sem):
    return pltpu.CompilerParams(dimension_semantics=sem, vmem_limit_bytes=VMEM_LIMIT_BYTES)


def _sigmoid(x):
    return 1.0 / (1.0 + jnp.exp(-x))


def _layer_norm_rows(y, g, b):
    mu = jnp.mean(y, axis=-1, keepdims=True)
    d = y - mu
    var = jnp.mean(d * d, axis=-1, keepdims=True)
    return d * lax.rsqrt(var + LN_EPS) * g + b


def _mm_body(x_ref, w_ref, o_ref, xb_ref, *, tn, transposed_out):
    xb_ref[...] = x_ref[...].astype(BF16)
    n = w_ref.shape[0] if transposed_out else w_ref.shape[1]
    for c in range(n // tn):
        sl = slice(c * tn, (c + 1) * tn)
        if transposed_out:
            o_ref[sl, :] = lax.dot_general(w_ref[sl, :], xb_ref[...], _NT,
                                           preferred_element_type=F32).astype(o_ref.dtype)
        else:
            o_ref[:, sl] = jnp.dot(xb_ref[...], w_ref[:, sl], preferred_element_type=F32).astype(o_ref.dtype)


def _matmul(x, w, *, tm, tn, out_dtype=F32):
    m, k = x.shape
    n = w.shape[1]
    return pl.pallas_call(
        functools.partial(_mm_body, tn=tn, transposed_out=False),
        grid=(m // tm,),
        in_specs=[pl.BlockSpec((tm, k), lambda i: (i, 0)),
                  pl.BlockSpec((k, n), lambda i: (0, 0), pipeline_mode=pl.Buffered(1))],
        out_specs=pl.BlockSpec((tm, n), lambda i: (i, 0)),
        out_shape=jax.ShapeDtypeStruct((m, n), out_dtype),
        scratch_shapes=[pltpu.VMEM((tm, k), BF16)],
        compiler_params=_cparams("parallel"),
        name="matmul",
    )(x, w)


def _matmul_nt(x, wt, *, tm, tn, out_dtype=F32):
    m, k = x.shape
    n = wt.shape[0]
    return pl.pallas_call(
        functools.partial(_mm_body, tn=tn, transposed_out=True),
        grid=(m // tm,),
        in_specs=[pl.BlockSpec((tm, k), lambda i: (i, 0)),
                  pl.BlockSpec((n, k), lambda i: (0, 0), pipeline_mode=pl.Buffered(1))],
        out_specs=pl.BlockSpec((n, tm), lambda i: (0, i)),
        out_shape=jax.ShapeDtypeStruct((n, m), out_dtype),
        scratch_shapes=[pltpu.VMEM((tm, k), BF16)],
        compiler_params=_cparams("parallel"),
        name="matmul_nt",
    )(x, wt)


def _mm_res_ln_body(o_ref, w_ref, x_ref, g_ref, b_ref, out_ref, *, lhs_transposed):
    lhs = o_ref[...].astype(BF16)
    if lhs_transposed:
        acc = lax.dot_general(lhs, w_ref[...], _TN, preferred_element_type=F32)
    else:
        acc = jnp.dot(lhs, w_ref[...], preferred_element_type=F32)
    y = DEEPNORM_ALPHA * x_ref[...] + acc
    out_ref[...] = _layer_norm_rows(y, g_ref[...], b_ref[...])


def _matmul_residual_ln(o, w, x, g, b, *, tm, lhs_transposed=False):
    m, d = x.shape
    k = w.shape[0]
    if lhs_transposed:
        o_spec = pl.BlockSpec((k, tm), lambda i: (0, i))
    else:
        o_spec = pl.BlockSpec((tm, k), lambda i: (i, 0))
    return pl.pallas_call(
        functools.partial(_mm_res_ln_body, lhs_transposed=lhs_transposed),
        grid=(m // tm,),
        in_specs=[o_spec,
                  pl.BlockSpec((k, d), lambda i: (0, 0)),
                  pl.BlockSpec((tm, d), lambda i: (i, 0)),
                  pl.BlockSpec((1, d), lambda i: (0, 0)),
                  pl.BlockSpec((1, d), lambda i: (0, 0))],
        out_specs=pl.BlockSpec((tm, d), lambda i: (i, 0)),
        out_shape=jax.ShapeDtypeStruct((m, d), F32),
        compiler_params=_cparams("parallel"),
        name="out_proj_residual_ln",
    )(o, w, x, g.reshape(1, d), b.reshape(1, d))


def _ffn_body(x_ref, xh_ref, w_in_ref, cw_ref, cb_ref, wo_ref, g_ref, b_ref,
              out_ref, xb_ref, xp_ref, hu_ref, hg_ref, act_ref, *, tm, tiles_per_seq, ff, cols):
    i = pl.program_id(0)
    d = x_ref.shape[1]
    halo = BF16_ROWS
    grp = SUBLANES
    n_grp = tm // grp
    first = (i % tiles_per_seq) == 0
    xb_ref[0:halo, :] = jnp.where(first, 0.0, xh_ref[...]).astype(BF16)
    xp = x_ref[...].reshape(grp, n_grp, d).swapaxes(0, 1).reshape(tm, d)
    xp_ref[...] = xp
    xb_ref[halo:, :] = xp.astype(BF16)
    sub = lax.broadcasted_iota(jnp.int32, (grp, cols), 0)

    def hidden(h_ref, c):
        h = jnp.dot(xb_ref[...], w_in_ref[:, c], preferred_element_type=F32)
        h_ref[halo:, :] = h[halo:]
        for back in (1, 2):
            last = h[halo + tm - back * grp:halo + tm - (back - 1) * grp]
            prev = jnp.where(sub == 0, h[halo - back:halo - back + 1], pltpu.roll(last, 1, 0))
            h_ref[halo - back * grp:halo - (back - 1) * grp, :] = prev

    def conv(h_ref, c):
        return (cw_ref[0:1, c] * h_ref[pl.ds(halo - 2 * grp, tm), :]
                + cw_ref[1:2, c] * h_ref[pl.ds(halo - grp, tm), :]
                + cw_ref[2:3, c] * h_ref[pl.ds(halo, tm), :]
                + cb_ref[:, c])

    for j in range(ff // cols):
        cu = slice(j * cols, (j + 1) * cols)
        cg = slice(ff + j * cols, ff + (j + 1) * cols)
        hu = hu_ref.at[j % 2]
        hg = hg_ref.at[j % 2]
        hidden(hu, cu)
        hidden(hg, cg)
        u = conv(hu, cu)
        gt = conv(hg, cg)
        act_ref[:, cu] = ((gt * _sigmoid(gt)) * u).astype(BF16)

    y = DEEPNORM_ALPHA * xp_ref[...] + jnp.dot(act_ref[...], wo_ref[...], preferred_element_type=F32)
    yn = _layer_norm_rows(y, g_ref[...], b_ref[...])
    out_ref[...] = yn.reshape(n_grp, grp, d).swapaxes(0, 1).reshape(tm, d)


def _conv_ffn_ln(x, seq_len, layer, w_in, conv_w, conv_b, w_out, g, b, *, tm, cols):
    m, d = x.shape
    ff = w_out.shape[1]
    halo = BF16_ROWS
    body = functools.partial(_ffn_body, tm=tm, tiles_per_seq=seq_len // tm, ff=ff, cols=cols)
    resident = lambda shape: pl.BlockSpec((None,) + shape, lambda i: (layer, 0, 0),
                                          pipeline_mode=pl.Buffered(1))
    n_layers = w_in.shape[0]
    conv_b = conv_b.reshape(n_layers, 1, 2 * ff)
    g = g.reshape(n_layers, 1, d)
    b = b.reshape(n_layers, 1, d)
    return pl.pallas_call(
        body,
        grid=(m // tm,),
        in_specs=[pl.BlockSpec((tm, d), lambda i: (i, 0)),
                  pl.BlockSpec((halo, d), lambda i: (jnp.maximum(i * (tm // halo) - 1, 0), 0)),
                  resident((d, 2 * ff)),
                  resident((3, 2 * ff)),
                  resident((1, 2 * ff)),
                  resident((ff, d)),
                  resident((1, d)),
                  resident((1, d))],
        out_specs=pl.BlockSpec((tm, d), lambda i: (i, 0)),
        out_shape=jax.ShapeDtypeStruct((m, d), F32),
        scratch_shapes=[pltpu.VMEM((tm + halo, d), BF16),
                        pltpu.VMEM((tm, d), F32),
                        pltpu.VMEM((2, tm + halo, cols), F32),
                        pltpu.VMEM((2, tm + halo, cols), F32),
                        pltpu.VMEM((tm, ff), BF16)],
        compiler_params=_cparams("parallel"),
        name="conv_ffn_ln",
    )(x, x, w_in, conv_w, conv_b, w_out, g, b)


def _head_pair_sum(v, ones_bd):
    hi = v.astype(BF16)
    lo = (v - hi.astype(F32)).astype(BF16)
    return (jnp.dot(hi, ones_bd, preferred_element_type=F32)
            + jnp.dot(lo, ones_bd, preferred_element_type=F32))


def _rwkv_proj_body(x_ref, xp_ref, mix_ref, wrkv_ref, w1_ref, w2_ref, a1_ref, a2_ref, g1_ref, g2_ref,
                    w0_ref, a0_ref, kk_ref, ka_ref,
                    r_out, ld_out, k_out, v_out, kk_out, b_out, gate_out, *, tm, tiles_per_seq):
    i = pl.program_id(0)
    d = x_ref.shape[1]
    x = x_ref[...]
    first = (i % tiles_per_seq) == 0
    prev = jnp.where(first, 0.0, xp_ref[SUBLANES - 1:SUBLANES, :])
    row = lax.broadcasted_iota(jnp.int32, (tm, d), 0)
    xs = jnp.where(row == 0, prev, pltpu.roll(x, 1, 0))
    xx = xs - x

    x_b = x.astype(BF16)
    xx_b = xx.astype(BF16)

    def mixed(j):
        return x_b + xx_b * mix_ref[j:j + 1, :].astype(BF16)

    def mm(a, w):
        return jnp.dot(a, w, preferred_element_type=F32)

    r = mm(mixed(0), wrkv_ref[0])
    k = mm(mixed(2), wrkv_ref[1])
    v = mm(mixed(3), wrkv_ref[2])
    lw = w0_ref[...] + mm(jnp.tanh(mm(mixed(1), w1_ref[...])).astype(BF16), w2_ref[...])
    softplus_neg = jnp.maximum(-lw, 0.0) + jnp.log(1.0 + jnp.exp(-jnp.abs(lw)))
    log_decay = -jnp.exp(-softplus_neg - 0.5)
    a = _sigmoid(a0_ref[...] + mm(mm(mixed(4), a1_ref[...]).astype(BF16), a2_ref[...]))
    gate = mm(_sigmoid(mm(mixed(5), g1_ref[...])).astype(BF16), g2_ref[...])

    kk = k * kk_ref[...]
    rr = lax.broadcasted_iota(jnp.int32, (LANES, LANES), 0) // RWKV_HEAD
    cc = lax.broadcasted_iota(jnp.int32, (LANES, LANES), 1) // RWKV_HEAD
    ones_bd = jnp.where(rr == cc, 1.0, 0.0).astype(BF16)
    sq = kk * kk
    ss = jnp.concatenate([_head_pair_sum(sq[:, j * LANES:(j + 1) * LANES], ones_bd)
                          for j in range(d // LANES)], axis=1)
    kk = kk / jnp.maximum(jnp.sqrt(ss), 1e-12)

    r_out[...] = r
    ld_out[...] = log_decay
    k_out[...] = k * (1.0 + (a - 1.0) * ka_ref[...])
    v_out[...] = v
    kk_out[...] = kk
    b_out[...] = kk * a
    gate_out[...] = gate


def _rwkv_proj(x, seq_len, mix, w_rkv, w0, w1, w2, a0, a1, a2, g1, g2, k_k, k_a, *, tm):
    m, d = x.shape
    full = lambda arr: pl.BlockSpec(arr.shape, lambda i: (0,) * arr.ndim, pipeline_mode=pl.Buffered(1))
    vec = lambda a: a.reshape(1, d)
    args = (mix, w_rkv.astype(BF16), w1.astype(BF16), w2.astype(BF16), a1.astype(BF16), a2.astype(BF16),
            g1.astype(BF16), g2.astype(BF16), vec(w0), vec(a0), vec(k_k), vec(k_a))
    row_spec = pl.BlockSpec((tm, d), lambda i: (i, 0))
    return pl.pallas_call(
        functools.partial(_rwkv_proj_body, tm=tm, tiles_per_seq=seq_len // tm),
        grid=(m // tm,),
        in_specs=[row_spec,
                  pl.BlockSpec((SUBLANES, d), lambda i: (jnp.maximum(i * (tm // SUBLANES) - 1, 0), 0))]
                 + [full(a) for a in args],
        out_specs=[row_spec] * 7,
        out_shape=[jax.ShapeDtypeStruct((m, d), F32)] * 7,
        compiler_params=_cparams("parallel"),
        name="rwkv_proj",
    )(x, x, *args)


def _bf16_dot(a, b, dims):
    return lax.dot_general(a.astype(BF16), b.astype(BF16), dims, preferred_element_type=F32)


def _head_sums(tiles, ones_bd):
    rows = tiles[0].shape[0]
    parts = []
    for t in tiles:
        hi = t.astype(BF16)
        parts += [hi, (t - hi.astype(F32)).astype(BF16)]
    res = jnp.dot(jnp.concatenate(parts, axis=0), ones_bd, preferred_element_type=F32)
    return [res[2 * i * rows:(2 * i + 1) * rows] + res[(2 * i + 1) * rows:(2 * i + 2) * rows]
            for i in range(len(tiles))]


def _rwkv_chunk_body(r_ref, ld_ref, k_ref, v_ref, kk_ref, b_ref, gate_ref, rk_ref, gg_ref, gb_ref,
                     o_ref, st_ref, *, chunk, pairs, subs):
    c = pl.program_id(2)

    @pl.when(c == 0)
    def _():
        st_ref[...] = jnp.zeros_like(st_ref)

    n = RWKV_HEAD
    row = lax.broadcasted_iota(jnp.int32, (chunk, LANES), 0)
    lane = lax.broadcasted_iota(jnp.int32, (chunk, LANES), 1)
    col = lane % n
    eye2 = jnp.where(row == col, 1.0, 0.0)
    head1 = lane >= n
    row2 = lax.broadcasted_iota(jnp.int32, (2 * chunk, LANES), 0)
    col2 = lax.broadcasted_iota(jnp.int32, (2 * chunk, LANES), 1) % n
    tri2 = jnp.where(row2 < chunk, row2 - 1, row2 - chunk) >= col2
    rr = lax.broadcasted_iota(jnp.int32, (LANES, LANES), 0) // n
    cc = lax.broadcasted_iota(jnp.int32, (LANES, LANES), 1) // n
    bd = rr == cc
    ones_bd = jnp.where(bd, 1.0, 0.0).astype(BF16)
    nn = (((1,), (0,)), ((), ()))

    def head_stack(t):
        return jnp.concatenate([jnp.where(head1, 0.0, t), jnp.where(head1, t, 0.0)], axis=0)

    def block_diag(t):
        return jnp.where(bd, jnp.concatenate([t, t], axis=0), 0.0)

    units = [(cc, p) for cc in range(subs) for p in range(pairs)]
    U = range(len(units))
    rows = [slice(cc * chunk, (cc + 1) * chunk) for cc, _ in units]
    sls = [slice(p * LANES, (p + 1) * LANES) for _, p in units]
    r = [r_ref[0, rows[i], sls[i]] for i in U]
    ld = [ld_ref[0, rows[i], sls[i]] for i in U]
    k = [k_ref[0, rows[i], sls[i]] for i in U]
    v = [v_ref[0, rows[i], sls[i]] for i in U]
    bv = [b_ref[0, rows[i], sls[i]] for i in U]

    cum = list(ld)
    shift = 1
    while shift < chunk:
        cum = [cu + jnp.where(row >= shift, pltpu.roll(cu, shift, 0), 0.0) for cu in cum]
        shift *= 2
    cum_last = [cu[chunk - 1:chunk, :] for cu in cum]
    e_inv = [jnp.exp(-cu) for cu in cum]
    r_hat = [r[i] * jnp.exp(cum[i]) for i in U]
    a_hat = [-(kk_ref[0, rows[i], sls[i]] * jnp.exp(cum[i] - ld[i])) for i in U]
    lhs = [jnp.concatenate([a_hat[i], r_hat[i]], axis=0) for i in U]
    sbk = [_bf16_dot(lhs[i], jnp.concatenate([head_stack(bv[i] * e_inv[i]), head_stack(k[i] * e_inv[i])], axis=0),
                     _NT) for i in U]
    sb = [jnp.where(tri2, t[:, :LANES], 0.0) for t in sbk]
    sk = [jnp.where(tri2, t[:, LANES:], 0.0) for t in sbk]
    a_ab = [t[:chunk] for t in sb]
    a_rb = [t[chunk:] for t in sb]

    xp = [_bf16_dot(t, block_diag(t), nn) for t in a_ab]
    tinv = [eye2 + t for t in a_ab]
    power = 2
    while 2 * power < chunk:
        both = [_bf16_dot(jnp.concatenate([xp[i], tinv[i]], axis=0), block_diag(xp[i]), nn) for i in U]
        xp = [t[:chunk] for t in both]
        tinv = [tinv[i] + both[i][chunk:] for i in U]
        power *= 2
    tinv = [tinv[i] + _bf16_dot(tinv[i], block_diag(xp[i]), nn) for i in U]

    state = [st_ref[p] for p in range(pairs)]
    y = [None] * len(units)
    for cc in range(subs):
        ids = [cc * pairs + p for p in range(pairs)]
        zy = [_bf16_dot(jnp.concatenate([lhs[i], sk[i]], axis=1),
                        jnp.concatenate([state[p].T, head_stack(v[i])], axis=0), nn)
              for p, i in enumerate(ids)]
        u = [_bf16_dot(tinv[i], head_stack(zy[p][:chunk]), nn) for p, i in enumerate(ids)]
        for p, i in enumerate(ids):
            y[i] = zy[p][chunk:] + _bf16_dot(a_rb[i], head_stack(u[p]), nn)
            e_last = jnp.exp(cum_last[i] - cum[i])
            s_new = state[p] * jnp.exp(cum_last[i]) + _bf16_dot(
                jnp.concatenate([u[p], v[i]], axis=0),
                jnp.concatenate([bv[i] * e_last, k[i] * e_last], axis=0), _TN)
            state[p] = jnp.where(bd, s_new, 0.0)
    for p in range(pairs):
        st_ref[p] = state[p]

    sums = _head_sums(y + [r[i] * k[i] * rk_ref[:, sls[i]] for i in U], ones_bd)
    dlt = [y[i] - sums[i] * (1.0 / n) for i in U]
    var = _head_sums([t * t for t in dlt], ones_bd)
    for i in U:
        yn = dlt[i] * lax.rsqrt(var[i] * (1.0 / n) + RWKV_GN_EPS) * gg_ref[:, sls[i]] + gb_ref[:, sls[i]]
        bonus = sums[len(units) + i] * v[i]
        o_ref[0, rows[i], sls[i]] = ((yn + bonus) * gate_ref[0, rows[i], sls[i]]).astype(o_ref.dtype)


def _rwkv_chunk(r, ld, k, v, kk, bvec, gate, r_k, gn_g, gn_b, *, pairs, subs):
    bsz, t_len, d = r.shape
    chunk = RWKV_HEAD
    width = pairs * LANES
    seq_spec = pl.BlockSpec((1, subs * chunk, width), lambda b, g, c: (b, c, g))
    vec_spec = pl.BlockSpec((1, width), lambda b, g, c: (0, g))
    return pl.pallas_call(
        functools.partial(_rwkv_chunk_body, chunk=chunk, pairs=pairs, subs=subs),
        grid=(bsz, d // width, t_len // (subs * chunk)),
        in_specs=[seq_spec] * 7 + [vec_spec] * 3,
        out_specs=seq_spec,
        out_shape=jax.ShapeDtypeStruct((bsz, t_len, d), BF16),
        scratch_shapes=[pltpu.VMEM((pairs, LANES, LANES), F32)],
        compiler_params=_cparams("parallel", "parallel", "arbitrary"),
        name="rwkv_chunk",
    )(r, ld, k, v, kk, bvec, gate, r_k.reshape(1, d), gn_g.reshape(1, d), gn_b.reshape(1, d))


def _rwkv_mixer(x, bsz, t_len, mix, w_rkv, w0, w1, w2, a0, a1, a2, g1, g2, k_k, k_a, r_k, gn_g, gn_b, w_o,
                ln_g, ln_b):
    d = x.shape[1]
    outs = _rwkv_proj(x, t_len, mix, w_rkv, w0, w1, w2, a0, a1, a2, g1, g2, k_k, k_a, tm=RWKV_PROJ_ROWS)
    seq = [o.reshape(bsz, t_len, d) for o in outs]
    o = _rwkv_chunk(*seq, r_k, gn_g, gn_b, pairs=RWKV_HEAD_PAIRS, subs=RWKV_SUBCHUNKS)
    return _matmul_residual_ln(o.reshape(bsz * t_len, d), w_o.astype(BF16), x, ln_g, ln_b, tm=OUT_PROJ_ROWS)


def _ret_chunk_body(q_ref, k_ref, v_ref, gate_ref, cos_ref, sin_ref, g_ref, b_ref, o_ref, st_ref, *, chunk):
    c = pl.program_id(1)

    @pl.when(c == 0)
    def _():
        st_ref[...] = jnp.zeros_like(st_ref)

    n_heads, dk, dv = st_ref.shape
    half = dk // 2
    cos = cos_ref[...]
    sin = sin_ref[...]

    def rotate(t):
        te, to = t[:, :half], t[:, half:]
        return jnp.concatenate([te * cos - to * sin, to * cos + te * sin], axis=1)

    ri = lax.broadcasted_iota(jnp.int32, (chunk, chunk), 0)
    ci = lax.broadcasted_iota(jnp.int32, (chunk, chunk), 1)
    rel = (ri - ci).astype(F32)
    pos = lax.broadcasted_iota(jnp.int32, (chunk, dk), 0).astype(F32)

    for h in range(n_heads):
        log_gamma = math.log(1.0 - 2.0 ** (-5.0 - h))
        inner = jnp.where(rel >= 0, jnp.exp(log_gamma * jnp.maximum(rel, 0.0)), 0.0)
        cross = jnp.exp(log_gamma * (pos + 1.0))
        sdecay = jnp.exp(log_gamma * (chunk - 1.0 - pos))
        chunk_decay = jnp.exp(log_gamma * chunk)

        q = rotate(q_ref[0, :, h * dk:(h + 1) * dk].astype(F32))
        k = rotate(k_ref[0, :, h * dk:(h + 1) * dk].astype(F32)) * (dk ** -0.5)
        v = v_ref[0, :, h * dv:(h + 1) * dv].astype(BF16)
        s = lax.dot_general(q.astype(BF16), k.astype(BF16), _NT, preferred_element_type=F32) * inner
        st = st_ref[h]
        o = (jnp.dot(s.astype(BF16), v, preferred_element_type=F32)
             + jnp.dot((q * cross).astype(BF16), st.astype(BF16), preferred_element_type=F32))
        st_ref[h] = st * chunk_decay + lax.dot_general((k * sdecay).astype(BF16), v, _TN,
                                                       preferred_element_type=F32)

        mu = jnp.mean(o, axis=-1, keepdims=True)
        dlt = o - mu
        var = jnp.mean(dlt * dlt, axis=-1, keepdims=True)
        vs = slice(h * dv, (h + 1) * dv)
        on = dlt * lax.rsqrt(var + RET_GN_EPS) * g_ref[:, vs] + b_ref[:, vs]
        gt = gate_ref[0, :, vs].astype(F32)
        o_ref[0, :, vs] = (gt * _sigmoid(gt) * on).astype(o_ref.dtype)


def _ret_chunk(proj, cos, sin, gn_g, gn_b, *, chunk):
    bsz, t_len, six_d = proj.shape
    d = six_d // 6
    h = RET_HEADS
    dk, dv = d // h, 2 * d // h
    return pl.pallas_call(
        functools.partial(_ret_chunk_body, chunk=chunk),
        grid=(bsz, t_len // chunk),
        in_specs=[pl.BlockSpec((1, chunk, d), lambda b, c: (b, c, 0)),
                  pl.BlockSpec((1, chunk, d), lambda b, c: (b, c, 1)),
                  pl.BlockSpec((1, chunk, 2 * d), lambda b, c: (b, c, 1)),
                  pl.BlockSpec((1, chunk, 2 * d), lambda b, c: (b, c, 2)),
                  pl.BlockSpec((chunk, dk // 2), lambda b, c: (c, 0)),
                  pl.BlockSpec((chunk, dk // 2), lambda b, c: (c, 0)),
                  pl.BlockSpec((1, 2 * d), lambda b, c: (0, 0)),
                  pl.BlockSpec((1, 2 * d), lambda b, c: (0, 0))],
        out_specs=pl.BlockSpec((1, chunk, 2 * d), lambda b, c: (b, c, 0)),
        out_shape=jax.ShapeDtypeStruct((bsz, t_len, 2 * d), BF16),
        scratch_shapes=[pltpu.VMEM((h, dk, dv), F32)],
        compiler_params=_cparams("parallel", "arbitrary"),
        name="retention_chunk",
    )(proj, proj, proj, proj, cos, sin, gn_g.reshape(1, 2 * d), gn_b.reshape(1, 2 * d))


def _retention_mixer(x, bsz, t_len, w_in, gn_g, gn_b, w_o, ln_g, ln_b):
    d = x.shape[1]
    h = RET_HEADS
    dk = d // h
    w_qk = w_in[:, :2 * d].reshape(d, 2 * h, dk // 2, 2).transpose(0, 1, 3, 2).reshape(d, 2 * d)
    w_perm = jnp.concatenate([w_qk, w_in[:, 2 * d:]], axis=1).astype(BF16)
    proj = _matmul(x, w_perm, tm=PROJ_ROWS, tn=PROJ_COLS, out_dtype=BF16)
    inv = 1.0 / (RET_ROPE_BASE ** jnp.linspace(0.0, 1.0, dk // 2, dtype=F32))
    ang = jnp.arange(t_len, dtype=F32)[:, None] * inv[None, :]
    o = _ret_chunk(proj.reshape(bsz, t_len, 6 * d), jnp.cos(ang), jnp.sin(ang), gn_g, gn_b, chunk=RET_CHUNK)
    return _matmul_residual_ln(o.reshape(bsz * t_len, 2 * d), w_o.astype(BF16), x, ln_g, ln_b,
                               tm=OUT_PROJ_ROWS)


def _moba_body(qt_ref, vt_ref, k_ref, o_ref, kmean_ref, va_ref, bias_ref, *, n_blk, pairs):
    qb = pl.program_id(2)
    blk = MOBA_BLOCK
    hd = MOBA_HEAD_DIM
    heads = 2 * pairs
    aug = hd + BF16_ROWS
    t_len = n_blk * blk
    H = range(heads)

    @pl.when(qb == 0)
    def _():
        kf = k_ref[...].astype(F32)
        kmean_ref[...] = jnp.mean(kf.reshape(n_blk, blk, pairs * LANES), axis=1)
        ones = jnp.ones((BF16_ROWS, t_len), BF16)
        for h in H:
            va_ref[h, 0:hd, :] = vt_ref[h * hd:(h + 1) * hd, :]
            va_ref[h, hd:aug, :] = ones

    kmean = kmean_ref[...]
    qt = qt_ref[...].astype(F32)
    zeros = jnp.zeros((hd, blk), F32)
    blk_id = lax.broadcasted_iota(jnp.int32, (n_blk, blk), 0)
    scale = hd ** -0.5

    qz_b = []
    for h in H:
        qh = qt[h * hd:(h + 1) * hd]
        qz = jnp.concatenate([qh, zeros] if h % 2 == 0 else [zeros, qh], axis=0)
        lanes = slice((h // 2) * LANES, (h // 2 + 1) * LANES)
        gate = jnp.dot(kmean[:, lanes], qz, precision=HIGHEST, preferred_element_type=F32)
        beaten = jnp.zeros((n_blk, blk), F32)
        for m in range(n_blk):
            gm = gate[m:m + 1, :]
            wins = jnp.where(gm > gate, 1.0, jnp.where(gm == gate, jnp.where(blk_id > m, 1.0, 0.0), 0.0))
            beaten = beaten + jnp.where(m < qb, wins, 0.0)
        bias_ref[h] = jnp.where(blk_id < qb, jnp.where(beaten < MOBA_TOPK, 0.0, NEG_BIG), NEG_BIG)
        qz_b.append((qz * (scale * LOG2_E)).astype(BF16))

    def scores(off):
        kb = k_ref[pl.ds(off, blk), :]
        return tuple(jnp.dot(kb[:, (h // 2) * LANES:(h // 2 + 1) * LANES], qz_b[h],
                             preferred_element_type=F32) for h in H)

    def accumulate(off, s, m_new, shift, carry):
        p = [jnp.exp2(s[h] - shift[h]).astype(BF16) for h in H]
        alpha = [jnp.exp2(carry[h][0] - m_new[h]) for h in H]
        acc = [alpha[h] * carry[h][1]
               + jnp.dot(va_ref[h, :, pl.ds(off, blk)], p[h], preferred_element_type=F32) for h in H]
        return tuple((m_new[h], acc[h]) for h in H)

    def attend(off, s, carry):
        m_new = [jnp.maximum(carry[h][0], jnp.max(s[h], axis=0, keepdims=True)) for h in H]
        return accumulate(off, s, m_new, m_new, carry)

    def past_block(nb, carry):
        off = pl.multiple_of(nb * blk, blk)
        s = scores(off)
        chosen = [bias_ref[h, pl.ds(nb, 1), :] >= 0.0 for h in H]
        m_new = [jnp.where(chosen[h], jnp.maximum(carry[h][0], jnp.max(s[h], axis=0, keepdims=True)),
                           carry[h][0]) for h in H]
        shift = [jnp.where(chosen[h], m_new[h], -NEG_BIG) for h in H]
        return accumulate(off, s, m_new, shift, carry)

    init = tuple((jnp.full((1, blk), NEG_BIG, F32), jnp.zeros((aug, blk), F32)) for _ in H)
    carry = lax.fori_loop(0, qb, past_block, init)
    key_i = lax.broadcasted_iota(jnp.int32, (blk, blk), 0)
    qry_i = lax.broadcasted_iota(jnp.int32, (blk, blk), 1)
    causal_bias = jnp.where(key_i <= qry_i, 0.0, NEG_BIG)
    own = pl.multiple_of(qb * blk, blk)
    s_own = scores(own)
    res = attend(own, [s_own[h] + causal_bias for h in H], carry)
    for h in H:
        acc = res[h][1]
        o_ref[h * hd:(h + 1) * hd, :] = acc[:hd] / acc[hd:hd + 1]


def _moba_attention(qvt, k, bsz, t_len, *, pairs):
    d = k.shape[1]
    blk = MOBA_BLOCK
    n_blk = t_len // blk
    width = pairs * LANES
    n_grp = d // width
    heads = 2 * pairs
    return pl.pallas_call(
        functools.partial(_moba_body, n_blk=n_blk, pairs=pairs),
        grid=(bsz, n_grp, n_blk),
        in_specs=[pl.BlockSpec((width, blk), lambda b, p, q: (p, b * n_blk + q)),
                  pl.BlockSpec((width, t_len), lambda b, p, q: (n_grp + p, b)),
                  pl.BlockSpec((t_len, width), lambda b, p, q: (b, p))],
        out_specs=pl.BlockSpec((width, blk), lambda b, p, q: (p, b * n_blk + q)),
        out_shape=jax.ShapeDtypeStruct((d, bsz * t_len), F32),
        scratch_shapes=[pltpu.VMEM((n_blk, width), F32),
                        pltpu.VMEM((heads, MOBA_HEAD_DIM + BF16_ROWS, t_len), BF16),
                        pltpu.VMEM((heads, n_blk, blk), F32)],
        compiler_params=_cparams("parallel", "parallel", "arbitrary"),
        name="moba_attention",
    )(qvt, qvt, k)


def _moba_mixer(x, bsz, t_len, w_qkv, w_o, ln_g, ln_b):
    d = x.shape[1]
    assert t_len % MOBA_BLOCK == 0 and d == MOBA_HEADS * MOBA_HEAD_DIM
    w_k = w_qkv[:, d:2 * d].astype(BF16)
    w_qv_t = jnp.concatenate([w_qkv[:, :d], w_qkv[:, 2 * d:]], axis=1).T.astype(BF16)
    k = _matmul(x, w_k, tm=PROJ_ROWS, tn=PROJ_COLS, out_dtype=BF16)
    qvt = _matmul_nt(x, w_qv_t, tm=PROJ_ROWS, tn=PROJ_COLS, out_dtype=BF16)
    ot = _moba_attention(qvt, k, bsz, t_len, pairs=MOBA_HEAD_PAIRS)
    return _matmul_residual_ln(ot, w_o.astype(BF16), x, ln_g, ln_b, tm=OUT_PROJ_ROWS, lhs_transposed=True)


def kernel(x, rwkv_mix, rwkv_w_rkv, rwkv_w0, rwkv_w1, rwkv_w2, rwkv_a0, rwkv_a1, rwkv_a2, rwkv_g1, rwkv_g2,
           rwkv_k_k, rwkv_k_a, rwkv_r_k, rwkv_gn_g, rwkv_gn_b, rwkv_w_o, ret_w_in, ret_gn_g, ret_gn_b, ret_w_o,
           moba_w_qkv, moba_w_o, ffn_w_in, ffn_conv_w, ffn_conv_b, ffn_w_out, ln1_g, ln1_b, ln2_g, ln2_b):
    bsz, t_len, d = x.shape
    h = x.reshape(bsz * t_len, d)
    ffn_w_in_b = ffn_w_in.astype(BF16)
    ffn_w_out_b = ffn_w_out.astype(BF16)
    for i in range(DEPTH):
        kind, j = i % N_MIXERS, i // N_MIXERS
        if kind == 0:
            h = _rwkv_mixer(h, bsz, t_len, rwkv_mix[j], rwkv_w_rkv[j], rwkv_w0[j], rwkv_w1[j], rwkv_w2[j],
                            rwkv_a0[j], rwkv_a1[j], rwkv_a2[j], rwkv_g1[j], rwkv_g2[j], rwkv_k_k[j],
                            rwkv_k_a[j], rwkv_r_k[j], rwkv_gn_g[j], rwkv_gn_b[j], rwkv_w_o[j],
                            ln1_g[i], ln1_b[i])
        elif kind == 1:
            h = _retention_mixer(h, bsz, t_len, ret_w_in[j], ret_gn_g[j], ret_gn_b[j], ret_w_o[j],
                                 ln1_g[i], ln1_b[i])
        else:
            h = _moba_mixer(h, bsz, t_len, moba_w_qkv[j], moba_w_o[j], ln1_g[i], ln1_b[i])
        h = _conv_ffn_ln(h, t_len, i, ffn_w_in_b, ffn_conv_w, ffn_conv_b, ffn_w_out_b, ln2_g, ln2_b,
                         tm=FFN_ROWS, cols=FFN_COL_SLICE)
    return h.reshape(bsz, t_len, d)
```

```python
import functools
import math

import jax
import jax.numpy as jnp
from jax import lax
from jax.experimental import pallas as pl
from jax.experimental.pallas import tpu as pltpu

F32 = jnp.float32
BF16 = jnp.bfloat16
HIGHEST = lax.Precision.HIGHEST

DEPTH = 4
N_MIXERS = 3
RWKV_HEAD = 64
RWKV_GN_EPS = 64e-5
RET_HEADS = 4
RET_ROPE_BASE = 10000.0
RET_GN_EPS = 1e-5
MOBA_HEADS = 16
MOBA_HEAD_DIM = 64
MOBA_BLOCK = 256
MOBA_TOPK = 3
LN_EPS = 1e-5
DEEPNORM_ALPHA = (2 * DEPTH) ** 0.25

LANES = 128
SUBLANES = 8
BF16_ROWS = 16
VMEM_LIMIT_BYTES = 52 * 1024 * 1024

NEG_BIG = -1e30
LOG2_E = math.log2(math.e)

PROJ_ROWS = 512
PROJ_COLS = 512
OUT_PROJ_ROWS = 1024
FFN_ROWS = 512
FFN_COL_SLICE = 256
RWKV_PROJ_ROWS = 512
RWKV_HEAD_PAIRS = 8
RWKV_SUBCHUNKS = 2
RET_CHUNK = 256
MOBA_HEAD_PAIRS = 4

_NT = (((1,), (1,)), ((), ()))
_TN = (((0,), (0,)), ((), ()))


def _cparams(*sem):
    return pltpu.CompilerParams(dimension_semantics=sem, vmem_limit_bytes=VMEM_LIMIT_BYTES)


def _sigmoid(x):
    return 1.0 / (1.0 + jnp.exp(-x))


def _layer_norm_rows(y, g, b):
    mu = jnp.mean(y, axis=-1, keepdims=True)
    d = y - mu
    var = jnp.mean(d * d, axis=-1, keepdims=True)
    return d * lax.rsqrt(var + LN_EPS) * g + b


def _mm_body(x_ref, w_ref, o_ref, xb_ref, *, tn, transposed_out):
    xb_ref[...] = x_ref[...].astype(BF16)
    n = w_ref.shape[0] if transposed_out else w_ref.shape[1]
    for c in range(n // tn):
        sl = slice(c * tn, (c + 1) * tn)
        if transposed_out:
            o_ref[sl, :] = lax.dot_general(w_ref[sl, :], xb_ref[...], _NT,
                                           preferred_element_type=F32).astype(o_ref.dtype)
        else:
            o_ref[:, sl] = jnp.dot(xb_ref[...], w_ref[:, sl], preferred_element_type=F32).astype(o_ref.dtype)


def _matmul(x, w, *, tm, tn, out_dtype=F32):
    m, k = x.shape
    n = w.shape[1]
    return pl.pallas_call(
        functools.partial(_mm_body, tn=tn, transposed_out=False),
        grid=(m // tm,),
        in_specs=[pl.BlockSpec((tm, k), lambda i: (i, 0)),
                  pl.BlockSpec((k, n), lambda i: (0, 0), pipeline_mode=pl.Buffered(1))],
        out_specs=pl.BlockSpec((tm, n), lambda i: (i, 0)),
        out_shape=jax.ShapeDtypeStruct((m, n), out_dtype),
        scratch_shapes=[pltpu.VMEM((tm, k), BF16)],
        compiler_params=_cparams("parallel"),
        name="matmul",
    )(x, w)


def _matmul_nt(x, wt, *, tm, tn, out_dtype=F32):
    m, k = x.shape
    n = wt.shape[0]
    return pl.pallas_call(
        functools.partial(_mm_body, tn=tn, transposed_out=True),
        grid=(m // tm,),
        in_specs=[pl.BlockSpec((tm, k), lambda i: (i, 0)),
                  pl.BlockSpec((n, k), lambda i: (0, 0), pipeline_mode=pl.Buffered(1))],
        out_specs=pl.BlockSpec((n, tm), lambda i: (0, i)),
        out_shape=jax.ShapeDtypeStruct((n, m), out_dtype),
        scratch_shapes=[pltpu.VMEM((tm, k), BF16)],
        compiler_params=_cparams("parallel"),
        name="matmul_nt",
    )(x, wt)


def _mm_res_ln_body(o_ref, w_ref, x_ref, g_ref, b_ref, out_ref, *, lhs_transposed):
    lhs = o_ref[...].astype(BF16)
    if lhs_transposed:
        acc = lax.dot_general(lhs, w_ref[...], _TN, preferred_element_type=F32)
    else:
        acc = jnp.dot(lhs, w_ref[...], preferred_element_type=F32)
    y = DEEPNORM_ALPHA * x_ref[...] + acc
    out_ref[...] = _layer_norm_rows(y, g_ref[...], b_ref[...])


def _matmul_residual_ln(o, w, x, g, b, *, tm, lhs_transposed=False):
    m, d = x.shape
    k = w.shape[0]
    if lhs_transposed:
        o_spec = pl.BlockSpec((k, tm), lambda i: (0, i))
    else:
        o_spec = pl.BlockSpec((tm, k), lambda i: (i, 0))
    return pl.pallas_call(
        functools.partial(_mm_res_ln_body, lhs_transposed=lhs_transposed),
        grid=(m // tm,),
        in_specs=[o_spec,
                  pl.BlockSpec((k, d), lambda i: (0, 0)),
                  pl.BlockSpec((tm, d), lambda i: (i, 0)),
                  pl.BlockSpec((1, d), lambda i: (0, 0)),
                  pl.BlockSpec((1, d), lambda i: (0, 0))],
        out_specs=pl.BlockSpec((tm, d), lambda i: (i, 0)),
        out_shape=jax.ShapeDtypeStruct((m, d), F32),
        compiler_params=_cparams("parallel"),
        name="out_proj_residual_ln",
    )(o, w, x, g.reshape(1, d), b.reshape(1, d))


def _ffn_body(x_ref, xh_ref, w_in_ref, cw_ref, cb_ref, wo_ref, g_ref, b_ref,
              out_ref, xb_ref, xp_ref, hu_ref, hg_ref, act_ref, *, tm, tiles_per_seq, ff, cols):
    i = pl.program_id(0)
    d = x_ref.shape[1]
    halo = BF16_ROWS
    grp = SUBLANES
    n_grp = tm // grp
    first = (i % tiles_per_seq) == 0
    xb_ref[0:halo, :] = jnp.where(first, 0.0, xh_ref[...]).astype(BF16)
    xp = x_ref[...].reshape(grp, n_grp, d).swapaxes(0, 1).reshape(tm, d)
    xp_ref[...] = xp
    xb_ref[halo:, :] = xp.astype(BF16)
    sub = lax.broadcasted_iota(jnp.int32, (grp, cols), 0)

    def hidden(h_ref, c):
        h = jnp.dot(xb_ref[...], w_in_ref[:, c], preferred_element_type=F32)
        h_ref[halo:, :] = h[halo:]
        for back in (1, 2):
            last = h[halo + tm - back * grp:halo + tm - (back - 1) * grp]
            prev = jnp.where(sub == 0, h[halo - back:halo - back + 1], pltpu.roll(last, 1, 0))
            h_ref[halo - back * grp:halo - (back - 1) * grp, :] = prev

    def conv(h_ref, c):
        return (cw_ref[0:1, c] * h_ref[pl.ds(halo - 2 * grp, tm), :]
                + cw_ref[1:2, c] * h_ref[pl.ds(halo - grp, tm), :]
                + cw_ref[2:3, c] * h_ref[pl.ds(halo, tm), :]
                + cb_ref[:, c])

    for j in range(ff // cols):
        cu = slice(j * cols, (j + 1) * cols)
        cg = slice(ff + j * cols, ff + (j + 1) * cols)
        hu = hu_ref.at[j % 2]
        hg = hg_ref.at[j % 2]
        hidden(hu, cu)
        hidden(hg, cg)
        u = conv(hu, cu)
        gt = conv(hg, cg)
        act_ref[:, cu] = ((gt * _sigmoid(gt)) * u).astype(BF16)

    halves = 2
    gp = n_grp // halves
    for q in range(halves):
        rs = slice(q * gp * grp, (q + 1) * gp * grp)
        y = DEEPNORM_ALPHA * xp_ref[rs, :] + jnp.dot(act_ref[rs, :], wo_ref[...], preferred_element_type=F32)
        t = _layer_norm_rows(y, g_ref[...], b_ref[...]).reshape(gp, grp, d).swapaxes(0, 1)
        for a in range(grp):
            out_ref[a * n_grp + q * gp:a * n_grp + (q + 1) * gp, :] = t[a]


def _conv_ffn_ln(x, seq_len, layer, w_in, conv_w, conv_b, w_out, g, b, *, tm, cols):
    m, d = x.shape
    ff = w_out.shape[1]
    halo = BF16_ROWS
    body = functools.partial(_ffn_body, tm=tm, tiles_per_seq=seq_len // tm, ff=ff, cols=cols)
    resident = lambda shape: pl.BlockSpec((None,) + shape, lambda i: (layer, 0, 0),
                                          pipeline_mode=pl.Buffered(1))
    n_layers = w_in.shape[0]
    conv_b = conv_b.reshape(n_layers, 1, 2 * ff)
    g = g.reshape(n_layers, 1, d)
    b = b.reshape(n_layers, 1, d)
    return pl.pallas_call(
        body,
        grid=(m // tm,),
        in_specs=[pl.BlockSpec((tm, d), lambda i: (i, 0)),
                  pl.BlockSpec((halo, d), lambda i: (jnp.maximum(i * (tm // halo) - 1, 0), 0)),
                  resident((d, 2 * ff)),
                  resident((3, 2 * ff)),
                  resident((1, 2 * ff)),
                  resident((ff, d)),
                  resident((1, d)),
                  resident((1, d))],
        out_specs=pl.BlockSpec((tm, d), lambda i: (i, 0)),
        out_shape=jax.ShapeDtypeStruct((m, d), F32),
        scratch_shapes=[pltpu.VMEM((tm + halo, d), BF16),
                        pltpu.VMEM((tm, d), F32),
                        pltpu.VMEM((2, tm + halo, cols), F32),
                        pltpu.VMEM((2, tm + halo, cols), F32),
                        pltpu.VMEM((tm, ff), BF16)],
        compiler_params=_cparams("parallel"),
        name="conv_ffn_ln",
    )(x, x, w_in, conv_w, conv_b, w_out, g, b)


def _head_pair_sum(v):
    head1 = lax.broadcasted_iota(jnp.int32, v.shape, 1) >= RWKV_HEAD
    total = jnp.sum(v, axis=1, keepdims=True)
    right = jnp.sum(jnp.where(head1, v, 0.0), axis=1, keepdims=True)
    return jnp.where(head1, right, total - right)


def _rwkv_proj_body(x_ref, xp_ref, mix_ref, wrkv_ref, w1_ref, w2_ref, a1_ref, a2_ref, g1_ref, g2_ref,
                    w0_ref, a0_ref, kk_ref, ka_ref,
                    r_out, ld_out, k_out, v_out, kk_out, b_out, gate_out, *, tm, tiles_per_seq):
    i = pl.program_id(0)
    d = x_ref.shape[1]
    x = x_ref[...]
    first = (i % tiles_per_seq) == 0
    prev = jnp.where(first, 0.0, xp_ref[SUBLANES - 1:SUBLANES, :])
    row = lax.broadcasted_iota(jnp.int32, (tm, d), 0)
    xs = jnp.where(row == 0, prev, pltpu.roll(x, 1, 0))
    xx = xs - x

    x_b = x.astype(BF16)
    xx_b = xx.astype(BF16)

    def mixed(j):
        return x_b + xx_b * mix_ref[j:j + 1, :].astype(BF16)

    def mm(a, w):
        return jnp.dot(a, w, preferred_element_type=F32)

    r = mm(mixed(0), wrkv_ref[0])
    k = mm(mixed(2), wrkv_ref[1])
    v = mm(mixed(3), wrkv_ref[2])
    lw = w0_ref[...] + mm(jnp.tanh(mm(mixed(1), w1_ref[...])).astype(BF16), w2_ref[...])
    softplus_neg = jnp.maximum(-lw, 0.0) + jnp.log(1.0 + jnp.exp(-jnp.abs(lw)))
    log_decay = -jnp.exp(-softplus_neg - 0.5)
    a = _sigmoid(a0_ref[...] + mm(mm(mixed(4), a1_ref[...]).astype(BF16), a2_ref[...]))
    gate = mm(_sigmoid(mm(mixed(5), g1_ref[...])).astype(BF16), g2_ref[...])

    kk = k * kk_ref[...]
    sq = kk * kk
    ss = jnp.concatenate([_head_pair_sum(sq[:, j * LANES:(j + 1) * LANES]) for j in range(d // LANES)], axis=1)
    kk = kk / jnp.maximum(jnp.sqrt(ss), 1e-12)

    r_out[...] = r
    ld_out[...] = log_decay
    k_out[...] = k * (1.0 + (a - 1.0) * ka_ref[...])
    v_out[...] = v
    kk_out[...] = kk
    b_out[...] = kk * a
    gate_out[...] = gate


def _rwkv_proj(x, seq_len, mix, w_rkv, w0, w1, w2, a0, a1, a2, g1, g2, k_k, k_a, *, tm):
    m, d = x.shape
    full = lambda arr: pl.BlockSpec(arr.shape, lambda i: (0,) * arr.ndim, pipeline_mode=pl.Buffered(1))
    vec = lambda a: a.reshape(1, d)
    args = (mix, w_rkv.astype(BF16), w1.astype(BF16), w2.astype(BF16), a1.astype(BF16), a2.astype(BF16),
            g1.astype(BF16), g2.astype(BF16), vec(w0), vec(a0), vec(k_k), vec(k_a))
    row_spec = pl.BlockSpec((tm, d), lambda i: (i, 0))
    return pl.pallas_call(
        functools.partial(_rwkv_proj_body, tm=tm, tiles_per_seq=seq_len // tm),
        grid=(m // tm,),
        in_specs=[row_spec,
                  pl.BlockSpec((SUBLANES, d), lambda i: (jnp.maximum(i * (tm // SUBLANES) - 1, 0), 0))]
                 + [full(a) for a in args],
        out_specs=[row_spec] * 7,
        out_shape=[jax.ShapeDtypeStruct((m, d), F32)] * 7,
        compiler_params=_cparams("parallel"),
        name="rwkv_proj",
    )(x, x, *args)


def _bf16_dot(a, b, dims):
    return lax.dot_general(a.astype(BF16), b.astype(BF16), dims, preferred_element_type=F32)


def _head_sums(tiles):
    return [_head_pair_sum(t) for t in tiles]


def _rwkv_chunk_body(r_ref, ld_ref, k_ref, v_ref, kk_ref, b_ref, gate_ref, rk_ref, gg_ref, gb_ref,
                     o_ref, st_ref, *, chunk, pairs, subs):
    c = pl.program_id(2)

    @pl.when(c == 0)
    def _():
        st_ref[...] = jnp.zeros_like(st_ref)

    n = RWKV_HEAD
    row = lax.broadcasted_iota(jnp.int32, (chunk, LANES), 0)
    lane = lax.broadcasted_iota(jnp.int32, (chunk, LANES), 1)
    col = lane % n
    eye2 = jnp.where(row == col, 1.0, 0.0)
    head1 = lane >= n
    row2 = lax.broadcasted_iota(jnp.int32, (2 * chunk, LANES), 0)
    col2 = lax.broadcasted_iota(jnp.int32, (2 * chunk, LANES), 1) % n
    tri2 = jnp.where(row2 < chunk, row2 - 1, row2 - chunk) >= col2
    rr = lax.broadcasted_iota(jnp.int32, (LANES, LANES), 0) // n
    cc = lax.broadcasted_iota(jnp.int32, (LANES, LANES), 1) // n
    bd = rr == cc
    nn = (((1,), (0,)), ((), ()))

    def head_stack(t):
        return jnp.concatenate([jnp.where(head1, 0.0, t), jnp.where(head1, t, 0.0)], axis=0)

    def block_diag(t):
        return jnp.where(bd, jnp.concatenate([t, t], axis=0), 0.0)

    units = [(cc, p) for cc in range(subs) for p in range(pairs)]
    U = range(len(units))
    rows = [slice(cc * chunk, (cc + 1) * chunk) for cc, _ in units]
    sls = [slice(p * LANES, (p + 1) * LANES) for _, p in units]
    r = [r_ref[0, rows[i], sls[i]] for i in U]
    ld = [ld_ref[0, rows[i], sls[i]] for i in U]
    k = [k_ref[0, rows[i], sls[i]] for i in U]
    v = [v_ref[0, rows[i], sls[i]] for i in U]
    bv = [b_ref[0, rows[i], sls[i]] for i in U]

    cum = list(ld)
    shift = 1
    while shift < chunk:
        cum = [cu + jnp.where(row >= shift, pltpu.roll(cu, shift, 0), 0.0) for cu in cum]
        shift *= 2
    cum_last = [cu[chunk - 1:chunk, :] for cu in cum]
    e_inv = [jnp.exp(-cu) for cu in cum]
    r_hat = [r[i] * jnp.exp(cum[i]) for i in U]
    a_hat = [-(kk_ref[0, rows[i], sls[i]] * jnp.exp(cum[i] - ld[i])) for i in U]
    lhs = [jnp.concatenate([a_hat[i], r_hat[i]], axis=0) for i in U]
    sbk = [_bf16_dot(lhs[i], jnp.concatenate([head_stack(bv[i] * e_inv[i]), head_stack(k[i] * e_inv[i])], axis=0),
                     _NT) for i in U]
    sb = [jnp.where(tri2, t[:, :LANES], 0.0) for t in sbk]
    sk = [jnp.where(tri2, t[:, LANES:], 0.0) for t in sbk]
    a_ab = [t[:chunk] for t in sb]
    a_rb = [t[chunk:] for t in sb]

    xp = [_bf16_dot(t, block_diag(t), nn) for t in a_ab]
    tinv = [eye2 + t for t in a_ab]
    power = 2
    while 2 * power < chunk:
        both = [_bf16_dot(jnp.concatenate([xp[i], tinv[i]], axis=0), block_diag(xp[i]), nn) for i in U]
        xp = [t[:chunk] for t in both]
        tinv = [tinv[i] + both[i][chunk:] for i in U]
        power *= 2
    tinv = [tinv[i] + _bf16_dot(tinv[i], block_diag(xp[i]), nn) for i in U]

    state = [st_ref[p] for p in range(pairs)]
    y = [None] * len(units)
    for cc in range(subs):
        ids = [cc * pairs + p for p in range(pairs)]
        zy = [_bf16_dot(jnp.concatenate([lhs[i], sk[i]], axis=1),
                        jnp.concatenate([state[p].T, head_stack(v[i])], axis=0), nn)
              for p, i in enumerate(ids)]
        u = [_bf16_dot(tinv[i], head_stack(zy[p][:chunk]), nn) for p, i in enumerate(ids)]
        for p, i in enumerate(ids):
            y[i] = zy[p][chunk:] + _bf16_dot(a_rb[i], head_stack(u[p]), nn)
            e_last = jnp.exp(cum_last[i] - cum[i])
            s_new = state[p] * jnp.exp(cum_last[i]) + _bf16_dot(
                jnp.concatenate([u[p], v[i]], axis=0),
                jnp.concatenate([bv[i] * e_last, k[i] * e_last], axis=0), _TN)
            state[p] = jnp.where(bd, s_new, 0.0)
    for p in range(pairs):
        st_ref[p] = state[p]

    sums = _head_sums(y + [r[i] * k[i] * rk_ref[:, sls[i]] for i in U])
    dlt = [y[i] - sums[i] * (1.0 / n) for i in U]
    var = _head_sums([t * t for t in dlt])
    for i in U:
        yn = dlt[i] * lax.rsqrt(var[i] * (1.0 / n) + RWKV_GN_EPS) * gg_ref[:, sls[i]] + gb_ref[:, sls[i]]
        bonus = sums[len(units) + i] * v[i]
        o_ref[0, rows[i], sls[i]] = ((yn + bonus) * gate_ref[0, rows[i], sls[i]]).astype(o_ref.dtype)


def _rwkv_chunk(r, ld, k, v, kk, bvec, gate, r_k, gn_g, gn_b, *, pairs, subs):
    bsz, t_len, d = r.shape
    chunk = RWKV_HEAD
    width = pairs * LANES
    seq_spec = pl.BlockSpec((1, subs * chunk, width), lambda b, g, c: (b, c, g))
    vec_spec = pl.BlockSpec((1, width), lambda b, g, c: (0, g))
    return pl.pallas_call(
        functools.partial(_rwkv_chunk_body, chunk=chunk, pairs=pairs, subs=subs),
        grid=(bsz, d // width, t_len // (subs * chunk)),
        in_specs=[seq_spec] * 7 + [vec_spec] * 3,
        out_specs=seq_spec,
        out_shape=jax.ShapeDtypeStruct((bsz, t_len, d), BF16),
        scratch_shapes=[pltpu.VMEM((pairs, LANES, LANES), F32)],
        compiler_params=_cparams("parallel", "parallel", "arbitrary"),
        name="rwkv_chunk",
    )(r, ld, k, v, kk, bvec, gate, r_k.reshape(1, d), gn_g.reshape(1, d), gn_b.reshape(1, d))


def _rwkv_mixer(x, bsz, t_len, mix, w_rkv, w0, w1, w2, a0, a1, a2, g1, g2, k_k, k_a, r_k, gn_g, gn_b, w_o,
                ln_g, ln_b):
    d = x.shape[1]
    outs = _rwkv_proj(x, t_len, mix, w_rkv, w0, w1, w2, a0, a1, a2, g1, g2, k_k, k_a, tm=RWKV_PROJ_ROWS)
    seq = [o.reshape(bsz, t_len, d) for o in outs]
    o = _rwkv_chunk(*seq, r_k, gn_g, gn_b, pairs=RWKV_HEAD_PAIRS, subs=RWKV_SUBCHUNKS)
    return _matmul_residual_ln(o.reshape(bsz * t_len, d), w_o.astype(BF16), x, ln_g, ln_b, tm=OUT_PROJ_ROWS)


def _ret_chunk_body(q_ref, k_ref, v_ref, gate_ref, cos_ref, sin_ref, g_ref, b_ref, o_ref, st_ref, *, chunk):
    c = pl.program_id(1)

    @pl.when(c == 0)
    def _():
        st_ref[...] = jnp.zeros_like(st_ref)

    n_heads, dk, dv = st_ref.shape
    half = dk // 2
    cos = cos_ref[...]
    sin = sin_ref[...]

    def rotate(t):
        te, to = t[:, :half], t[:, half:]
        return jnp.concatenate([te * cos - to * sin, to * cos + te * sin], axis=1)

    ri = lax.broadcasted_iota(jnp.int32, (chunk, chunk), 0)
    ci = lax.broadcasted_iota(jnp.int32, (chunk, chunk), 1)
    rel = (ri - ci).astype(F32)
    pos = lax.broadcasted_iota(jnp.int32, (chunk, dk), 0).astype(F32)

    for h in range(n_heads):
        log_gamma = math.log(1.0 - 2.0 ** (-5.0 - h))
        inner = jnp.where(rel >= 0, jnp.exp(log_gamma * jnp.maximum(rel, 0.0)), 0.0)
        cross = jnp.exp(log_gamma * (pos + 1.0))
        sdecay = jnp.exp(log_gamma * (chunk - 1.0 - pos))
        chunk_decay = jnp.exp(log_gamma * chunk)

        q = rotate(q_ref[0, :, h * dk:(h + 1) * dk].astype(F32))
        k = rotate(k_ref[0, :, h * dk:(h + 1) * dk].astype(F32)) * (dk ** -0.5)
        v = v_ref[0, :, h * dv:(h + 1) * dv].astype(BF16)
        s = lax.dot_general(q.astype(BF16), k.astype(BF16), _NT, preferred_element_type=F32) * inner
        st = st_ref[h]
        o = (jnp.dot(s.astype(BF16), v, preferred_element_type=F32)
             + jnp.dot((q * cross).astype(BF16), st.astype(BF16), preferred_element_type=F32))
        st_ref[h] = st * chunk_decay + lax.dot_general((k * sdecay).astype(BF16), v, _TN,
                                                       preferred_element_type=F32)

        mu = jnp.mean(o, axis=-1, keepdims=True)
        dlt = o - mu
        var = jnp.mean(dlt * dlt, axis=-1, keepdims=True)
        vs = slice(h * dv, (h + 1) * dv)
        on = dlt * lax.rsqrt(var + RET_GN_EPS) * g_ref[:, vs] + b_ref[:, vs]
        gt = gate_ref[0, :, vs].astype(F32)
        o_ref[0, :, vs] = (gt * _sigmoid(gt) * on).astype(o_ref.dtype)


def _ret_chunk(proj, cos, sin, gn_g, gn_b, *, chunk):
    bsz, t_len, six_d = proj.shape
    d = six_d // 6
    h = RET_HEADS
    dk, dv = d // h, 2 * d // h
    return pl.pallas_call(
        functools.partial(_ret_chunk_body, chunk=chunk),
        grid=(bsz, t_len // chunk),
        in_specs=[pl.BlockSpec((1, chunk, d), lambda b, c: (b, c, 0)),
                  pl.BlockSpec((1, chunk, d), lambda b, c: (b, c, 1)),
                  pl.BlockSpec((1, chunk, 2 * d), lambda b, c: (b, c, 1)),
                  pl.BlockSpec((1, chunk, 2 * d), lambda b, c: (b, c, 2)),
                  pl.BlockSpec((chunk, dk // 2), lambda b, c: (c, 0)),
                  pl.BlockSpec((chunk, dk // 2), lambda b, c: (c, 0)),
                  pl.BlockSpec((1, 2 * d), lambda b, c: (0, 0)),
                  pl.BlockSpec((1, 2 * d), lambda b, c: (0, 0))],
        out_specs=pl.BlockSpec((1, chunk, 2 * d), lambda b, c: (b, c, 0)),
        out_shape=jax.ShapeDtypeStruct((bsz, t_len, 2 * d), BF16),
        scratch_shapes=[pltpu.VMEM((h, dk, dv), F32)],
        compiler_params=_cparams("parallel", "arbitrary"),
        name="retention_chunk",
    )(proj, proj, proj, proj, cos, sin, gn_g.reshape(1, 2 * d), gn_b.reshape(1, 2 * d))


def _retention_mixer(x, bsz, t_len, w_in, gn_g, gn_b, w_o, ln_g, ln_b):
    d = x.shape[1]
    h = RET_HEADS
    dk = d // h
    w_qk = w_in[:, :2 * d].reshape(d, 2 * h, dk // 2, 2).transpose(0, 1, 3, 2).reshape(d, 2 * d)
    w_perm = jnp.concatenate([w_qk, w_in[:, 2 * d:]], axis=1).astype(BF16)
    proj = _matmul(x, w_perm, tm=PROJ_ROWS, tn=PROJ_COLS, out_dtype=BF16)
    inv = 1.0 / (RET_ROPE_BASE ** jnp.linspace(0.0, 1.0, dk // 2, dtype=F32))
    ang = jnp.arange(t_len, dtype=F32)[:, None] * inv[None, :]
    o = _ret_chunk(proj.reshape(bsz, t_len, 6 * d), jnp.cos(ang), jnp.sin(ang), gn_g, gn_b, chunk=RET_CHUNK)
    return _matmul_residual_ln(o.reshape(bsz * t_len, 2 * d), w_o.astype(BF16), x, ln_g, ln_b,
                               tm=OUT_PROJ_ROWS)


def _moba_body(qt_ref, vt_ref, k_ref, o_ref, kmean_ref, va_ref, bias_ref, *, n_blk, pairs):
    qb = pl.program_id(2)
    blk = MOBA_BLOCK
    hd = MOBA_HEAD_DIM
    heads = 2 * pairs
    aug = hd + BF16_ROWS
    t_len = n_blk * blk
    H = range(heads)

    @pl.when(qb == 0)
    def _():
        kf = k_ref[...].astype(F32)
        kmean_ref[...] = jnp.mean(kf.reshape(n_blk, blk, pairs * LANES), axis=1)
        ones = jnp.ones((BF16_ROWS, t_len), BF16)
        for h in H:
            va_ref[h, 0:hd, :] = vt_ref[h * hd:(h + 1) * hd, :]
            va_ref[h, hd:aug, :] = ones

    kmean = kmean_ref[...]
    qt = qt_ref[...].astype(F32)
    zeros = jnp.zeros((hd, blk), F32)
    blk_id = lax.broadcasted_iota(jnp.int32, (n_blk, blk), 0)
    scale = hd ** -0.5

    qz_b = []
    for h in H:
        qh = qt[h * hd:(h + 1) * hd]
        qz = jnp.concatenate([qh, zeros] if h % 2 == 0 else [zeros, qh], axis=0)
        lanes = slice((h // 2) * LANES, (h // 2 + 1) * LANES)
        gate = jnp.dot(kmean[:, lanes], qz, precision=HIGHEST, preferred_element_type=F32)
        beaten = jnp.zeros((n_blk, blk), F32)
        for m in range(n_blk):
            gm = gate[m:m + 1, :]
            wins = jnp.where(gm > gate, 1.0, jnp.where(gm == gate, jnp.where(blk_id > m, 1.0, 0.0), 0.0))
            beaten = beaten + jnp.where(m < qb, wins, 0.0)
        bias_ref[h] = jnp.where(blk_id < qb, jnp.where(beaten < MOBA_TOPK, 0.0, NEG_BIG), NEG_BIG)
        qz_b.append((qz * (scale * LOG2_E)).astype(BF16))

    def scores(off):
        kb = k_ref[pl.ds(off, blk), :]
        return tuple(jnp.dot(kb[:, (h // 2) * LANES:(h // 2 + 1) * LANES], qz_b[h],
                             preferred_element_type=F32) for h in H)

    def accumulate(off, s, m_new, shift, carry):
        p = [jnp.exp2(s[h] - shift[h]).astype(BF16) for h in H]
        alpha = [jnp.exp2(carry[h][0] - m_new[h]) for h in H]
        acc = [alpha[h] * carry[h][1]
               + jnp.dot(va_ref[h, :, pl.ds(off, blk)], p[h], preferred_element_type=F32) for h in H]
        return tuple((m_new[h], acc[h]) for h in H)

    def attend(off, s, carry):
        m_new = [jnp.maximum(carry[h][0], jnp.max(s[h], axis=0, keepdims=True)) for h in H]
        return accumulate(off, s, m_new, m_new, carry)

    def past_block(nb, carry):
        off = pl.multiple_of(nb * blk, blk)
        s = scores(off)
        chosen = [bias_ref[h, pl.ds(nb, 1), :] >= 0.0 for h in H]
        m_new = [jnp.where(chosen[h], jnp.maximum(carry[h][0], jnp.max(s[h], axis=0, keepdims=True)),
                           carry[h][0]) for h in H]
        shift = [jnp.where(chosen[h], m_new[h], -NEG_BIG) for h in H]
        return accumulate(off, s, m_new, shift, carry)

    init = tuple((jnp.full((1, blk), NEG_BIG, F32), jnp.zeros((aug, blk), F32)) for _ in H)
    carry = lax.fori_loop(0, qb, past_block, init)
    key_i = lax.broadcasted_iota(jnp.int32, (blk, blk), 0)
    qry_i = lax.broadcasted_iota(jnp.int32, (blk, blk), 1)
    causal_bias = jnp.where(key_i <= qry_i, 0.0, NEG_BIG)
    own = pl.multiple_of(qb * blk, blk)
    s_own = scores(own)
    res = attend(own, [s_own[h] + causal_bias for h in H], carry)
    for h in H:
        acc = res[h][1]
        o_ref[h * hd:(h + 1) * hd, :] = acc[:hd] / acc[hd:hd + 1]


def _moba_attention(qvt, k, bsz, t_len, *, pairs):
    d = k.shape[1]
    blk = MOBA_BLOCK
    n_blk = t_len // blk
    width = pairs * LANES
    n_grp = d // width
    heads = 2 * pairs
    return pl.pallas_call(
        functools.partial(_moba_body, n_blk=n_blk, pairs=pairs),
        grid=(bsz, n_grp, n_blk),
        in_specs=[pl.BlockSpec((width, blk), lambda b, p, q: (p, b * n_blk + q)),
                  pl.BlockSpec((width, t_len), lambda b, p, q: (n_grp + p, b)),
                  pl.BlockSpec((t_len, width), lambda b, p, q: (b, p))],
        out_specs=pl.BlockSpec((width, blk), lambda b, p, q: (p, b * n_blk + q)),
        out_shape=jax.ShapeDtypeStruct((d, bsz * t_len), F32),
        scratch_shapes=[pltpu.VMEM((n_blk, width), F32),
                        pltpu.VMEM((heads, MOBA_HEAD_DIM + BF16_ROWS, t_len), BF16),
                        pltpu.VMEM((heads, n_blk, blk), F32)],
        compiler_params=_cparams("parallel", "parallel", "arbitrary"),
        name="moba_attention",
    )(qvt, qvt, k)


def _moba_mixer(x, bsz, t_len, w_qkv, w_o, ln_g, ln_b):
    d = x.shape[1]
    assert t_len % MOBA_BLOCK == 0 and d == MOBA_HEADS * MOBA_HEAD_DIM
    w_k = w_qkv[:, d:2 * d].astype(BF16)
    w_qv_t = jnp.concatenate([w_qkv[:, :d], w_qkv[:, 2 * d:]], axis=1).T.astype(BF16)
    k = _matmul(x, w_k, tm=PROJ_ROWS, tn=PROJ_COLS, out_dtype=BF16)
    qvt = _matmul_nt(x, w_qv_t, tm=PROJ_ROWS, tn=PROJ_COLS, out_dtype=BF16)
    ot = _moba_attention(qvt, k, bsz, t_len, pairs=MOBA_HEAD_PAIRS)
    return _matmul_residual_ln(ot, w_o.astype(BF16), x, ln_g, ln_b, tm=OUT_PROJ_ROWS, lhs_transposed=True)


def kernel(x, rwkv_mix, rwkv_w_rkv, rwkv_w0, rwkv_w1, rwkv_w2, rwkv_a0, rwkv_a1, rwkv_a2, rwkv_g1, rwkv_g2,
           rwkv_k_k, rwkv_k_a, rwkv_r_k, rwkv_gn_g, rwkv_gn_b, rwkv_w_o, ret_w_in, ret_gn_g, ret_gn_b, ret_w_o,
           moba_w_qkv, moba_w_o, ffn_w_in, ffn_conv_w, ffn_conv_b, ffn_w_out, ln1_g, ln1_b, ln2_g, ln2_b):
    bsz, t_len, d = x.shape
    h = x.reshape(bsz * t_len, d)
    ffn_w_in_b = ffn_w_in.astype(BF16)
    ffn_w_out_b = ffn_w_out.astype(BF16)
    for i in range(DEPTH):
        kind, j = i % N_MIXERS, i // N_MIXERS
        if kind == 0:
            h = _rwkv_mixer(h, bsz, t_len, rwkv_mix[j], rwkv_w_rkv[j], rwkv_w0[j], rwkv_w1[j], rwkv_w2[j],
                            rwkv_a0[j], rwkv_a1[j], rwkv_a2[j], rwkv_g1[j], rwkv_g2[j], rwkv_k_k[j],
                            rwkv_k_a[j], rwkv_r_k[j], rwkv_gn_g[j], rwkv_gn_b[j], rwkv_w_o[j],
                            ln1_g[i], ln1_b[i])
        elif kind == 1:
            h = _retention_mixer(h, bsz, t_len, ret_w_in[j], ret_gn_g[j], ret_gn_b[j], ret_w_o[j],
                                 ln1_g[i], ln1_b[i])
        else:
            h = _moba_mixer(h, bsz, t_len, moba_w_qkv[j], moba_w_o[j], ln1_g[i], ln1_b[i])
        h = _conv_ffn_ln(h, t_len, i, ffn_w_in_b, ffn_conv_w, ffn_conv_b, ffn_w_out_b, ln2_g, ln2_b,
                         tm=FFN_ROWS, cols=FFN_COL_SLICE)
    return h.reshape(bsz, t_len, d)
```

```python
import functools
import math

import jax
import jax.numpy as jnp
from jax import lax
from jax.experimental import pallas as pl
from jax.experimental.pallas import tpu as pltpu

F32 = jnp.float32
BF16 = jnp.bfloat16
HIGHEST = lax.Precision.HIGHEST

DEPTH = 4
N_MIXERS = 3
RWKV_HEAD = 64
RWKV_GN_EPS = 64e-5
RET_HEADS = 4
RET_ROPE_BASE = 10000.0
RET_GN_EPS = 1e-5
MOBA_HEADS = 16
MOBA_HEAD_DIM = 64
MOBA_BLOCK = 256
MOBA_TOPK = 3
LN_EPS = 1e-5
DEEPNORM_ALPHA = (2 * DEPTH) ** 0.25

LANES = 128
SUBLANES = 8
BF16_ROWS = 16
VMEM_LIMIT_BYTES = 52 * 1024 * 1024

NEG_BIG = -1e30
LOG2_E = math.log2(math.e)

PROJ_ROWS = 512
PROJ_COLS = 512
OUT_PROJ_ROWS = 1024
FFN_ROWS = 512
FFN_COL_SLICE = 256
RWKV_PROJ_ROWS = 512
RWKV_HEAD_PAIRS = 8
RWKV_SUBCHUNKS = 2
RET_CHUNK = 256
MOBA_HEAD_PAIRS = 4
STREAM_BUFFERS = 3

_NT = (((1,), (1,)), ((), ()))
_TN = (((0,), (0,)), ((), ()))


def _cparams(*sem):
    return pltpu.CompilerParams(dimension_semantics=sem, vmem_limit_bytes=VMEM_LIMIT_BYTES)


def _sigmoid(x):
    return 1.0 / (1.0 + jnp.exp(-x))


def _layer_norm_rows(y, g, b):
    mu = jnp.mean(y, axis=-1, keepdims=True)
    d = y - mu
    var = jnp.mean(d * d, axis=-1, keepdims=True)
    return d * lax.rsqrt(var + LN_EPS) * g + b


def _mm_body(x_ref, w_ref, o_ref, xb_ref, *, tn, transposed_out):
    xb_ref[...] = x_ref[...].astype(BF16)
    n = w_ref.shape[0] if transposed_out else w_ref.shape[1]
    for c in range(n // tn):
        sl = slice(c * tn, (c + 1) * tn)
        if transposed_out:
            o_ref[sl, :] = lax.dot_general(w_ref[sl, :], xb_ref[...], _NT,
                                           preferred_element_type=F32).astype(o_ref.dtype)
        else:
            o_ref[:, sl] = jnp.dot(xb_ref[...], w_ref[:, sl], preferred_element_type=F32).astype(o_ref.dtype)


def _matmul(x, w, *, tm, tn, out_dtype=F32):
    m, k = x.shape
    n = w.shape[1]
    return pl.pallas_call(
        functools.partial(_mm_body, tn=tn, transposed_out=False),
        grid=(m // tm,),
        in_specs=[pl.BlockSpec((tm, k), lambda i: (i, 0)),
                  pl.BlockSpec((k, n), lambda i: (0, 0), pipeline_mode=pl.Buffered(1))],
        out_specs=pl.BlockSpec((tm, n), lambda i: (i, 0)),
        out_shape=jax.ShapeDtypeStruct((m, n), out_dtype),
        scratch_shapes=[pltpu.VMEM((tm, k), BF16)],
        compiler_params=_cparams("parallel"),
        name="matmul",
    )(x, w)


def _matmul_nt(x, wt, *, tm, tn, out_dtype=F32):
    m, k = x.shape
    n = wt.shape[0]
    return pl.pallas_call(
        functools.partial(_mm_body, tn=tn, transposed_out=True),
        grid=(m // tm,),
        in_specs=[pl.BlockSpec((tm, k), lambda i: (i, 0)),
                  pl.BlockSpec((n, k), lambda i: (0, 0), pipeline_mode=pl.Buffered(1))],
        out_specs=pl.BlockSpec((n, tm), lambda i: (0, i)),
        out_shape=jax.ShapeDtypeStruct((n, m), out_dtype),
        scratch_shapes=[pltpu.VMEM((tm, k), BF16)],
        compiler_params=_cparams("parallel"),
        name="matmul_nt",
    )(x, wt)


def _mm_res_ln_body(o_ref, w_ref, x_ref, g_ref, b_ref, out_ref, *, lhs_transposed):
    lhs = o_ref[...].astype(BF16)
    if lhs_transposed:
        acc = lax.dot_general(lhs, w_ref[...], _TN, preferred_element_type=F32)
    else:
        acc = jnp.dot(lhs, w_ref[...], preferred_element_type=F32)
    y = DEEPNORM_ALPHA * x_ref[...] + acc
    out_ref[...] = _layer_norm_rows(y, g_ref[...], b_ref[...])


def _mm_res_ln_pipelined(o_hbm, w_ref, x_hbm, g_ref, b_ref, out_hbm, *, tm, lhs_transposed):
    m, d = x_hbm.shape
    k = w_ref.shape[0]
    streamed = pl.Buffered(STREAM_BUFFERS)
    if lhs_transposed:
        o_spec = pl.BlockSpec((k, tm), lambda i: (0, i), pipeline_mode=streamed)
    else:
        o_spec = pl.BlockSpec((tm, k), lambda i: (i, 0), pipeline_mode=streamed)

    def step(o_ref, x_ref, out_ref):
        _mm_res_ln_body(o_ref, w_ref, x_ref, g_ref, b_ref, out_ref, lhs_transposed=lhs_transposed)

    pltpu.emit_pipeline(
        step,
        grid=(m // tm,),
        in_specs=[o_spec, pl.BlockSpec((tm, d), lambda i: (i, 0), pipeline_mode=streamed)],
        out_specs=[pl.BlockSpec((tm, d), lambda i: (i, 0))],
    )(o_hbm, x_hbm, out_hbm)


def _matmul_residual_ln(o, w, x, g, b, *, tm, lhs_transposed=False):
    m, d = x.shape
    in_vmem = pl.BlockSpec(memory_space=pltpu.VMEM)
    in_hbm = pl.BlockSpec(memory_space=pl.ANY)
    return pl.pallas_call(
        functools.partial(_mm_res_ln_pipelined, tm=tm, lhs_transposed=lhs_transposed),
        in_specs=[in_hbm, in_vmem, in_hbm, in_vmem, in_vmem],
        out_specs=in_hbm,
        out_shape=jax.ShapeDtypeStruct((m, d), F32),
        compiler_params=pltpu.CompilerParams(vmem_limit_bytes=VMEM_LIMIT_BYTES),
        name="out_proj_residual_ln",
    )(o, w, x, g.reshape(1, d), b.reshape(1, d))


def _ffn_body(x_ref, xh_ref, w_in_ref, cw_ref, cb_ref, wo_ref, g_ref, b_ref,
              out_ref, xb_ref, xp_ref, hu_ref, hg_ref, act_ref, *, tm, tiles_per_seq, ff, cols):
    i = pl.program_id(0)
    d = x_ref.shape[1]
    halo = BF16_ROWS
    grp = SUBLANES
    n_grp = tm // grp
    first = (i % tiles_per_seq) == 0
    xb_ref[0:halo, :] = jnp.where(first, 0.0, xh_ref[...]).astype(BF16)
    xp = x_ref[...].reshape(grp, n_grp, d).swapaxes(0, 1).reshape(tm, d)
    xp_ref[...] = xp
    xb_ref[halo:, :] = xp.astype(BF16)
    sub = lax.broadcasted_iota(jnp.int32, (grp, cols), 0)

    def hidden(h_ref, c):
        h = jnp.dot(xb_ref[...], w_in_ref[:, c], preferred_element_type=F32)
        h_ref[halo:, :] = h[halo:]
        for back in (1, 2):
            last = h[halo + tm - back * grp:halo + tm - (back - 1) * grp]
            prev = jnp.where(sub == 0, h[halo - back:halo - back + 1], pltpu.roll(last, 1, 0))
            h_ref[halo - back * grp:halo - (back - 1) * grp, :] = prev

    def conv(h_ref, c):
        return (cw_ref[0:1, c] * h_ref[pl.ds(halo - 2 * grp, tm), :]
                + cw_ref[1:2, c] * h_ref[pl.ds(halo - grp, tm), :]
                + cw_ref[2:3, c] * h_ref[pl.ds(halo, tm), :]
                + cb_ref[:, c])

    for j in range(ff // cols):
        cu = slice(j * cols, (j + 1) * cols)
        cg = slice(ff + j * cols, ff + (j + 1) * cols)
        hu = hu_ref.at[j % 2]
        hg = hg_ref.at[j % 2]
        hidden(hu, cu)
        hidden(hg, cg)
        u = conv(hu, cu)
        gt = conv(hg, cg)
        act_ref[:, cu] = ((gt * _sigmoid(gt)) * u).astype(BF16)

    halves = 2
    gp = n_grp // halves
    for q in range(halves):
        rs = slice(q * gp * grp, (q + 1) * gp * grp)
        y = DEEPNORM_ALPHA * xp_ref[rs, :] + jnp.dot(act_ref[rs, :], wo_ref[...], preferred_element_type=F32)
        t = _layer_norm_rows(y, g_ref[...], b_ref[...]).reshape(gp, grp, d).swapaxes(0, 1)
        for a in range(grp):
            out_ref[a * n_grp + q * gp:a * n_grp + (q + 1) * gp, :] = t[a]


def _conv_ffn_ln(x, seq_len, layer, w_in, conv_w, conv_b, w_out, g, b, *, tm, cols):
    m, d = x.shape
    ff = w_out.shape[1]
    halo = BF16_ROWS
    body = functools.partial(_ffn_body, tm=tm, tiles_per_seq=seq_len // tm, ff=ff, cols=cols)
    resident = lambda shape: pl.BlockSpec((None,) + shape, lambda i: (layer, 0, 0),
                                          pipeline_mode=pl.Buffered(1))
    n_layers = w_in.shape[0]
    conv_b = conv_b.reshape(n_layers, 1, 2 * ff)
    g = g.reshape(n_layers, 1, d)
    b = b.reshape(n_layers, 1, d)
    return pl.pallas_call(
        body,
        grid=(m // tm,),
        in_specs=[pl.BlockSpec((tm, d), lambda i: (i, 0)),
                  pl.BlockSpec((halo, d), lambda i: (jnp.maximum(i * (tm // halo) - 1, 0), 0)),
                  resident((d, 2 * ff)),
                  resident((3, 2 * ff)),
                  resident((1, 2 * ff)),
                  resident((ff, d)),
                  resident((1, d)),
                  resident((1, d))],
        out_specs=pl.BlockSpec((tm, d), lambda i: (i, 0)),
        out_shape=jax.ShapeDtypeStruct((m, d), F32),
        scratch_shapes=[pltpu.VMEM((tm + halo, d), BF16),
                        pltpu.VMEM((tm, d), F32),
                        pltpu.VMEM((2, tm + halo, cols), F32),
                        pltpu.VMEM((2, tm + halo, cols), F32),
                        pltpu.VMEM((tm, ff), BF16)],
        compiler_params=_cparams("parallel"),
        name="conv_ffn_ln",
    )(x, x, w_in, conv_w, conv_b, w_out, g, b)


def _head_pair_sum(v):
    head1 = lax.broadcasted_iota(jnp.int32, v.shape, 1) >= RWKV_HEAD
    total = jnp.sum(v, axis=1, keepdims=True)
    right = jnp.sum(jnp.where(head1, v, 0.0), axis=1, keepdims=True)
    return jnp.where(head1, right, total - right)


def _rwkv_proj_body(x_ref, xp_ref, mix_ref, wrkv_ref, w1_ref, w2_ref, a1_ref, a2_ref, g1_ref, g2_ref,
                    w0_ref, a0_ref, kk_ref, ka_ref,
                    r_out, ld_out, k_out, v_out, kk_out, b_out, gate_out, *, tm, tiles_per_seq):
    i = pl.program_id(0)
    d = x_ref.shape[1]
    x = x_ref[...]
    first = (i % tiles_per_seq) == 0
    prev = jnp.where(first, 0.0, xp_ref[SUBLANES - 1:SUBLANES, :])
    row = lax.broadcasted_iota(jnp.int32, (tm, d), 0)
    xs = jnp.where(row == 0, prev, pltpu.roll(x, 1, 0))
    xx = xs - x

    x_b = x.astype(BF16)
    xx_b = xx.astype(BF16)

    def mixed(j):
        return x_b + xx_b * mix_ref[j:j + 1, :].astype(BF16)

    def mm(a, w):
        return jnp.dot(a, w, preferred_element_type=F32)

    r = mm(mixed(0), wrkv_ref[0])
    k = mm(mixed(2), wrkv_ref[1])
    v = mm(mixed(3), wrkv_ref[2])
    lw = w0_ref[...] + mm(jnp.tanh(mm(mixed(1), w1_ref[...])).astype(BF16), w2_ref[...])
    softplus_neg = jnp.maximum(-lw, 0.0) + jnp.log(1.0 + jnp.exp(-jnp.abs(lw)))
    log_decay = -jnp.exp(-softplus_neg - 0.5)
    a = _sigmoid(a0_ref[...] + mm(mm(mixed(4), a1_ref[...]).astype(BF16), a2_ref[...]))
    gate = mm(_sigmoid(mm(mixed(5), g1_ref[...])).astype(BF16), g2_ref[...])

    kk = k * kk_ref[...]
    sq = kk * kk
    ss = jnp.concatenate([_head_pair_sum(sq[:, j * LANES:(j + 1) * LANES]) for j in range(d // LANES)], axis=1)
    kk = kk / jnp.maximum(jnp.sqrt(ss), 1e-12)

    r_out[...] = r
    ld_out[...] = log_decay
    k_out[...] = k * (1.0 + (a - 1.0) * ka_ref[...])
    v_out[...] = v
    kk_out[...] = kk
    b_out[...] = kk * a
    gate_out[...] = gate


def _rwkv_proj(x, seq_len, mix, w_rkv, w0, w1, w2, a0, a1, a2, g1, g2, k_k, k_a, *, tm):
    m, d = x.shape
    full = lambda arr: pl.BlockSpec(arr.shape, lambda i: (0,) * arr.ndim, pipeline_mode=pl.Buffered(1))
    vec = lambda a: a.reshape(1, d)
    args = (mix, w_rkv.astype(BF16), w1.astype(BF16), w2.astype(BF16), a1.astype(BF16), a2.astype(BF16),
            g1.astype(BF16), g2.astype(BF16), vec(w0), vec(a0), vec(k_k), vec(k_a))
    row_spec = pl.BlockSpec((tm, d), lambda i: (i, 0))
    return pl.pallas_call(
        functools.partial(_rwkv_proj_body, tm=tm, tiles_per_seq=seq_len // tm),
        grid=(m // tm,),
        in_specs=[row_spec,
                  pl.BlockSpec((SUBLANES, d), lambda i: (jnp.maximum(i * (tm // SUBLANES) - 1, 0), 0))]
                 + [full(a) for a in args],
        out_specs=[row_spec] * 7,
        out_shape=[jax.ShapeDtypeStruct((m, d), F32)] * 7,
        compiler_params=_cparams("parallel"),
        name="rwkv_proj",
    )(x, x, *args)


def _bf16_dot(a, b, dims):
    return lax.dot_general(a.astype(BF16), b.astype(BF16), dims, preferred_element_type=F32)


def _head_sums(tiles):
    return [_head_pair_sum(t) for t in tiles]


def _rwkv_chunk_body(r_ref, ld_ref, k_ref, v_ref, kk_ref, b_ref, gate_ref, rk_ref, gg_ref, gb_ref,
                     o_ref, st_ref, *, chunk, pairs, subs):
    c = pl.program_id(2)

    @pl.when(c == 0)
    def _():
        st_ref[...] = jnp.zeros_like(st_ref)

    n = RWKV_HEAD
    row = lax.broadcasted_iota(jnp.int32, (chunk, LANES), 0)
    lane = lax.broadcasted_iota(jnp.int32, (chunk, LANES), 1)
    col = lane % n
    eye2 = jnp.where(row == col, 1.0, 0.0)
    head1 = lane >= n
    row2 = lax.broadcasted_iota(jnp.int32, (2 * chunk, LANES), 0)
    col2 = lax.broadcasted_iota(jnp.int32, (2 * chunk, LANES), 1) % n
    tri2 = jnp.where(row2 < chunk, row2 - 1, row2 - chunk) >= col2
    rr = lax.broadcasted_iota(jnp.int32, (LANES, LANES), 0) // n
    cc = lax.broadcasted_iota(jnp.int32, (LANES, LANES), 1) // n
    bd = rr == cc
    nn = (((1,), (0,)), ((), ()))

    def head_stack(t):
        return jnp.concatenate([jnp.where(head1, 0.0, t), jnp.where(head1, t, 0.0)], axis=0)

    def block_diag(t):
        return jnp.where(bd, jnp.concatenate([t, t], axis=0), 0.0)

    units = [(cc, p) for cc in range(subs) for p in range(pairs)]
    U = range(len(units))
    rows = [slice(cc * chunk, (cc + 1) * chunk) for cc, _ in units]
    sls = [slice(p * LANES, (p + 1) * LANES) for _, p in units]
    r = [r_ref[0, rows[i], sls[i]] for i in U]
    ld = [ld_ref[0, rows[i], sls[i]] for i in U]
    k = [k_ref[0, rows[i], sls[i]] for i in U]
    v = [v_ref[0, rows[i], sls[i]] for i in U]
    bv = [b_ref[0, rows[i], sls[i]] for i in U]

    cum = list(ld)
    shift = 1
    while shift < chunk:
        cum = [cu + jnp.where(row >= shift, pltpu.roll(cu, shift, 0), 0.0) for cu in cum]
        shift *= 2
    cum_last = [cu[chunk - 1:chunk, :] for cu in cum]
    e_inv = [jnp.exp(-cu) for cu in cum]
    r_hat = [r[i] * jnp.exp(cum[i]) for i in U]
    a_hat = [-(kk_ref[0, rows[i], sls[i]] * jnp.exp(cum[i] - ld[i])) for i in U]
    lhs = [jnp.concatenate([a_hat[i], r_hat[i]], axis=0) for i in U]
    sbk = [_bf16_dot(lhs[i], jnp.concatenate([head_stack(bv[i] * e_inv[i]), head_stack(k[i] * e_inv[i])], axis=0),
                     _NT) for i in U]
    sb = [jnp.where(tri2, t[:, :LANES], 0.0) for t in sbk]
    sk = [jnp.where(tri2, t[:, LANES:], 0.0) for t in sbk]
    a_ab = [t[:chunk] for t in sb]
    a_rb = [t[chunk:] for t in sb]

    xp = [_bf16_dot(t, block_diag(t), nn) for t in a_ab]
    tinv = [eye2 + t for t in a_ab]
    power = 2
    while 2 * power < chunk:
        both = [_bf16_dot(jnp.concatenate([xp[i], tinv[i]], axis=0), block_diag(xp[i]), nn) for i in U]
        xp = [t[:chunk] for t in both]
        tinv = [tinv[i] + both[i][chunk:] for i in U]
        power *= 2
    tinv = [tinv[i] + _bf16_dot(tinv[i], block_diag(xp[i]), nn) for i in U]

    state = [st_ref[p] for p in range(pairs)]
    y = [None] * len(units)
    for cc in range(subs):
        ids = [cc * pairs + p for p in range(pairs)]
        zy = [_bf16_dot(jnp.concatenate([lhs[i], sk[i]], axis=1),
                        jnp.concatenate([state[p].T, head_stack(v[i])], axis=0), nn)
              for p, i in enumerate(ids)]
        u = [_bf16_dot(tinv[i], head_stack(zy[p][:chunk]), nn) for p, i in enumerate(ids)]
        for p, i in enumerate(ids):
            y[i] = zy[p][chunk:] + _bf16_dot(a_rb[i], head_stack(u[p]), nn)
            e_last = jnp.exp(cum_last[i] - cum[i])
            s_new = state[p] * jnp.exp(cum_last[i]) + _bf16_dot(
                jnp.concatenate([u[p], v[i]], axis=0),
                jnp.concatenate([bv[i] * e_last, k[i] * e_last], axis=0), _TN)
            state[p] = jnp.where(bd, s_new, 0.0)
    for p in range(pairs):
        st_ref[p] = state[p]

    sums = _head_sums(y + [r[i] * k[i] * rk_ref[:, sls[i]] for i in U])
    dlt = [y[i] - sums[i] * (1.0 / n) for i in U]
    var = _head_sums([t * t for t in dlt])
    for i in U:
        yn = dlt[i] * lax.rsqrt(var[i] * (1.0 / n) + RWKV_GN_EPS) * gg_ref[:, sls[i]] + gb_ref[:, sls[i]]
        bonus = sums[len(units) + i] * v[i]
        o_ref[0, rows[i], sls[i]] = ((yn + bonus) * gate_ref[0, rows[i], sls[i]]).astype(o_ref.dtype)


def _rwkv_chunk(r, ld, k, v, kk, bvec, gate, r_k, gn_g, gn_b, *, pairs, subs):
    bsz, t_len, d = r.shape
    chunk = RWKV_HEAD
    width = pairs * LANES
    seq_spec = pl.BlockSpec((1, subs * chunk, width), lambda b, g, c: (b, c, g))
    vec_spec = pl.BlockSpec((1, width), lambda b, g, c: (0, g))
    return pl.pallas_call(
        functools.partial(_rwkv_chunk_body, chunk=chunk, pairs=pairs, subs=subs),
        grid=(bsz, d // width, t_len // (subs * chunk)),
        in_specs=[seq_spec] * 7 + [vec_spec] * 3,
        out_specs=seq_spec,
        out_shape=jax.ShapeDtypeStruct((bsz, t_len, d), BF16),
        scratch_shapes=[pltpu.VMEM((pairs, LANES, LANES), F32)],
        compiler_params=_cparams("parallel", "parallel", "arbitrary"),
        name="rwkv_chunk",
    )(r, ld, k, v, kk, bvec, gate, r_k.reshape(1, d), gn_g.reshape(1, d), gn_b.reshape(1, d))


def _rwkv_mixer(x, bsz, t_len, mix, w_rkv, w0, w1, w2, a0, a1, a2, g1, g2, k_k, k_a, r_k, gn_g, gn_b, w_o,
                ln_g, ln_b):
    d = x.shape[1]
    outs = _rwkv_proj(x, t_len, mix, w_rkv, w0, w1, w2, a0, a1, a2, g1, g2, k_k, k_a, tm=RWKV_PROJ_ROWS)
    seq = [o.reshape(bsz, t_len, d) for o in outs]
    o = _rwkv_chunk(*seq, r_k, gn_g, gn_b, pairs=RWKV_HEAD_PAIRS, subs=RWKV_SUBCHUNKS)
    return _matmul_residual_ln(o.reshape(bsz * t_len, d), w_o.astype(BF16), x, ln_g, ln_b, tm=OUT_PROJ_ROWS)


def _ret_chunk_body(q_ref, k_ref, v_ref, gate_ref, cos_ref, sin_ref, g_ref, b_ref, o_ref, st_ref, *, chunk):
    c = pl.program_id(1)

    @pl.when(c == 0)
    def _():
        st_ref[...] = jnp.zeros_like(st_ref)

    n_heads, dk, dv = st_ref.shape
    half = dk // 2
    cos = cos_ref[...]
    sin = sin_ref[...]

    def rotate(t):
        te, to = t[:, :half], t[:, half:]
        return jnp.concatenate([te * cos - to * sin, to * cos + te * sin], axis=1)

    ri = lax.broadcasted_iota(jnp.int32, (chunk, chunk), 0)
    ci = lax.broadcasted_iota(jnp.int32, (chunk, chunk), 1)
    rel = (ri - ci).astype(F32)
    pos = lax.broadcasted_iota(jnp.int32, (chunk, dk), 0).astype(F32)

    for h in range(n_heads):
        log_gamma = math.log(1.0 - 2.0 ** (-5.0 - h))
        inner = jnp.where(rel >= 0, jnp.exp(log_gamma * jnp.maximum(rel, 0.0)), 0.0)
        cross = jnp.exp(log_gamma * (pos + 1.0))
        sdecay = jnp.exp(log_gamma * (chunk - 1.0 - pos))
        chunk_decay = jnp.exp(log_gamma * chunk)

        q = rotate(q_ref[0, :, h * dk:(h + 1) * dk].astype(F32))
        k = rotate(k_ref[0, :, h * dk:(h + 1) * dk].astype(F32)) * (dk ** -0.5)
        v = v_ref[0, :, h * dv:(h + 1) * dv].astype(BF16)
        s = lax.dot_general(q.astype(BF16), k.astype(BF16), _NT, preferred_element_type=F32) * inner
        st = st_ref[h]
        o = (jnp.dot(s.astype(BF16), v, preferred_element_type=F32)
             + jnp.dot((q * cross).astype(BF16), st.astype(BF16), preferred_element_type=F32))
        st_ref[h] = st * chunk_decay + lax.dot_general((k * sdecay).astype(BF16), v, _TN,
                                                       preferred_element_type=F32)

        mu = jnp.mean(o, axis=-1, keepdims=True)
        dlt = o - mu
        var = jnp.mean(dlt * dlt, axis=-1, keepdims=True)
        vs = slice(h * dv, (h + 1) * dv)
        on = dlt * lax.rsqrt(var + RET_GN_EPS) * g_ref[:, vs] + b_ref[:, vs]
        gt = gate_ref[0, :, vs].astype(F32)
        o_ref[0, :, vs] = (gt * _sigmoid(gt) * on).astype(o_ref.dtype)


def _ret_chunk(proj, cos, sin, gn_g, gn_b, *, chunk):
    bsz, t_len, six_d = proj.shape
    d = six_d // 6
    h = RET_HEADS
    dk, dv = d // h, 2 * d // h
    return pl.pallas_call(
        functools.partial(_ret_chunk_body, chunk=chunk),
        grid=(bsz, t_len // chunk),
        in_specs=[pl.BlockSpec((1, chunk, d), lambda b, c: (b, c, 0)),
                  pl.BlockSpec((1, chunk, d), lambda b, c: (b, c, 1)),
                  pl.BlockSpec((1, chunk, 2 * d), lambda b, c: (b, c, 1)),
                  pl.BlockSpec((1, chunk, 2 * d), lambda b, c: (b, c, 2)),
                  pl.BlockSpec((chunk, dk // 2), lambda b, c: (c, 0)),
                  pl.BlockSpec((chunk, dk // 2), lambda b, c: (c, 0)),
                  pl.BlockSpec((1, 2 * d), lambda b, c: (0, 0)),
                  pl.BlockSpec((1, 2 * d), lambda b, c: (0, 0))],
        out_specs=pl.BlockSpec((1, chunk, 2 * d), lambda b, c: (b, c, 0)),
        out_shape=jax.ShapeDtypeStruct((bsz, t_len, 2 * d), BF16),
        scratch_shapes=[pltpu.VMEM((h, dk, dv), F32)],
        compiler_params=_cparams("parallel", "arbitrary"),
        name="retention_chunk",
    )(proj, proj, proj, proj, cos, sin, gn_g.reshape(1, 2 * d), gn_b.reshape(1, 2 * d))


def _retention_mixer(x, bsz, t_len, w_in, gn_g, gn_b, w_o, ln_g, ln_b):
    d = x.shape[1]
    h = RET_HEADS
    dk = d // h
    w_qk = w_in[:, :2 * d].reshape(d, 2 * h, dk // 2, 2).transpose(0, 1, 3, 2).reshape(d, 2 * d)
    w_perm = jnp.concatenate([w_qk, w_in[:, 2 * d:]], axis=1).astype(BF16)
    proj = _matmul(x, w_perm, tm=PROJ_ROWS, tn=PROJ_COLS, out_dtype=BF16)
    inv = 1.0 / (RET_ROPE_BASE ** jnp.linspace(0.0, 1.0, dk // 2, dtype=F32))
    ang = jnp.arange(t_len, dtype=F32)[:, None] * inv[None, :]
    o = _ret_chunk(proj.reshape(bsz, t_len, 6 * d), jnp.cos(ang), jnp.sin(ang), gn_g, gn_b, chunk=RET_CHUNK)
    return _matmul_residual_ln(o.reshape(bsz * t_len, 2 * d), w_o.astype(BF16), x, ln_g, ln_b,
                               tm=OUT_PROJ_ROWS)


def _moba_body(qt_ref, vt_ref, k_ref, o_ref, kmean_ref, va_ref, bias_ref, *, n_blk, pairs):
    qb = pl.program_id(2)
    blk = MOBA_BLOCK
    hd = MOBA_HEAD_DIM
    heads = 2 * pairs
    aug = hd + BF16_ROWS
    t_len = n_blk * blk
    H = range(heads)

    @pl.when(qb == 0)
    def _():
        kf = k_ref[...].astype(F32)
        kmean_ref[...] = jnp.mean(kf.reshape(n_blk, blk, pairs * LANES), axis=1)
        ones = jnp.ones((BF16_ROWS, t_len), BF16)
        for h in H:
            va_ref[h, 0:hd, :] = vt_ref[h * hd:(h + 1) * hd, :]
            va_ref[h, hd:aug, :] = ones

    kmean = kmean_ref[...]
    qt = qt_ref[...].astype(F32)
    zeros = jnp.zeros((hd, blk), F32)
    blk_id = lax.broadcasted_iota(jnp.int32, (n_blk, blk), 0)
    scale = hd ** -0.5

    qz_b = []
    for h in H:
        qh = qt[h * hd:(h + 1) * hd]
        qz = jnp.concatenate([qh, zeros] if h % 2 == 0 else [zeros, qh], axis=0)
        lanes = slice((h // 2) * LANES, (h // 2 + 1) * LANES)
        gate = jnp.dot(kmean[:, lanes], qz, precision=HIGHEST, preferred_element_type=F32)
        beaten = jnp.zeros((n_blk, blk), F32)
        for m in range(n_blk):
            gm = gate[m:m + 1, :]
            wins = jnp.where(gm > gate, 1.0, jnp.where(gm == gate, jnp.where(blk_id > m, 1.0, 0.0), 0.0))
            beaten = beaten + jnp.where(m < qb, wins, 0.0)
        bias_ref[h] = jnp.where(blk_id < qb, jnp.where(beaten < MOBA_TOPK, 0.0, NEG_BIG), NEG_BIG)
        qz_b.append((qz * (scale * LOG2_E)).astype(BF16))

    def scores(off):
        kb = k_ref[pl.ds(off, blk), :]
        return tuple(jnp.dot(kb[:, (h // 2) * LANES:(h // 2 + 1) * LANES], qz_b[h],
                             preferred_element_type=F32) for h in H)

    def accumulate(off, s, m_new, shift, carry):
        p = [jnp.exp2(s[h] - shift[h]).astype(BF16) for h in H]
        alpha = [jnp.exp2(carry[h][0] - m_new[h]) for h in H]
        acc = [alpha[h] * carry[h][1]
               + jnp.dot(va_ref[h, :, pl.ds(off, blk)], p[h], preferred_element_type=F32) for h in H]
        return tuple((m_new[h], acc[h]) for h in H)

    def attend(off, s, carry):
        m_new = [jnp.maximum(carry[h][0], jnp.max(s[h], axis=0, keepdims=True)) for h in H]
        return accumulate(off, s, m_new, m_new, carry)

    def past_block(nb, carry):
        off = pl.multiple_of(nb * blk, blk)
        s = scores(off)
        chosen = [bias_ref[h, pl.ds(nb, 1), :] >= 0.0 for h in H]
        m_new = [jnp.where(chosen[h], jnp.maximum(carry[h][0], jnp.max(s[h], axis=0, keepdims=True)),
                           carry[h][0]) for h in H]
        shift = [jnp.where(chosen[h], m_new[h], -NEG_BIG) for h in H]
        return accumulate(off, s, m_new, shift, carry)

    init = tuple((jnp.full((1, blk), NEG_BIG, F32), jnp.zeros((aug, blk), F32)) for _ in H)
    carry = lax.fori_loop(0, qb, past_block, init)
    key_i = lax.broadcasted_iota(jnp.int32, (blk, blk), 0)
    qry_i = lax.broadcasted_iota(jnp.int32, (blk, blk), 1)
    causal_bias = jnp.where(key_i <= qry_i, 0.0, NEG_BIG)
    own = pl.multiple_of(qb * blk, blk)
    s_own = scores(own)
    res = attend(own, [s_own[h] + causal_bias for h in H], carry)
    for h in H:
        acc = res[h][1]
        o_ref[h * hd:(h + 1) * hd, :] = acc[:hd] / acc[hd:hd + 1]


def _moba_attention(qvt, k, bsz, t_len, *, pairs):
    d = k.shape[1]
    blk = MOBA_BLOCK
    n_blk = t_len // blk
    width = pairs * LANES
    n_grp = d // width
    heads = 2 * pairs
    return pl.pallas_call(
        functools.partial(_moba_body, n_blk=n_blk, pairs=pairs),
        grid=(bsz, n_grp, n_blk),
        in_specs=[pl.BlockSpec((width, blk), lambda b, p, q: (p, b * n_blk + q)),
                  pl.BlockSpec((width, t_len), lambda b, p, q: (n_grp + p, b)),
                  pl.BlockSpec((t_len, width), lambda b, p, q: (b, p))],
        out_specs=pl.BlockSpec((width, blk), lambda b, p, q: (p, b * n_blk + q)),
        out_shape=jax.ShapeDtypeStruct((d, bsz * t_len), F32),
        scratch_shapes=[pltpu.VMEM((n_blk, width), F32),
                        pltpu.VMEM((heads, MOBA_HEAD_DIM + BF16_ROWS, t_len), BF16),
                        pltpu.VMEM((heads, n_blk, blk), F32)],
        compiler_params=_cparams("parallel", "parallel", "arbitrary"),
        name="moba_attention",
    )(qvt, qvt, k)


def _moba_mixer(x, bsz, t_len, w_qkv, w_o, ln_g, ln_b):
    d = x.shape[1]
    assert t_len % MOBA_BLOCK == 0 and d == MOBA_HEADS * MOBA_HEAD_DIM
    w_k = w_qkv[:, d:2 * d].astype(BF16)
    w_qv_t = jnp.concatenate([w_qkv[:, :d], w_qkv[:, 2 * d:]], axis=1).T.astype(BF16)
    k = _matmul(x, w_k, tm=PROJ_ROWS, tn=PROJ_COLS, out_dtype=BF16)
    qvt = _matmul_nt(x, w_qv_t, tm=PROJ_ROWS, tn=PROJ_COLS, out_dtype=BF16)
    ot = _moba_attention(qvt, k, bsz, t_len, pairs=MOBA_HEAD_PAIRS)
    return _matmul_residual_ln(ot, w_o.astype(BF16), x, ln_g, ln_b, tm=OUT_PROJ_ROWS, lhs_transposed=True)


def kernel(x, rwkv_mix, rwkv_w_rkv, rwkv_w0, rwkv_w1, rwkv_w2, rwkv_a0, rwkv_a1, rwkv_a2, rwkv_g1, rwkv_g2,
           rwkv_k_k, rwkv_k_a, rwkv_r_k, rwkv_gn_g, rwkv_gn_b, rwkv_w_o, ret_w_in, ret_gn_g, ret_gn_b, ret_w_o,
           moba_w_qkv, moba_w_o, ffn_w_in, ffn_conv_w, ffn_conv_b, ffn_w_out, ln1_g, ln1_b, ln2_g, ln2_b):
    bsz, t_len, d = x.shape
    h = x.reshape(bsz * t_len, d)
    ffn_w_in_b = ffn_w_in.astype(BF16)
    ffn_w_out_b = ffn_w_out.astype(BF16)
    for i in range(DEPTH):
        kind, j = i % N_MIXERS, i // N_MIXERS
        if kind == 0:
            h = _rwkv_mixer(h, bsz, t_len, rwkv_mix[j], rwkv_w_rkv[j], rwkv_w0[j], rwkv_w1[j], rwkv_w2[j],
                            rwkv_a0[j], rwkv_a1[j], rwkv_a2[j], rwkv_g1[j], rwkv_g2[j], rwkv_k_k[j],
                            rwkv_k_a[j], rwkv_r_k[j], rwkv_gn_g[j], rwkv_gn_b[j], rwkv_w_o[j],
                            ln1_g[i], ln1_b[i])
        elif kind == 1:
            h = _retention_mixer(h, bsz, t_len, ret_w_in[j], ret_gn_g[j], ret_gn_b[j], ret_w_o[j],
                                 ln1_g[i], ln1_b[i])
        else:
            h = _moba_mixer(h, bsz, t_len, moba_w_qkv[j], moba_w_o[j], ln1_g[i], ln1_b[i])
        h = _conv_ffn_ln(h, t_len, i, ffn_w_in_b, ffn_conv_w, ffn_conv_b, ffn_w_out_b, ln2_g, ln2_b,
                         tm=FFN_ROWS, cols=FFN_COL_SLICE)
    return h.reshape(bsz, t_len, d)
```

```python
import functools
import math

import jax
import jax.numpy as jnp
from jax import lax
from jax.experimental import pallas as pl
from jax.experimental.pallas import tpu as pltpu

F32 = jnp.float32
BF16 = jnp.bfloat16
HIGHEST = lax.Precision.HIGHEST

DEPTH = 4
N_MIXERS = 3
RWKV_HEAD = 64
RWKV_GN_EPS = 64e-5
RET_HEADS = 4
RET_ROPE_BASE = 10000.0
RET_GN_EPS = 1e-5
MOBA_HEADS = 16
MOBA_HEAD_DIM = 64
MOBA_BLOCK = 256
MOBA_TOPK = 3
LN_EPS = 1e-5
DEEPNORM_ALPHA = (2 * DEPTH) ** 0.25

LANES = 128
SUBLANES = 8
BF16_ROWS = 16
VMEM_LIMIT_BYTES = 52 * 1024 * 1024

NEG_BIG = -1e30
LOG2_E = math.log2(math.e)

PROJ_ROWS = 512
PROJ_COLS = 512
OUT_PROJ_ROWS = 1024
FFN_ROWS = 512
FFN_COL_SLICE = 256
RWKV_PROJ_ROWS = 512
RWKV_HEAD_PAIRS = 8
RWKV_SUBCHUNKS = 2
RET_CHUNK = 256
MOBA_HEAD_PAIRS = 4
STREAM_BUFFERS = 3

_NT = (((1,), (1,)), ((), ()))
_TN = (((0,), (0,)), ((), ()))


def _cparams(*sem):
    return pltpu.CompilerParams(dimension_semantics=sem, vmem_limit_bytes=VMEM_LIMIT_BYTES)


def _sigmoid(x):
    return 1.0 / (1.0 + jnp.exp(-x))


def _layer_norm_rows(y, g, b):
    mu = jnp.mean(y, axis=-1, keepdims=True)
    d = y - mu
    var = jnp.mean(d * d, axis=-1, keepdims=True)
    return d * lax.rsqrt(var + LN_EPS) * g + b


def _mm_body(x_ref, w_ref, o_ref, xb_ref, *, tn, transposed_out):
    xb_ref[...] = x_ref[...].astype(BF16)
    n = w_ref.shape[0] if transposed_out else w_ref.shape[1]
    for c in range(n // tn):
        sl = slice(c * tn, (c + 1) * tn)
        if transposed_out:
            o_ref[sl, :] = lax.dot_general(w_ref[sl, :], xb_ref[...], _NT,
                                           preferred_element_type=F32).astype(o_ref.dtype)
        else:
            o_ref[:, sl] = jnp.dot(xb_ref[...], w_ref[:, sl], preferred_element_type=F32).astype(o_ref.dtype)


def _mm_pipelined(x_hbm, w_ref, o_hbm, xb_ref, *, tm, tn, transposed_out):
    m, k = x_hbm.shape
    n = w_ref.shape[0] if transposed_out else w_ref.shape[1]

    def step(x_ref, o_ref):
        _mm_body(x_ref, w_ref, o_ref, xb_ref, tn=tn, transposed_out=transposed_out)

    o_spec = pl.BlockSpec((n, tm), lambda i: (0, i)) if transposed_out else pl.BlockSpec((tm, n), lambda i: (i, 0))
    pltpu.emit_pipeline(
        step,
        grid=(m // tm,),
        in_specs=[pl.BlockSpec((tm, k), lambda i: (i, 0), pipeline_mode=pl.Buffered(STREAM_BUFFERS))],
        out_specs=[o_spec],
    )(x_hbm, o_hbm)


def _projection(x, w, *, tm, tn, out_dtype, transposed_out, name):
    m, k = x.shape
    n = w.shape[0] if transposed_out else w.shape[1]
    return pl.pallas_call(
        functools.partial(_mm_pipelined, tm=tm, tn=tn, transposed_out=transposed_out),
        in_specs=[pl.BlockSpec(memory_space=pl.ANY), pl.BlockSpec(memory_space=pltpu.VMEM)],
        out_specs=pl.BlockSpec(memory_space=pl.ANY),
        out_shape=jax.ShapeDtypeStruct((n, m) if transposed_out else (m, n), out_dtype),
        scratch_shapes=[pltpu.VMEM((tm, k), BF16)],
        compiler_params=pltpu.CompilerParams(vmem_limit_bytes=VMEM_LIMIT_BYTES),
        name=name,
    )(x, w)


def _matmul(x, w, *, tm, tn, out_dtype=F32):
    return _projection(x, w, tm=tm, tn=tn, out_dtype=out_dtype, transposed_out=False, name="matmul")


def _matmul_nt(x, wt, *, tm, tn, out_dtype=F32):
    return _projection(x, wt, tm=tm, tn=tn, out_dtype=out_dtype, transposed_out=True, name="matmul_nt")


def _mm_res_ln_body(o_ref, w_ref, x_ref, g_ref, b_ref, out_ref, *, lhs_transposed):
    lhs = o_ref[...].astype(BF16)
    if lhs_transposed:
        acc = lax.dot_general(lhs, w_ref[...], _TN, preferred_element_type=F32)
    else:
        acc = jnp.dot(lhs, w_ref[...], preferred_element_type=F32)
    y = DEEPNORM_ALPHA * x_ref[...] + acc
    out_ref[...] = _layer_norm_rows(y, g_ref[...], b_ref[...])


def _mm_res_ln_pipelined(o_hbm, w_ref, x_hbm, g_ref, b_ref, out_hbm, *, tm, lhs_transposed):
    m, d = x_hbm.shape
    k = w_ref.shape[0]
    streamed = pl.Buffered(STREAM_BUFFERS)
    if lhs_transposed:
        o_spec = pl.BlockSpec((k, tm), lambda i: (0, i), pipeline_mode=streamed)
    else:
        o_spec = pl.BlockSpec((tm, k), lambda i: (i, 0), pipeline_mode=streamed)

    def step(o_ref, x_ref, out_ref):
        _mm_res_ln_body(o_ref, w_ref, x_ref, g_ref, b_ref, out_ref, lhs_transposed=lhs_transposed)

    pltpu.emit_pipeline(
        step,
        grid=(m // tm,),
        in_specs=[o_spec, pl.BlockSpec((tm, d), lambda i: (i, 0), pipeline_mode=streamed)],
        out_specs=[pl.BlockSpec((tm, d), lambda i: (i, 0))],
    )(o_hbm, x_hbm, out_hbm)


def _matmul_residual_ln(o, w, x, g, b, *, tm, lhs_transposed=False):
    m, d = x.shape
    in_vmem = pl.BlockSpec(memory_space=pltpu.VMEM)
    in_hbm = pl.BlockSpec(memory_space=pl.ANY)
    return pl.pallas_call(
        functools.partial(_mm_res_ln_pipelined, tm=tm, lhs_transposed=lhs_transposed),
        in_specs=[in_hbm, in_vmem, in_hbm, in_vmem, in_vmem],
        out_specs=in_hbm,
        out_shape=jax.ShapeDtypeStruct((m, d), F32),
        compiler_params=pltpu.CompilerParams(vmem_limit_bytes=VMEM_LIMIT_BYTES),
        name="out_proj_residual_ln",
    )(o, w, x, g.reshape(1, d), b.reshape(1, d))


def _ffn_body(x_ref, xh_ref, w_in_ref, cw_ref, cb_ref, wo_ref, g_ref, b_ref,
              out_ref, xb_ref, xp_ref, hu_ref, hg_ref, act_ref, *, tm, tiles_per_seq, ff, cols):
    i = pl.program_id(0)
    d = x_ref.shape[1]
    halo = BF16_ROWS
    grp = SUBLANES
    n_grp = tm // grp
    first = (i % tiles_per_seq) == 0
    xb_ref[0:halo, :] = jnp.where(first, 0.0, xh_ref[...]).astype(BF16)
    xp = x_ref[...].reshape(grp, n_grp, d).swapaxes(0, 1).reshape(tm, d)
    xp_ref[...] = xp
    xb_ref[halo:, :] = xp.astype(BF16)
    sub = lax.broadcasted_iota(jnp.int32, (grp, cols), 0)

    def hidden(h_ref, c):
        h = jnp.dot(xb_ref[...], w_in_ref[:, c], preferred_element_type=F32)
        h_ref[halo:, :] = h[halo:]
        for back in (1, 2):
            last = h[halo + tm - back * grp:halo + tm - (back - 1) * grp]
            prev = jnp.where(sub == 0, h[halo - back:halo - back + 1], pltpu.roll(last, 1, 0))
            h_ref[halo - back * grp:halo - (back - 1) * grp, :] = prev

    def conv(h_ref, c):
        return (cw_ref[0:1, c] * h_ref[pl.ds(halo - 2 * grp, tm), :]
                + cw_ref[1:2, c] * h_ref[pl.ds(halo - grp, tm), :]
                + cw_ref[2:3, c] * h_ref[pl.ds(halo, tm), :]
                + cb_ref[:, c])

    for j in range(ff // cols):
        cu = slice(j * cols, (j + 1) * cols)
        cg = slice(ff + j * cols, ff + (j + 1) * cols)
        hu = hu_ref.at[j % 2]
        hg = hg_ref.at[j % 2]
        hidden(hu, cu)
        hidden(hg, cg)
        u = conv(hu, cu)
        gt = conv(hg, cg)
        act_ref[:, cu] = ((gt * _sigmoid(gt)) * u).astype(BF16)

    halves = 2
    gp = n_grp // halves
    for q in range(halves):
        rs = slice(q * gp * grp, (q + 1) * gp * grp)
        y = DEEPNORM_ALPHA * xp_ref[rs, :] + jnp.dot(act_ref[rs, :], wo_ref[...], preferred_element_type=F32)
        t = _layer_norm_rows(y, g_ref[...], b_ref[...]).reshape(gp, grp, d).swapaxes(0, 1)
        for a in range(grp):
            out_ref[a * n_grp + q * gp:a * n_grp + (q + 1) * gp, :] = t[a]


def _conv_ffn_ln(x, seq_len, layer, w_in, conv_w, conv_b, w_out, g, b, *, tm, cols):
    m, d = x.shape
    ff = w_out.shape[1]
    halo = BF16_ROWS
    body = functools.partial(_ffn_body, tm=tm, tiles_per_seq=seq_len // tm, ff=ff, cols=cols)
    resident = lambda shape: pl.BlockSpec((None,) + shape, lambda i: (layer, 0, 0),
                                          pipeline_mode=pl.Buffered(1))
    n_layers = w_in.shape[0]
    conv_b = conv_b.reshape(n_layers, 1, 2 * ff)
    g = g.reshape(n_layers, 1, d)
    b = b.reshape(n_layers, 1, d)
    return pl.pallas_call(
        body,
        grid=(m // tm,),
        in_specs=[pl.BlockSpec((tm, d), lambda i: (i, 0)),
                  pl.BlockSpec((halo, d), lambda i: (jnp.maximum(i * (tm // halo) - 1, 0), 0)),
                  resident((d, 2 * ff)),
                  resident((3, 2 * ff)),
                  resident((1, 2 * ff)),
                  resident((ff, d)),
                  resident((1, d)),
                  resident((1, d))],
        out_specs=pl.BlockSpec((tm, d), lambda i: (i, 0)),
        out_shape=jax.ShapeDtypeStruct((m, d), F32),
        scratch_shapes=[pltpu.VMEM((tm + halo, d), BF16),
                        pltpu.VMEM((tm, d), F32),
                        pltpu.VMEM((2, tm + halo, cols), F32),
                        pltpu.VMEM((2, tm + halo, cols), F32),
                        pltpu.VMEM((tm, ff), BF16)],
        compiler_params=_cparams("parallel"),
        name="conv_ffn_ln",
    )(x, x, w_in, conv_w, conv_b, w_out, g, b)


def _head_pair_sum(v):
    head1 = lax.broadcasted_iota(jnp.int32, v.shape, 1) >= RWKV_HEAD
    total = jnp.sum(v, axis=1, keepdims=True)
    right = jnp.sum(jnp.where(head1, v, 0.0), axis=1, keepdims=True)
    return jnp.where(head1, right, total - right)


def _rwkv_proj_body(x_ref, xp_ref, mix_ref, wrkv_ref, w1_ref, w2_ref, a1_ref, a2_ref, g1_ref, g2_ref,
                    w0_ref, a0_ref, kk_ref, ka_ref,
                    r_out, ld_out, k_out, v_out, kk_out, b_out, gate_out, *, tm, tiles_per_seq):
    i = pl.program_id(0)
    d = x_ref.shape[1]
    x = x_ref[...]
    first = (i % tiles_per_seq) == 0
    prev = jnp.where(first, 0.0, xp_ref[SUBLANES - 1:SUBLANES, :])
    row = lax.broadcasted_iota(jnp.int32, (tm, d), 0)
    xs = jnp.where(row == 0, prev, pltpu.roll(x, 1, 0))
    xx = xs - x

    x_b = x.astype(BF16)
    xx_b = xx.astype(BF16)

    def mixed(j):
        return x_b + xx_b * mix_ref[j:j + 1, :].astype(BF16)

    def mm(a, w):
        return jnp.dot(a, w, preferred_element_type=F32)

    r = mm(mixed(0), wrkv_ref[0])
    k = mm(mixed(2), wrkv_ref[1])
    v = mm(mixed(3), wrkv_ref[2])
    lw = w0_ref[...] + mm(jnp.tanh(mm(mixed(1), w1_ref[...])).astype(BF16), w2_ref[...])
    softplus_neg = jnp.maximum(-lw, 0.0) + jnp.log(1.0 + jnp.exp(-jnp.abs(lw)))
    log_decay = -jnp.exp(-softplus_neg - 0.5)
    a = _sigmoid(a0_ref[...] + mm(mm(mixed(4), a1_ref[...]).astype(BF16), a2_ref[...]))
    gate = mm(_sigmoid(mm(mixed(5), g1_ref[...])).astype(BF16), g2_ref[...])

    kk = k * kk_ref[...]
    sq = kk * kk
    ss = jnp.concatenate([_head_pair_sum(sq[:, j * LANES:(j + 1) * LANES]) for j in range(d // LANES)], axis=1)
    kk = kk / jnp.maximum(jnp.sqrt(ss), 1e-12)

    r_out[...] = r
    ld_out[...] = log_decay
    k_out[...] = k * (1.0 + (a - 1.0) * ka_ref[...])
    v_out[...] = v
    kk_out[...] = kk
    b_out[...] = kk * a
    gate_out[...] = gate


def _rwkv_proj(x, seq_len, mix, w_rkv, w0, w1, w2, a0, a1, a2, g1, g2, k_k, k_a, *, tm):
    m, d = x.shape
    full = lambda arr: pl.BlockSpec(arr.shape, lambda i: (0,) * arr.ndim, pipeline_mode=pl.Buffered(1))
    vec = lambda a: a.reshape(1, d)
    args = (mix, w_rkv.astype(BF16), w1.astype(BF16), w2.astype(BF16), a1.astype(BF16), a2.astype(BF16),
            g1.astype(BF16), g2.astype(BF16), vec(w0), vec(a0), vec(k_k), vec(k_a))
    row_spec = pl.BlockSpec((tm, d), lambda i: (i, 0))
    return pl.pallas_call(
        functools.partial(_rwkv_proj_body, tm=tm, tiles_per_seq=seq_len // tm),
        grid=(m // tm,),
        in_specs=[row_spec,
                  pl.BlockSpec((SUBLANES, d), lambda i: (jnp.maximum(i * (tm // SUBLANES) - 1, 0), 0))]
                 + [full(a) for a in args],
        out_specs=[row_spec] * 7,
        out_shape=[jax.ShapeDtypeStruct((m, d), F32)] * 7,
        compiler_params=_cparams("parallel"),
        name="rwkv_proj",
    )(x, x, *args)


def _bf16_dot(a, b, dims):
    return lax.dot_general(a.astype(BF16), b.astype(BF16), dims, preferred_element_type=F32)


def _head_sums(tiles):
    return [_head_pair_sum(t) for t in tiles]


def _rwkv_chunk_body(r_ref, ld_ref, k_ref, v_ref, kk_ref, b_ref, gate_ref, rk_ref, gg_ref, gb_ref,
                     o_ref, st_ref, *, chunk, pairs, subs):
    c = pl.program_id(2)

    @pl.when(c == 0)
    def _():
        st_ref[...] = jnp.zeros_like(st_ref)

    n = RWKV_HEAD
    row = lax.broadcasted_iota(jnp.int32, (chunk, LANES), 0)
    lane = lax.broadcasted_iota(jnp.int32, (chunk, LANES), 1)
    col = lane % n
    eye2 = jnp.where(row == col, 1.0, 0.0)
    head1 = lane >= n
    row2 = lax.broadcasted_iota(jnp.int32, (2 * chunk, LANES), 0)
    col2 = lax.broadcasted_iota(jnp.int32, (2 * chunk, LANES), 1) % n
    tri2 = jnp.where(row2 < chunk, row2 - 1, row2 - chunk) >= col2
    rr = lax.broadcasted_iota(jnp.int32, (LANES, LANES), 0) // n
    cc = lax.broadcasted_iota(jnp.int32, (LANES, LANES), 1) // n
    bd = rr == cc
    nn = (((1,), (0,)), ((), ()))

    def head_stack(t):
        return jnp.concatenate([jnp.where(head1, 0.0, t), jnp.where(head1, t, 0.0)], axis=0)

    def block_diag(t):
        return jnp.where(bd, jnp.concatenate([t, t], axis=0), 0.0)

    units = [(cc, p) for cc in range(subs) for p in range(pairs)]
    U = range(len(units))
    rows = [slice(cc * chunk, (cc + 1) * chunk) for cc, _ in units]
    sls = [slice(p * LANES, (p + 1) * LANES) for _, p in units]
    r = [r_ref[0, rows[i], sls[i]] for i in U]
    ld = [ld_ref[0, rows[i], sls[i]] for i in U]
    k = [k_ref[0, rows[i], sls[i]] for i in U]
    v = [v_ref[0, rows[i], sls[i]] for i in U]
    bv = [b_ref[0, rows[i], sls[i]] for i in U]

    cum = list(ld)
    shift = 1
    while shift < chunk:
        cum = [cu + jnp.where(row >= shift, pltpu.roll(cu, shift, 0), 0.0) for cu in cum]
        shift *= 2
    cum_last = [cu[chunk - 1:chunk, :] for cu in cum]
    e_inv = [jnp.exp(-cu) for cu in cum]
    r_hat = [r[i] * jnp.exp(cum[i]) for i in U]
    a_hat = [-(kk_ref[0, rows[i], sls[i]] * jnp.exp(cum[i] - ld[i])) for i in U]
    lhs = [jnp.concatenate([a_hat[i], r_hat[i]], axis=0) for i in U]
    sbk = [_bf16_dot(lhs[i], jnp.concatenate([head_stack(bv[i] * e_inv[i]), head_stack(k[i] * e_inv[i])], axis=0),
                     _NT) for i in U]
    sb = [jnp.where(tri2, t[:, :LANES], 0.0) for t in sbk]
    sk = [jnp.where(tri2, t[:, LANES:], 0.0) for t in sbk]
    a_ab = [t[:chunk] for t in sb]
    a_rb = [t[chunk:] for t in sb]

    xp = [_bf16_dot(t, block_diag(t), nn) for t in a_ab]
    tinv = [eye2 + t for t in a_ab]
    power = 2
    while 2 * power < chunk:
        both = [_bf16_dot(jnp.concatenate([xp[i], tinv[i]], axis=0), block_diag(xp[i]), nn) for i in U]
        xp = [t[:chunk] for t in both]
        tinv = [tinv[i] + both[i][chunk:] for i in U]
        power *= 2
    tinv = [tinv[i] + _bf16_dot(tinv[i], block_diag(xp[i]), nn) for i in U]

    state = [st_ref[p] for p in range(pairs)]
    y = [None] * len(units)
    for cc in range(subs):
        ids = [cc * pairs + p for p in range(pairs)]
        zy = [_bf16_dot(jnp.concatenate([lhs[i], sk[i]], axis=1),
                        jnp.concatenate([state[p].T, head_stack(v[i])], axis=0), nn)
              for p, i in enumerate(ids)]
        u = [_bf16_dot(tinv[i], head_stack(zy[p][:chunk]), nn) for p, i in enumerate(ids)]
        for p, i in enumerate(ids):
            y[i] = zy[p][chunk:] + _bf16_dot(a_rb[i], head_stack(u[p]), nn)
            e_last = jnp.exp(cum_last[i] - cum[i])
            s_new = state[p] * jnp.exp(cum_last[i]) + _bf16_dot(
                jnp.concatenate([u[p], v[i]], axis=0),
                jnp.concatenate([bv[i] * e_last, k[i] * e_last], axis=0), _TN)
            state[p] = jnp.where(bd, s_new, 0.0)
    for p in range(pairs):
        st_ref[p] = state[p]

    sums = _head_sums(y + [r[i] * k[i] * rk_ref[:, sls[i]] for i in U])
    dlt = [y[i] - sums[i] * (1.0 / n) for i in U]
    var = _head_sums([t * t for t in dlt])
    for i in U:
        yn = dlt[i] * lax.rsqrt(var[i] * (1.0 / n) + RWKV_GN_EPS) * gg_ref[:, sls[i]] + gb_ref[:, sls[i]]
        bonus = sums[len(units) + i] * v[i]
        o_ref[0, rows[i], sls[i]] = ((yn + bonus) * gate_ref[0, rows[i], sls[i]]).astype(o_ref.dtype)


def _rwkv_chunk(r, ld, k, v, kk, bvec, gate, r_k, gn_g, gn_b, *, pairs, subs):
    bsz, t_len, d = r.shape
    chunk = RWKV_HEAD
    width = pairs * LANES
    seq_spec = pl.BlockSpec((1, subs * chunk, width), lambda b, g, c: (b, c, g))
    vec_spec = pl.BlockSpec((1, width), lambda b, g, c: (0, g))
    return pl.pallas_call(
        functools.partial(_rwkv_chunk_body, chunk=chunk, pairs=pairs, subs=subs),
        grid=(bsz, d // width, t_len // (subs * chunk)),
        in_specs=[seq_spec] * 7 + [vec_spec] * 3,
        out_specs=seq_spec,
        out_shape=jax.ShapeDtypeStruct((bsz, t_len, d), BF16),
        scratch_shapes=[pltpu.VMEM((pairs, LANES, LANES), F32)],
        compiler_params=_cparams("parallel", "parallel", "arbitrary"),
        name="rwkv_chunk",
    )(r, ld, k, v, kk, bvec, gate, r_k.reshape(1, d), gn_g.reshape(1, d), gn_b.reshape(1, d))


def _rwkv_mixer(x, bsz, t_len, mix, w_rkv, w0, w1, w2, a0, a1, a2, g1, g2, k_k, k_a, r_k, gn_g, gn_b, w_o,
                ln_g, ln_b):
    d = x.shape[1]
    outs = _rwkv_proj(x, t_len, mix, w_rkv, w0, w1, w2, a0, a1, a2, g1, g2, k_k, k_a, tm=RWKV_PROJ_ROWS)
    seq = [o.reshape(bsz, t_len, d) for o in outs]
    o = _rwkv_chunk(*seq, r_k, gn_g, gn_b, pairs=RWKV_HEAD_PAIRS, subs=RWKV_SUBCHUNKS)
    return _matmul_residual_ln(o.reshape(bsz * t_len, d), w_o.astype(BF16), x, ln_g, ln_b, tm=OUT_PROJ_ROWS)


def _ret_chunk_body(q_ref, k_ref, v_ref, gate_ref, cos_ref, sin_ref, g_ref, b_ref, o_ref, st_ref, *, chunk):
    c = pl.program_id(1)

    @pl.when(c == 0)
    def _():
        st_ref[...] = jnp.zeros_like(st_ref)

    n_heads, dk, dv = st_ref.shape
    half = dk // 2
    cos = cos_ref[...]
    sin = sin_ref[...]

    def rotate(t):
        te, to = t[:, :half], t[:, half:]
        return jnp.concatenate([te * cos - to * sin, to * cos + te * sin], axis=1)

    ri = lax.broadcasted_iota(jnp.int32, (chunk, chunk), 0)
    ci = lax.broadcasted_iota(jnp.int32, (chunk, chunk), 1)
    rel = (ri - ci).astype(F32)
    pos = lax.broadcasted_iota(jnp.int32, (chunk, dk), 0).astype(F32)

    for h in range(n_heads):
        log_gamma = math.log(1.0 - 2.0 ** (-5.0 - h))
        inner = jnp.where(rel >= 0, jnp.exp(log_gamma * jnp.maximum(rel, 0.0)), 0.0)
        cross = jnp.exp(log_gamma * (pos + 1.0))
        sdecay = jnp.exp(log_gamma * (chunk - 1.0 - pos))
        chunk_decay = jnp.exp(log_gamma * chunk)

        q = rotate(q_ref[0, :, h * dk:(h + 1) * dk].astype(F32))
        k = rotate(k_ref[0, :, h * dk:(h + 1) * dk].astype(F32)) * (dk ** -0.5)
        v = v_ref[0, :, h * dv:(h + 1) * dv].astype(BF16)
        s = lax.dot_general(q.astype(BF16), k.astype(BF16), _NT, preferred_element_type=F32) * inner
        st = st_ref[h]
        o = (jnp.dot(s.astype(BF16), v, preferred_element_type=F32)
             + jnp.dot((q * cross).astype(BF16), st.astype(BF16), preferred_element_type=F32))
        st_ref[h] = st * chunk_decay + lax.dot_general((k * sdecay).astype(BF16), v, _TN,
                                                       preferred_element_type=F32)

        mu = jnp.mean(o, axis=-1, keepdims=True)
        dlt = o - mu
        var = jnp.mean(dlt * dlt, axis=-1, keepdims=True)
        vs = slice(h * dv, (h + 1) * dv)
        on = dlt * lax.rsqrt(var + RET_GN_EPS) * g_ref[:, vs] + b_ref[:, vs]
        gt = gate_ref[0, :, vs].astype(F32)
        o_ref[0, :, vs] = (gt * _sigmoid(gt) * on).astype(o_ref.dtype)


def _ret_chunk(proj, cos, sin, gn_g, gn_b, *, chunk):
    bsz, t_len, six_d = proj.shape
    d = six_d // 6
    h = RET_HEADS
    dk, dv = d // h, 2 * d // h
    return pl.pallas_call(
        functools.partial(_ret_chunk_body, chunk=chunk),
        grid=(bsz, t_len // chunk),
        in_specs=[pl.BlockSpec((1, chunk, d), lambda b, c: (b, c, 0)),
                  pl.BlockSpec((1, chunk, d), lambda b, c: (b, c, 1)),
                  pl.BlockSpec((1, chunk, 2 * d), lambda b, c: (b, c, 1)),
                  pl.BlockSpec((1, chunk, 2 * d), lambda b, c: (b, c, 2)),
                  pl.BlockSpec((chunk, dk // 2), lambda b, c: (c, 0)),
                  pl.BlockSpec((chunk, dk // 2), lambda b, c: (c, 0)),
                  pl.BlockSpec((1, 2 * d), lambda b, c: (0, 0)),
                  pl.BlockSpec((1, 2 * d), lambda b, c: (0, 0))],
        out_specs=pl.BlockSpec((1, chunk, 2 * d), lambda b, c: (b, c, 0)),
        out_shape=jax.ShapeDtypeStruct((bsz, t_len, 2 * d), BF16),
        scratch_shapes=[pltpu.VMEM((h, dk, dv), F32)],
        compiler_params=_cparams("parallel", "arbitrary"),
        name="retention_chunk",
    )(proj, proj, proj, proj, cos, sin, gn_g.reshape(1, 2 * d), gn_b.reshape(1, 2 * d))


def _retention_mixer(x, bsz, t_len, w_in, gn_g, gn_b, w_o, ln_g, ln_b):
    d = x.shape[1]
    h = RET_HEADS
    dk = d // h
    w_qk = w_in[:, :2 * d].reshape(d, 2 * h, dk // 2, 2).transpose(0, 1, 3, 2).reshape(d, 2 * d)
    w_perm = jnp.concatenate([w_qk, w_in[:, 2 * d:]], axis=1).astype(BF16)
    proj = _matmul(x, w_perm, tm=PROJ_ROWS, tn=PROJ_COLS, out_dtype=BF16)
    inv = 1.0 / (RET_ROPE_BASE ** jnp.linspace(0.0, 1.0, dk // 2, dtype=F32))
    ang = jnp.arange(t_len, dtype=F32)[:, None] * inv[None, :]
    o = _ret_chunk(proj.reshape(bsz, t_len, 6 * d), jnp.cos(ang), jnp.sin(ang), gn_g, gn_b, chunk=RET_CHUNK)
    return _matmul_residual_ln(o.reshape(bsz * t_len, 2 * d), w_o.astype(BF16), x, ln_g, ln_b,
                               tm=OUT_PROJ_ROWS)


def _moba_body(qt_ref, vt_ref, k_ref, o_ref, kmean_ref, va_ref, bias_ref, *, n_blk, pairs):
    qb = pl.program_id(2)
    blk = MOBA_BLOCK
    hd = MOBA_HEAD_DIM
    heads = 2 * pairs
    aug = hd + BF16_ROWS
    t_len = n_blk * blk
    H = range(heads)

    @pl.when(qb == 0)
    def _():
        kf = k_ref[...].astype(F32)
        kmean_ref[...] = jnp.mean(kf.reshape(n_blk, blk, pairs * LANES), axis=1)
        ones = jnp.ones((BF16_ROWS, t_len), BF16)
        for h in H:
            va_ref[h, 0:hd, :] = vt_ref[h * hd:(h + 1) * hd, :]
            va_ref[h, hd:aug, :] = ones

    kmean = kmean_ref[...]
    qt = qt_ref[...].astype(F32)
    zeros = jnp.zeros((hd, blk), F32)
    blk_id = lax.broadcasted_iota(jnp.int32, (n_blk, blk), 0)
    scale = hd ** -0.5

    qz_b = []
    for h in H:
        qh = qt[h * hd:(h + 1) * hd]
        qz = jnp.concatenate([qh, zeros] if h % 2 == 0 else [zeros, qh], axis=0)
        lanes = slice((h // 2) * LANES, (h // 2 + 1) * LANES)
        gate = jnp.dot(kmean[:, lanes], qz, precision=HIGHEST, preferred_element_type=F32)
        beaten = jnp.zeros((n_blk, blk), F32)
        for m in range(n_blk):
            gm = gate[m:m + 1, :]
            wins = jnp.where(gm > gate, 1.0, jnp.where(gm == gate, jnp.where(blk_id > m, 1.0, 0.0), 0.0))
            beaten = beaten + jnp.where(m < qb, wins, 0.0)
        bias_ref[h] = jnp.where(blk_id < qb, jnp.where(beaten < MOBA_TOPK, 0.0, NEG_BIG), NEG_BIG)
        qz_b.append((qz * (scale * LOG2_E)).astype(BF16))

    def scores(off):
        kb = k_ref[pl.ds(off, blk), :]
        return tuple(jnp.dot(kb[:, (h // 2) * LANES:(h // 2 + 1) * LANES], qz_b[h],
                             preferred_element_type=F32) for h in H)

    def accumulate(off, s, m_new, shift, carry):
        p = [jnp.exp2(s[h] - shift[h]).astype(BF16) for h in H]
        alpha = [jnp.exp2(carry[h][0] - m_new[h]) for h in H]
        acc = [alpha[h] * carry[h][1]
               + jnp.dot(va_ref[h, :, pl.ds(off, blk)], p[h], preferred_element_type=F32) for h in H]
        return tuple((m_new[h], acc[h]) for h in H)

    def attend(off, s, carry):
        m_new = [jnp.maximum(carry[h][0], jnp.max(s[h], axis=0, keepdims=True)) for h in H]
        return accumulate(off, s, m_new, m_new, carry)

    def past_block(nb, carry):
        off = pl.multiple_of(nb * blk, blk)
        s = scores(off)
        chosen = [bias_ref[h, pl.ds(nb, 1), :] >= 0.0 for h in H]
        m_new = [jnp.where(chosen[h], jnp.maximum(carry[h][0], jnp.max(s[h], axis=0, keepdims=True)),
                           carry[h][0]) for h in H]
        shift = [jnp.where(chosen[h], m_new[h], -NEG_BIG) for h in H]
        return accumulate(off, s, m_new, shift, carry)

    init = tuple((jnp.full((1, blk), NEG_BIG, F32), jnp.zeros((aug, blk), F32)) for _ in H)
    carry = lax.fori_loop(0, qb, past_block, init)
    key_i = lax.broadcasted_iota(jnp.int32, (blk, blk), 0)
    qry_i = lax.broadcasted_iota(jnp.int32, (blk, blk), 1)
    causal_bias = jnp.where(key_i <= qry_i, 0.0, NEG_BIG)
    own = pl.multiple_of(qb * blk, blk)
    s_own = scores(own)
    res = attend(own, [s_own[h] + causal_bias for h in H], carry)
    for h in H:
        acc = res[h][1]
        o_ref[h * hd:(h + 1) * hd, :] = acc[:hd] / acc[hd:hd + 1]


def _moba_attention(qvt, k, bsz, t_len, *, pairs):
    d = k.shape[1]
    blk = MOBA_BLOCK
    n_blk = t_len // blk
    width = pairs * LANES
    n_grp = d // width
    heads = 2 * pairs
    return pl.pallas_call(
        functools.partial(_moba_body, n_blk=n_blk, pairs=pairs),
        grid=(bsz, n_grp, n_blk),
        in_specs=[pl.BlockSpec((width, blk), lambda b, p, q: (p, b * n_blk + q)),
                  pl.BlockSpec((width, t_len), lambda b, p, q: (n_grp + p, b)),
                  pl.BlockSpec((t_len, width), lambda b, p, q: (b, p))],
        out_specs=pl.BlockSpec((width, blk), lambda b, p, q: (p, b * n_blk + q)),
        out_shape=jax.ShapeDtypeStruct((d, bsz * t_len), F32),
        scratch_shapes=[pltpu.VMEM((n_blk, width), F32),
                        pltpu.VMEM((heads, MOBA_HEAD_DIM + BF16_ROWS, t_len), BF16),
                        pltpu.VMEM((heads, n_blk, blk), F32)],
        compiler_params=_cparams("parallel", "parallel", "arbitrary"),
        name="moba_attention",
    )(qvt, qvt, k)


def _moba_mixer(x, bsz, t_len, w_qkv, w_o, ln_g, ln_b):
    d = x.shape[1]
    assert t_len % MOBA_BLOCK == 0 and d == MOBA_HEADS * MOBA_HEAD_DIM
    w_k = w_qkv[:, d:2 * d].astype(BF16)
    w_qv_t = jnp.concatenate([w_qkv[:, :d], w_qkv[:, 2 * d:]], axis=1).T.astype(BF16)
    k = _matmul(x, w_k, tm=PROJ_ROWS, tn=PROJ_COLS, out_dtype=BF16)
    qvt = _matmul_nt(x, w_qv_t, tm=PROJ_ROWS, tn=PROJ_COLS, out_dtype=BF16)
    ot = _moba_attention(qvt, k, bsz, t_len, pairs=MOBA_HEAD_PAIRS)
    return _matmul_residual_ln(ot, w_o.astype(BF16), x, ln_g, ln_b, tm=OUT_PROJ_ROWS, lhs_transposed=True)


def kernel(x, rwkv_mix, rwkv_w_rkv, rwkv_w0, rwkv_w1, rwkv_w2, rwkv_a0, rwkv_a1, rwkv_a2, rwkv_g1, rwkv_g2,
           rwkv_k_k, rwkv_k_a, rwkv_r_k, rwkv_gn_g, rwkv_gn_b, rwkv_w_o, ret_w_in, ret_gn_g, ret_gn_b, ret_w_o,
           moba_w_qkv, moba_w_o, ffn_w_in, ffn_conv_w, ffn_conv_b, ffn_w_out, ln1_g, ln1_b, ln2_g, ln2_b):
    bsz, t_len, d = x.shape
    h = x.reshape(bsz * t_len, d)
    ffn_w_in_b = ffn_w_in.astype(BF16)
    ffn_w_out_b = ffn_w_out.astype(BF16)
    for i in range(DEPTH):
        kind, j = i % N_MIXERS, i // N_MIXERS
        if kind == 0:
            h = _rwkv_mixer(h, bsz, t_len, rwkv_mix[j], rwkv_w_rkv[j], rwkv_w0[j], rwkv_w1[j], rwkv_w2[j],
                            rwkv_a0[j], rwkv_a1[j], rwkv_a2[j], rwkv_g1[j], rwkv_g2[j], rwkv_k_k[j],
                            rwkv_k_a[j], rwkv_r_k[j], rwkv_gn_g[j], rwkv_gn_b[j], rwkv_w_o[j],
                            ln1_g[i], ln1_b[i])
        elif kind == 1:
            h = _retention_mixer(h, bsz, t_len, ret_w_in[j], ret_gn_g[j], ret_gn_b[j], ret_w_o[j],
                                 ln1_g[i], ln1_b[i])
        else:
            h = _moba_mixer(h, bsz, t_len, moba_w_qkv[j], moba_w_o[j], ln1_g[i], ln1_b[i])
        h = _conv_ffn_ln(h, t_len, i, ffn_w_in_b, ffn_conv_w, ffn_conv_b, ffn_w_out_b, ln2_g, ln2_b,
                         tm=FFN_ROWS, cols=FFN_COL_SLICE)
    return h.reshape(bsz, t_len, d)
```
